```python
import math
import jax, jax.numpy as jnp
from jax import lax
import numpy as np

D_MODEL = 2048
BATCH = 2
SEQ = 4096
DEPTH = 2

N_Q_HEADS = 16
N_KV_HEADS = 4
HEAD_DIM = 64
GROUP = N_Q_HEADS // N_KV_HEADS
ATTN_WIDTH = N_Q_HEADS * HEAD_DIM
KV_WIDTH = N_KV_HEADS * HEAD_DIM
WINDOW = 128
BLOCK = WINDOW
N_BUCKETS = 32
MAX_DISTANCE = 128
RNN_WIDTH = D_MODEL
RNN_BLOCKS = 16
RNN_BLOCK_DIM = RNN_WIDTH // RNN_BLOCKS
CONV_WIDTH = 4
LRU_C = 8.0
SPLITS = (ATTN_WIDTH, KV_WIDTH, KV_WIDTH, ATTN_WIDTH, RNN_WIDTH, RNN_WIDTH, D_MODEL, D_MODEL)
IN_WIDTH = sum(SPLITS)
SPLIT_POINTS = np.cumsum(SPLITS)[:-1].tolist()
DEEPNORM_ALPHA = (2.0 * DEPTH) ** 0.25
DEEPNORM_BETA = (8.0 * DEPTH) ** -0.25
LN_EPS = 1e-5
NEG_INF = -1e30

kernel_name = "hybrid_swa_sink_rglru_gated_deepnorm"


def t5_causal_bucket(dist):
    max_exact = N_BUCKETS // 2
    n = jnp.maximum(dist, 0)
    nf = jnp.maximum(n, max_exact).astype(jnp.float32)
    large = max_exact + (jnp.log(nf / max_exact) / math.log(MAX_DISTANCE / max_exact)
                         * (N_BUCKETS - max_exact)).astype(jnp.int32)
    large = jnp.minimum(large, N_BUCKETS - 1)
    return jnp.where(n < max_exact, n, large)


def layer_norm(x, g, b):
    xf = x.astype(jnp.float32)
    mu = xf.mean(-1, keepdims=True)
    var = jnp.square(xf - mu).mean(-1, keepdims=True)
    return ((xf - mu) * lax.rsqrt(var + LN_EPS) * g.astype(jnp.float32)
            + b.astype(jnp.float32)).astype(x.dtype)


def sliding_window_attention(q, k, v, sinks, rel_bias):
    B, S = q.shape[:2]
    nb = S // BLOCK
    qb = q.reshape(B, nb, BLOCK, N_KV_HEADS, GROUP, HEAD_DIM)
    pad = jnp.zeros((B, BLOCK, N_KV_HEADS, HEAD_DIM), k.dtype)

    def band(t):
        tp = jnp.concatenate([pad, t], axis=1).reshape(B, nb + 1, BLOCK, N_KV_HEADS, HEAD_DIM)
        return jnp.concatenate([tp[:, :-1], tp[:, 1:]], axis=2)

    kband, vband = band(k), band(v)
    s = jnp.einsum('bnqkgd,bnskd->bnkgqs', qb, kband).astype(jnp.float32) * (HEAD_DIM ** -0.5)

    qi = jnp.arange(BLOCK)[:, None]
    kj = jnp.arange(2 * BLOCK)[None, :]
    dist = qi + BLOCK - kj
    bias = rel_bias[t5_causal_bucket(dist)].astype(jnp.float32)
    bias = bias.transpose(2, 0, 1).reshape(N_KV_HEADS, GROUP, BLOCK, 2 * BLOCK)
    in_window = (dist >= 0) & (dist < WINDOW)
    key_pos = jnp.arange(nb)[:, None] * BLOCK + kj - BLOCK
    valid = in_window[None] & (key_pos >= 0)[:, None, :]
    s = jnp.where(valid[None, :, None, None], s + bias, NEG_INF)

    sink = sinks.astype(jnp.float32).reshape(N_KV_HEADS, GROUP)[None, None, :, :, None, None]
    m = jnp.maximum(s.max(-1, keepdims=True), sink)
    p = jnp.exp(s - m)
    denom = p.sum(-1, keepdims=True) + jnp.exp(sink - m)
    p = (p / denom).astype(v.dtype)
    o = jnp.einsum('bnkgqs,bnskd->bnqkgd', p, vband)
    return o.reshape(B, S, ATTN_WIDTH)


def causal_depthwise_conv(x, w, b):
    y = lax.conv_general_dilated(x, w[:, None, :], window_strides=(1,),
                                 padding=[(CONV_WIDTH - 1, 0)],
                                 dimension_numbers=('NWC', 'WIO', 'NWC'),
                                 feature_group_count=x.shape[-1])
    return y + b


def rg_lru(x, w_r, b_r, w_i, b_i, lam):
    B, S, _ = x.shape
    xb = x.reshape(B, S, RNN_BLOCKS, RNN_BLOCK_DIM)
    r = jax.nn.sigmoid(jnp.einsum('bshi,hij->bshj', xb, w_r).reshape(B, S, RNN_WIDTH) + b_r)
    i = jax.nn.sigmoid(jnp.einsum('bshi,hij->bshj', xb, w_i).reshape(B, S, RNN_WIDTH) + b_i)
    log_a = -LRU_C * r.astype(jnp.float32) * jax.nn.softplus(-lam.astype(jnp.float32))
    a = jnp.exp(log_a)
    u = jnp.sqrt(-jnp.expm1(2.0 * log_a)) * (i * x).astype(jnp.float32)

    def combine(left, right):
        a_l, h_l = left
        a_r, h_r = right
        return a_l * a_r, a_r * h_l + h_r

    _, h = lax.associative_scan(combine, (a, u), axis=1)
    return h.astype(x.dtype)


def hybrid_layer(x, w_in, conv_w, conv_b, w_r, b_r, w_i, b_i, lru_lambda, sinks,
                 w_attn_out, w_rnn_out, w_out, ln_g, ln_b, rel_bias):
    B, S, _ = x.shape
    proj = x @ w_in
    q, k, v, g_attn, x_rnn, g_rnn, m_attn, m_rnn = jnp.split(proj, SPLIT_POINTS, axis=-1)
    attn = sliding_window_attention(q.reshape(B, S, N_Q_HEADS, HEAD_DIM),
                                    k.reshape(B, S, N_KV_HEADS, HEAD_DIM),
                                    v.reshape(B, S, N_KV_HEADS, HEAD_DIM),
                                    sinks, rel_bias)
    y_attn = (attn * jax.nn.silu(g_attn)) @ w_attn_out
    h = rg_lru(causal_depthwise_conv(x_rnn, conv_w, conv_b), w_r, b_r, w_i, b_i, lru_lambda)
    y_rnn = (h * jax.nn.silu(g_rnn)) @ w_rnn_out
    mixed = jax.nn.sigmoid(m_attn) * y_attn + jax.nn.sigmoid(m_rnn) * y_rnn
    out = mixed @ w_out
    return layer_norm(DEEPNORM_ALPHA * x + out, ln_g, ln_b)


def setup_inputs(seed: int = 0) -> dict:
    key = jax.random.key(seed)
    ks = jax.random.split(key, 16)
    f32 = jnp.float32
    x = jax.random.normal(ks[0], (BATCH, SEQ, D_MODEL), f32)
    w_in = jax.random.normal(ks[1], (DEPTH, D_MODEL, IN_WIDTH), f32) * D_MODEL ** -0.5
    conv_w = jax.random.normal(ks[2], (DEPTH, CONV_WIDTH, RNN_WIDTH), f32) * CONV_WIDTH ** -0.5
    conv_b = jax.random.normal(ks[3], (DEPTH, RNN_WIDTH), f32) * 0.01
    w_r = jax.random.normal(ks[4], (DEPTH, RNN_BLOCKS, RNN_BLOCK_DIM, RNN_BLOCK_DIM), f32) * RNN_BLOCK_DIM ** -0.5
    b_r = jax.random.normal(ks[5], (DEPTH, RNN_WIDTH), f32) * 0.01
    w_i = jax.random.normal(ks[6], (DEPTH, RNN_BLOCKS, RNN_BLOCK_DIM, RNN_BLOCK_DIM), f32) * RNN_BLOCK_DIM ** -0.5
    b_i = jax.random.normal(ks[7], (DEPTH, RNN_WIDTH), f32) * 0.01
    u = jax.random.uniform(ks[8], (DEPTH, RNN_WIDTH), f32, minval=0.9, maxval=0.999)
    a = u ** (1.0 / LRU_C)
    lru_lambda = jnp.log(a) - jnp.log1p(-a)
    sinks = jax.random.normal(ks[9], (DEPTH, N_Q_HEADS), f32) * 0.5
    w_attn_out = jax.random.normal(ks[10], (DEPTH, ATTN_WIDTH, D_MODEL), f32) * ATTN_WIDTH ** -0.5
    w_rnn_out = jax.random.normal(ks[11], (DEPTH, RNN_WIDTH, D_MODEL), f32) * RNN_WIDTH ** -0.5
    w_out = jax.random.normal(ks[12], (DEPTH, D_MODEL, D_MODEL), f32) * (D_MODEL ** -0.5) * DEEPNORM_BETA
    ln_g = 1.0 + 0.02 * jax.random.normal(ks[13], (DEPTH, D_MODEL), f32)
    ln_b = 0.02 * jax.random.normal(ks[14], (DEPTH, D_MODEL), f32)
    rel_bias = jax.random.normal(ks[15], (N_BUCKETS, N_Q_HEADS), f32) * 0.5
    return {"x": x, "w_in": w_in, "conv_w": conv_w, "conv_b": conv_b, "w_r": w_r, "b_r": b_r,
            "w_i": w_i, "b_i": b_i, "lru_lambda": lru_lambda, "sinks": sinks,
            "w_attn_out": w_attn_out, "w_rnn_out": w_rnn_out, "w_out": w_out,
            "ln_g": ln_g, "ln_b": ln_b, "rel_bias": rel_bias}


def reference(x, w_in, conv_w, conv_b, w_r, b_r, w_i, b_i, lru_lambda, sinks,
              w_attn_out, w_rnn_out, w_out, ln_g, ln_b, rel_bias):
    for l in range(DEPTH):
        x = hybrid_layer(x, w_in[l], conv_w[l], conv_b[l], w_r[l], b_r[l], w_i[l], b_i[l],
                         lru_lambda[l], sinks[l], w_attn_out[l], w_rnn_out[l], w_out[l],
                         ln_g[l], ln_b[l], rel_bias)
    return x
```

```python
import functools
import math

import numpy as np
import jax
import jax.numpy as jnp
from jax import lax
from jax.experimental import pallas as pl
from jax.experimental.pallas import tpu as pltpu

D_MODEL = 2048
N_Q_HEADS = 16
N_KV_HEADS = 4
HEAD_DIM = 64
ATTN_WIDTH = N_Q_HEADS * HEAD_DIM
KV_WIDTH = N_KV_HEADS * HEAD_DIM
WINDOW = 128
N_BUCKETS = 32
MAX_DISTANCE = 128
RNN_WIDTH = D_MODEL
RNN_BLOCKS = 16
RNN_BLOCK_DIM = RNN_WIDTH // RNN_BLOCKS
CONV_WIDTH = 4
LRU_C = 8.0
LN_EPS = 1e-5
NEG_INF = -1e30

Q0 = 0
K0 = Q0 + ATTN_WIDTH
V0 = K0 + KV_WIDTH
GA0 = V0 + KV_WIDTH
XR0 = GA0 + ATTN_WIDTH
GR0 = XR0 + RNN_WIDTH
MA0 = GR0 + RNN_WIDTH
MR0 = MA0 + D_MODEL
IN_WIDTH = MR0 + D_MODEL

LANES = 128
SUBLANES = 8
VMEM_LIMIT = 60 * 1024 * 1024

PROJ_TM = 1024
PROJ_TN = 1536
MIX_TM = 256


def _t5_bucket_table():
    qi = np.arange(WINDOW)[:, None]
    kj = np.arange(2 * WINDOW)[None, :]
    dist = qi + WINDOW - kj
    max_exact = N_BUCKETS // 2
    n = np.maximum(dist, 0)
    nf = np.maximum(n, max_exact).astype(np.float32)
    large = max_exact + (np.log(nf / np.float32(max_exact)) / np.float32(math.log(MAX_DISTANCE / max_exact))
                         * np.float32(N_BUCKETS - max_exact)).astype(np.int32)
    large = np.minimum(large, N_BUCKETS - 1)
    bucket = np.where(n < max_exact, n, large)
    in_window = (dist >= 0) & (dist < WINDOW)
    return np.where(in_window, bucket, -1).astype(np.int32)


def _sigmoid(x):
    return 0.5 * jnp.tanh(0.5 * x) + 0.5


def _inproj_kernel(x_ref, w_ref, o_ref):
    o_ref[...] = jnp.dot(x_ref[...].astype(jnp.bfloat16), w_ref[...],
                         preferred_element_type=jnp.float32).astype(o_ref.dtype)


def _in_projection(x2d, w_bf16):
    m, k = x2d.shape
    n = w_bf16.shape[1]
    grid = (n // PROJ_TN, m // PROJ_TM)
    return pl.pallas_call(
        _inproj_kernel,
        grid=grid,
        in_specs=[pl.BlockSpec((PROJ_TM, k), lambda j, i: (i, 0)),
                  pl.BlockSpec((k, PROJ_TN), lambda j, i: (0, j))],
        out_specs=pl.BlockSpec((PROJ_TM, PROJ_TN), lambda j, i: (i, j)),
        out_shape=jax.ShapeDtypeStruct((m, n), jnp.bfloat16),
        compiler_params=pltpu.CompilerParams(
            dimension_semantics=("arbitrary", "arbitrary"), vmem_limit_bytes=VMEM_LIMIT),
        name="in_projection",
    )(x2d, w_bf16)


def _mixer_kernel(relb_ref, sinks_ref,
                  proj_ref, x_ref, bucket_ref, wri_ref, bri_ref, lam_ref, convw_ref, convb_ref,
                  wao_ref, wro_ref, wo_ref, lng_ref, lnb_ref,
                  out_ref,
                  bias_scr, kx_scr, vx_scr, attn_scr, hist_scr, a_scr, u_scr, hcar_scr,
                  *, alpha):
    tm = MIX_TM
    nqb = tm // WINDOW
    s_idx = pl.program_id(1)
    first_tile = s_idx == 0
    f32 = jnp.float32
    bf16 = jnp.bfloat16

    @pl.when((pl.program_id(0) == 0) & first_tile)
    def _():
        bucket = bucket_ref[...]
        for h in range(N_Q_HEADS):
            acc = jnp.full((WINDOW, 2 * WINDOW), NEG_INF, f32)
            for b in range(N_BUCKETS):
                acc = jnp.where(bucket == b, relb_ref[b, h], acc)
            bias_scr[h] = acc

    @pl.when(first_tile)
    def _():
        kx_scr[:, 0:WINDOW, :] = jnp.zeros((8, WINDOW, LANES), bf16)
        vx_scr[:, 0:WINDOW, :] = jnp.zeros((8, WINDOW, LANES), bf16)
        hist_scr[0:SUBLANES, :] = jnp.zeros((SUBLANES, RNN_WIDTH), f32)
        hcar_scr[...] = jnp.zeros((1, RNN_WIDTH), f32)

    @pl.when(jnp.logical_not(first_tile))
    def _():
        kx_scr[:, 0:WINDOW, :] = kx_scr[:, tm:tm + WINDOW, :]
        vx_scr[:, 0:WINDOW, :] = vx_scr[:, tm:tm + WINDOW, :]
        hist_scr[0:SUBLANES, :] = hist_scr[tm:tm + SUBLANES, :]

    lane = lax.broadcasted_iota(jnp.int32, (tm, LANES), 1)
    lo = lane < HEAD_DIM
    for p in range(N_KV_HEADS // 2):
        for src0, scr in ((K0, kx_scr), (V0, vx_scr)):
            t = proj_ref[:, src0 + p * LANES: src0 + (p + 1) * LANES].astype(f32)
            tr = pltpu.roll(t, HEAD_DIM, axis=1)
            zero = jnp.zeros_like(t)
            scr[4 * p + 0, WINDOW:WINDOW + tm, :] = jnp.where(lo, t, zero).astype(bf16)
            scr[4 * p + 1, WINDOW:WINDOW + tm, :] = jnp.where(lo, zero, tr).astype(bf16)
            scr[4 * p + 2, WINDOW:WINDOW + tm, :] = jnp.where(lo, tr, zero).astype(bf16)
            scr[4 * p + 3, WINDOW:WINDOW + tm, :] = jnp.where(lo, zero, t).astype(bf16)

    lo_q = lax.broadcasted_iota(jnp.int32, (WINDOW, LANES), 1) < HEAD_DIM
    col = lax.broadcasted_iota(jnp.int32, (1, 2 * WINDOW), 1)
    nt_dims = (((1,), (1,)), ((), ()))
    for j in range(nqb):
        r0 = j * WINDOW
        if j == 0:
            pad_mask = jnp.where(first_tile & (col < WINDOW), NEG_INF, 0.0).astype(f32)
        for kh in range(N_KV_HEADS):
            ka = kx_scr[2 * kh + 0, r0:r0 + 2 * WINDOW, :]
            kb = kx_scr[2 * kh + 1, r0:r0 + 2 * WINDOW, :]
            va = vx_scr[2 * kh + 0, r0:r0 + 2 * WINDOW, :]
            vb = vx_scr[2 * kh + 1, r0:r0 + 2 * WINDOW, :]
            for pr in range(2):
                pair = 2 * kh + pr
                qp = proj_ref[r0:r0 + WINDOW, Q0 + pair * LANES: Q0 + (pair + 1) * LANES]
                qp = qp * jnp.asarray(HEAD_DIM ** -0.5, bf16)
                ps = []
                rs = []
                for par, kk in ((0, ka), (1, kb)):
                    h = 2 * pair + par
                    s = lax.dot_general(qp, kk, nt_dims, preferred_element_type=f32)
                    s = s + bias_scr[h]
                    if j == 0:
                        s = s + pad_mask
                    sink = sinks_ref[h]
                    m = jnp.maximum(jnp.max(s, axis=-1, keepdims=True), sink)
                    e = jnp.exp(s - m)
                    den = jnp.sum(e, axis=-1, keepdims=True) + jnp.exp(sink - m)
                    ps.append(e.astype(bf16))
                    rs.append(1.0 / den)
                o = (jnp.dot(ps[0], va, preferred_element_type=f32)
                     + jnp.dot(ps[1], vb, preferred_element_type=f32))
                o = o * jnp.where(lo_q, rs[0], rs[1])
                attn_scr[r0:r0 + WINDOW, pair * LANES:(pair + 1) * LANES] = o

    hist_scr[SUBLANES:SUBLANES + tm, :] = proj_ref[:, XR0:XR0 + RNN_WIDTH].astype(f32)
    neg_lam = -lam_ref[...]
    softplus_neg_lam = jnp.maximum(neg_lam, 0.0) + jnp.log1p(jnp.exp(-jnp.abs(neg_lam)))
    for b in range(RNN_BLOCKS):
        c0 = b * RNN_BLOCK_DIM
        c1 = c0 + RNN_BLOCK_DIM
        xc = convb_ref[:, c0:c1]
        for kk in range(CONV_WIDTH):
            off = SUBLANES - (CONV_WIDTH - 1) + kk
            xc = xc + convw_ref[kk:kk + 1, c0:c1] * hist_scr[off:off + tm, c0:c1]
        ri = jnp.dot(xc.astype(bf16), wri_ref[b], preferred_element_type=f32)
        r = _sigmoid(ri[:, :RNN_BLOCK_DIM] + bri_ref[0:1, c0:c1])
        ig = _sigmoid(ri[:, RNN_BLOCK_DIM:] + bri_ref[1:2, c0:c1])
        log_a = (-LRU_C) * r * softplus_neg_lam[:, c0:c1]
        a = jnp.exp(log_a)
        one_m_a2 = jnp.tanh(-log_a) * (1.0 + a * a)
        a_scr[:, c0:c1] = a
        u_scr[:, c0:c1] = jnp.sqrt(one_m_a2) * (ig * xc)

    row = lax.broadcasted_iota(jnp.int32, (SUBLANES, RNN_WIDTH), 0)

    def scan_group(g, carry):
        r0 = pl.multiple_of(g * SUBLANES, SUBLANES)
        a = a_scr[pl.ds(r0, SUBLANES), :]
        u = u_scr[pl.ds(r0, SUBLANES), :]
        for d in (1, 2, 4):
            keep = row >= d
            a_sh = pltpu.roll(a, d, axis=0)
            u_sh = pltpu.roll(u, d, axis=0)
            u = jnp.where(keep, a * u_sh + u, u)
            a = jnp.where(keep, a * a_sh, a)
        h = u + a * carry
        u_scr[pl.ds(r0, SUBLANES), :] = h
        return h[SUBLANES - 1:SUBLANES, :]

    hcar_scr[...] = lax.fori_loop(0, tm // SUBLANES, scan_group, hcar_scr[...])

    g_attn = proj_ref[:, GA0:GA0 + ATTN_WIDTH].astype(f32)
    ya_in = attn_scr[...] * (g_attn * _sigmoid(g_attn))
    y_attn = jnp.dot(ya_in.astype(bf16), wao_ref[...], preferred_element_type=f32)
    g_rnn = proj_ref[:, GR0:GR0 + RNN_WIDTH].astype(f32)
    yr_in = u_scr[...] * (g_rnn * _sigmoid(g_rnn))
    y_rnn = jnp.dot(yr_in.astype(bf16), wro_ref[...], preferred_element_type=f32)
    mixed = (_sigmoid(proj_ref[:, MA0:MA0 + D_MODEL].astype(f32)) * y_attn
             + _sigmoid(proj_ref[:, MR0:MR0 + D_MODEL].astype(f32)) * y_rnn)
    out = jnp.dot(mixed.astype(bf16), wo_ref[...], preferred_element_type=f32)
    z = alpha * x_ref[...] + out
    mu = jnp.mean(z, axis=-1, keepdims=True)
    zc = z - mu
    var = jnp.mean(zc * zc, axis=-1, keepdims=True)
    out_ref[...] = zc * lax.rsqrt(var + LN_EPS) * lng_ref[...] + lnb_ref[...]


def _const_spec(shape):
    nd = len(shape)
    return pl.BlockSpec(shape, lambda b, s: (0,) * nd, pipeline_mode=pl.Buffered(1))


def _mixer(proj, x2d, bucket, relb, sinks, wri, bri, sp_lam, convw, convb, wao, wro, wo, lng, lnb,
           *, batch, seq, alpha):
    tm = MIX_TM
    nst = seq // tm
    row_map = lambda b, s: (b * nst + s, 0)
    smem = pl.BlockSpec(memory_space=pltpu.SMEM)
    in_specs = [
        smem, smem,
        pl.BlockSpec((tm, IN_WIDTH), row_map),
        pl.BlockSpec((tm, D_MODEL), row_map),
        _const_spec(bucket.shape), _const_spec(wri.shape), _const_spec(bri.shape),
        _const_spec(sp_lam.shape), _const_spec(convw.shape), _const_spec(convb.shape),
        _const_spec(wao.shape), _const_spec(wro.shape), _const_spec(wo.shape),
        _const_spec(lng.shape), _const_spec(lnb.shape),
    ]
    scratch = [
        pltpu.VMEM((N_Q_HEADS, WINDOW, 2 * WINDOW), jnp.float32),
        pltpu.VMEM((8, WINDOW + tm, LANES), jnp.bfloat16),
        pltpu.VMEM((8, WINDOW + tm, LANES), jnp.bfloat16),
        pltpu.VMEM((tm, ATTN_WIDTH), jnp.float32),
        pltpu.VMEM((SUBLANES + tm, RNN_WIDTH), jnp.float32),
        pltpu.VMEM((tm, RNN_WIDTH), jnp.float32),
        pltpu.VMEM((tm, RNN_WIDTH), jnp.float32),
        pltpu.VMEM((1, RNN_WIDTH), jnp.float32),
    ]
    return pl.pallas_call(
        functools.partial(_mixer_kernel, alpha=alpha),
        grid=(batch, nst),
        in_specs=in_specs,
        out_specs=pl.BlockSpec((tm, D_MODEL), row_map),
        out_shape=jax.ShapeDtypeStruct((batch * seq, D_MODEL), jnp.float32),
        scratch_shapes=scratch,
        compiler_params=pltpu.CompilerParams(
            dimension_semantics=("arbitrary", "arbitrary"), vmem_limit_bytes=VMEM_LIMIT),
        name="mixer",
    )(relb, sinks, proj, x2d, bucket, wri, bri, sp_lam, convw, convb, wao, wro, wo, lng, lnb)


def kernel(x, w_in, conv_w, conv_b, w_r, b_r, w_i, b_i, lru_lambda, sinks, w_attn_out, w_rnn_out, w_out,
           ln_g, ln_b, rel_bias):
    batch, seq, d = x.shape
    depth = w_in.shape[0]
    alpha = (2.0 * depth) ** 0.25
    bf16 = jnp.bfloat16
    bucket = jnp.asarray(_t5_bucket_table())
    x2d = x.reshape(batch * seq, d)
    for l in range(depth):
        proj = _in_projection(x2d, w_in[l].astype(bf16))
        wri = jnp.concatenate([w_r[l], w_i[l]], axis=-1).astype(bf16)
        bri = jnp.stack([b_r[l], b_i[l]])
        lam = lru_lambda[l][None, :]
        x2d = _mixer(proj, x2d, bucket, rel_bias, sinks[l], wri, bri, lam, conv_w[l], conv_b[l][None, :],
                     w_attn_out[l].astype(bf16), w_rnn_out[l].astype(bf16), w_out[l].astype(bf16),
                     ln_g[l][None, :], ln_b[l][None, :], batch=batch, seq=seq, alpha=alpha)
    return x2d.reshape(batch, seq, d)
```

```python
import functools
import math

import numpy as np
import jax
import jax.numpy as jnp
from jax import lax
from jax.experimental import pallas as pl
from jax.experimental.pallas import tpu as pltpu

D_MODEL = 2048
N_Q_HEADS = 16
N_KV_HEADS = 4
HEAD_DIM = 64
ATTN_WIDTH = N_Q_HEADS * HEAD_DIM
KV_WIDTH = N_KV_HEADS * HEAD_DIM
WINDOW = 128
N_BUCKETS = 32
MAX_DISTANCE = 128
RNN_WIDTH = D_MODEL
RNN_BLOCKS = 16
RNN_BLOCK_DIM = RNN_WIDTH // RNN_BLOCKS
CONV_WIDTH = 4
LRU_C = 8.0
LN_EPS = 1e-5
NEG_INF = -1e30

Q0 = 0
K0 = Q0 + ATTN_WIDTH
V0 = K0 + KV_WIDTH
GA0 = V0 + KV_WIDTH
XR0 = GA0 + ATTN_WIDTH
GR0 = XR0 + RNN_WIDTH
MA0 = GR0 + RNN_WIDTH
MR0 = MA0 + D_MODEL
IN_WIDTH = MR0 + D_MODEL

LANES = 128
SUBLANES = 8
VMEM_LIMIT = 56 * 1024 * 1024

COL_TILE = 512
PLAIN_TM = 1024
RNN_TM = 1024
MIX_TM = 256

PA_Q0 = 0
PA_K0 = PA_Q0 + ATTN_WIDTH
PA_V0 = PA_K0 + KV_WIDTH
PA_GA0 = PA_V0 + KV_WIDTH
PA_MA0 = PA_GA0 + ATTN_WIDTH
PA_MR0 = PA_MA0 + D_MODEL
PA_WIDTH = PA_MR0 + D_MODEL
N_PLAIN_TILES = PA_WIDTH // COL_TILE
N_PRE_RNN_TILES = XR0 // COL_TILE
N_RNN_COL_TILES = (MA0 - XR0) // COL_TILE


def _t5_bucket_table():
    qi = np.arange(WINDOW)[:, None]
    kj = np.arange(2 * WINDOW)[None, :]
    dist = qi + WINDOW - kj
    max_exact = N_BUCKETS // 2
    n = np.maximum(dist, 0)
    nf = np.maximum(n, max_exact).astype(np.float32)
    large = max_exact + (np.log(nf / np.float32(max_exact)) / np.float32(math.log(MAX_DISTANCE / max_exact))
                         * np.float32(N_BUCKETS - max_exact)).astype(np.int32)
    large = np.minimum(large, N_BUCKETS - 1)
    bucket = np.where(n < max_exact, n, large)
    in_window = (dist >= 0) & (dist < WINDOW)
    return np.where(in_window, bucket, -1).astype(np.int32)


def _sigmoid(x):
    return 0.5 * jnp.tanh(0.5 * x) + 0.5


def _plain_proj_kernel(x_ref, w_ref, pa_ref, xb_ref):
    n = pl.program_id(1)

    @pl.when(n == 0)
    def _():
        xb_ref[...] = x_ref[...].astype(jnp.bfloat16)

    acc = jnp.dot(xb_ref[...], w_ref[...], preferred_element_type=jnp.float32)
    n_q = ATTN_WIDTH // COL_TILE
    n_qkv = PA_GA0 // COL_TILE
    n_ga = PA_MA0 // COL_TILE
    is_lin = n < n_qkv
    is_silu = (n >= n_qkv) & (n < n_ga)
    p_lin = jnp.where(n < n_q, HEAD_DIM ** -0.5, jnp.where(is_lin, 1.0, 0.0)).astype(jnp.float32)
    p_silu = jnp.where(is_silu, 1.0, 0.0).astype(jnp.float32)
    p_sig = jnp.where(is_lin | is_silu, 0.0, 1.0).astype(jnp.float32)
    sg = _sigmoid(acc)
    pa_ref[...] = (acc * (p_lin + p_silu * sg) + p_sig * sg).astype(pa_ref.dtype)


def _plain_projection(x2d, w_bf16):
    m, k = x2d.shape
    tm = PLAIN_TM

    def w_map(i, n):
        return (0, jnp.where(n < N_PRE_RNN_TILES, n, n + N_RNN_COL_TILES))

    return pl.pallas_call(
        _plain_proj_kernel,
        grid=(m // tm, N_PLAIN_TILES),
        in_specs=[pl.BlockSpec((tm, k), lambda i, n: (i, 0)),
                  pl.BlockSpec((k, COL_TILE), w_map)],
        out_specs=[pl.BlockSpec((tm, COL_TILE), lambda i, n: (i, n)),
                   pl.BlockSpec((tm, k), lambda i, n: (i, 0))],
        out_shape=[jax.ShapeDtypeStruct((m, PA_WIDTH), jnp.bfloat16),
                   jax.ShapeDtypeStruct((m, k), jnp.bfloat16)],
        compiler_params=pltpu.CompilerParams(
            dimension_semantics=("arbitrary", "arbitrary"), vmem_limit_bytes=VMEM_LIMIT),
        name="plain_projection",
    )(x2d, w_bf16)


def _linear_scan(a, u, carry):
    tm = a.shape[0]
    groups = tm // SUBLANES
    a3 = a.reshape(groups, SUBLANES, LANES)
    u3 = u.reshape(groups, SUBLANES, LANES)
    row = lax.broadcasted_iota(jnp.int32, (groups, SUBLANES, LANES), 1)
    for d in (1, 2, 4):
        keep = row >= d
        a_sh = pltpu.roll(a3, d, axis=1)
        u_sh = pltpu.roll(u3, d, axis=1)
        u3 = jnp.where(keep, a3 * u_sh + u3, u3)
        a3 = jnp.where(keep, a3 * a_sh, a3)
    hs = []
    for g in range(groups):
        hg = u3[g] + a3[g] * carry
        hs.append(hg)
        carry = hg[SUBLANES - 1:SUBLANES, :]
    return jnp.concatenate(hs, axis=0), carry


def _rnn_proj_kernel(xb_ref, wx_ref, wg_ref, wri_ref, bri_ref, lam_ref, convw_ref, convb_ref,
                     yr_ref, hist_scr, hcar_scr, *, tiles_per_seq):
    tm = RNN_TM
    f32 = jnp.float32
    bf16 = jnp.bfloat16
    first = (pl.program_id(1) % tiles_per_seq) == 0

    hist_scr[0:SUBLANES, :] = jnp.where(first, 0.0, hist_scr[tm:tm + SUBLANES, :])
    carry_all = jnp.where(first, 0.0, hcar_scr[...])

    xb = xb_ref[...]
    hist_scr[SUBLANES:SUBLANES + tm, :] = jnp.dot(xb, wx_ref[...], preferred_element_type=f32)
    g = jnp.dot(xb, wg_ref[...], preferred_element_type=f32)

    neg_lam = -lam_ref[...]
    log_a_scale = (-LRU_C) * (jnp.maximum(neg_lam, 0.0) + jnp.log1p(jnp.exp(-jnp.abs(neg_lam))))
    for b in range(COL_TILE // RNN_BLOCK_DIM):
        c0 = b * RNN_BLOCK_DIM
        c1 = c0 + RNN_BLOCK_DIM
        xc = convb_ref[:, c0:c1]
        for kk in range(CONV_WIDTH):
            off = SUBLANES - (CONV_WIDTH - 1) + kk
            xc = xc + convw_ref[kk:kk + 1, c0:c1] * hist_scr[off:off + tm, c0:c1]
        ri = jnp.dot(xc.astype(bf16), wri_ref[b], preferred_element_type=f32)
        r = _sigmoid(ri[:, :RNN_BLOCK_DIM] + bri_ref[0:1, c0:c1])
        ig = _sigmoid(ri[:, RNN_BLOCK_DIM:] + bri_ref[1:2, c0:c1])
        log_a = r * log_a_scale[:, c0:c1]
        a = jnp.exp(log_a)
        one_m_a2 = jnp.tanh(log_a) * (-1.0 - a * a)
        mult = jnp.where(one_m_a2 > 0.0, one_m_a2 * lax.rsqrt(one_m_a2), 0.0)
        h, carry = _linear_scan(a, mult * (ig * xc), carry_all[:, c0:c1])
        hcar_scr[:, c0:c1] = carry
        gb = g[:, c0:c1]
        yr_ref[:, c0:c1] = (h * (gb * _sigmoid(gb))).astype(yr_ref.dtype)


def _rnn_projection(xb, wx, wg, wri, bri, lam, convw, convb, *, seq):
    m, k = xb.shape
    tm = RNN_TM
    nblk = COL_TILE // RNN_BLOCK_DIM
    col = lambda c, i: (0, c)
    return pl.pallas_call(
        functools.partial(_rnn_proj_kernel, tiles_per_seq=seq // tm),
        grid=(RNN_WIDTH // COL_TILE, m // tm),
        in_specs=[pl.BlockSpec((tm, k), lambda c, i: (i, 0)),
                  pl.BlockSpec((k, COL_TILE), lambda c, i: (0, XR0 // COL_TILE + c)),
                  pl.BlockSpec((k, COL_TILE), lambda c, i: (0, GR0 // COL_TILE + c)),
                  pl.BlockSpec((nblk, RNN_BLOCK_DIM, 2 * RNN_BLOCK_DIM), lambda c, i: (c, 0, 0)),
                  pl.BlockSpec((2, COL_TILE), col),
                  pl.BlockSpec((1, COL_TILE), col),
                  pl.BlockSpec((CONV_WIDTH, COL_TILE), col),
                  pl.BlockSpec((1, COL_TILE), col)],
        out_specs=pl.BlockSpec((tm, COL_TILE), lambda c, i: (i, c)),
        out_shape=jax.ShapeDtypeStruct((m, RNN_WIDTH), jnp.bfloat16),
        scratch_shapes=[pltpu.VMEM((SUBLANES + tm, COL_TILE), jnp.float32),
                        pltpu.VMEM((1, COL_TILE), jnp.float32)],
        compiler_params=pltpu.CompilerParams(
            dimension_semantics=("arbitrary", "arbitrary"), vmem_limit_bytes=VMEM_LIMIT),
        name="rnn_projection",
    )(xb, wx, wg, wri, bri, lam, convw, convb)


def _mixer_kernel(relb_ref, sinks_ref,
                  pa_ref, yr_ref, x_ref, bucket_ref, wao_ref, wro_ref, wo_ref, lng_ref, lnb_ref,
                  out_ref,
                  bias_scr, kx_scr, vx_scr, attn_scr,
                  *, alpha):
    tm = MIX_TM
    nqb = tm // WINDOW
    first_tile = pl.program_id(1) == 0
    f32 = jnp.float32
    bf16 = jnp.bfloat16

    @pl.when((pl.program_id(0) == 0) & first_tile)
    def _():
        bucket = bucket_ref[...]
        for h in range(N_Q_HEADS):
            acc = jnp.full((WINDOW, 2 * WINDOW), NEG_INF, f32)
            for b in range(N_BUCKETS):
                acc = jnp.where(bucket == b, relb_ref[b, h], acc)
            bias_scr[h] = acc

    for scr in (kx_scr, vx_scr):
        prev = scr[:, tm:tm + WINDOW, :]
        scr[:, 0:WINDOW, :] = jnp.where(first_tile, jnp.zeros_like(prev), prev)

    lane = lax.broadcasted_iota(jnp.int32, (tm, LANES), 1)
    lo = lane < HEAD_DIM
    for p in range(N_KV_HEADS // 2):
        for src0, scr in ((PA_K0, kx_scr), (PA_V0, vx_scr)):
            t = pa_ref[:, src0 + p * LANES: src0 + (p + 1) * LANES].astype(f32)
            tr = pltpu.roll(t, HEAD_DIM, axis=1)
            zero = jnp.zeros_like(t)
            scr[4 * p + 0, WINDOW:WINDOW + tm, :] = jnp.where(lo, t, zero).astype(bf16)
            scr[4 * p + 1, WINDOW:WINDOW + tm, :] = jnp.where(lo, zero, tr).astype(bf16)
            scr[4 * p + 2, WINDOW:WINDOW + tm, :] = jnp.where(lo, tr, zero).astype(bf16)
            scr[4 * p + 3, WINDOW:WINDOW + tm, :] = jnp.where(lo, zero, t).astype(bf16)

    lo_q = lax.broadcasted_iota(jnp.int32, (WINDOW, LANES), 1) < HEAD_DIM
    col = lax.broadcasted_iota(jnp.int32, (1, 2 * WINDOW), 1)
    nt_dims = (((1,), (1,)), ((), ()))
    for j in range(nqb):
        r0 = j * WINDOW
        if j == 0:
            pad_mask = jnp.where(first_tile & (col < WINDOW), NEG_INF, 0.0).astype(f32)
        for kh in range(N_KV_HEADS):
            ka = kx_scr[2 * kh + 0, r0:r0 + 2 * WINDOW, :]
            kb = kx_scr[2 * kh + 1, r0:r0 + 2 * WINDOW, :]
            va = vx_scr[2 * kh + 0, r0:r0 + 2 * WINDOW, :]
            vb = vx_scr[2 * kh + 1, r0:r0 + 2 * WINDOW, :]
            for pr in range(2):
                pair = 2 * kh + pr
                qp = pa_ref[r0:r0 + WINDOW, PA_Q0 + pair * LANES: PA_Q0 + (pair + 1) * LANES]
                ps = []
                rs = []
                for par, kk in ((0, ka), (1, kb)):
                    h = 2 * pair + par
                    s = lax.dot_general(qp, kk, nt_dims, preferred_element_type=f32)
                    s = s + bias_scr[h]
                    if j == 0:
                        s = s + pad_mask
                    sink = sinks_ref[h]
                    m = jnp.maximum(jnp.max(s, axis=-1, keepdims=True), sink)
                    e = jnp.exp(s - m)
                    den = jnp.sum(e, axis=-1, keepdims=True) + jnp.exp(sink - m)
                    ps.append(e.astype(bf16))
                    rs.append(1.0 / den)
                o = (jnp.dot(ps[0], va, preferred_element_type=f32)
                     + jnp.dot(ps[1], vb, preferred_element_type=f32))
                o = o * jnp.where(lo_q, rs[0], rs[1])
                ga = pa_ref[r0:r0 + WINDOW, PA_GA0 + pair * LANES: PA_GA0 + (pair + 1) * LANES]
                attn_scr[r0:r0 + WINDOW, pair * LANES:(pair + 1) * LANES] = (o * ga.astype(f32)).astype(bf16)

    y_attn = jnp.dot(attn_scr[...], wao_ref[...], preferred_element_type=f32)
    y_rnn = jnp.dot(yr_ref[...], wro_ref[...], preferred_element_type=f32)
    mixed = (pa_ref[:, PA_MA0:PA_MA0 + D_MODEL].astype(f32) * y_attn
             + pa_ref[:, PA_MR0:PA_MR0 + D_MODEL].astype(f32) * y_rnn)
    out = jnp.dot(mixed.astype(bf16), wo_ref[...], preferred_element_type=f32)
    z = alpha * x_ref[...] + out
    mu = jnp.mean(z, axis=-1, keepdims=True)
    zc = z - mu
    var = jnp.mean(zc * zc, axis=-1, keepdims=True)
    out_ref[...] = zc * lax.rsqrt(var + LN_EPS) * lng_ref[...] + lnb_ref[...]


def _const_spec(shape):
    nd = len(shape)
    return pl.BlockSpec(shape, lambda b, s: (0,) * nd, pipeline_mode=pl.Buffered(1))


def _mixer(pa, yr, x2d, bucket, relb, sinks, wao, wro, wo, lng, lnb, *, batch, seq, alpha):
    tm = MIX_TM
    nst = seq // tm
    row_map = lambda b, s: (b * nst + s, 0)
    smem = pl.BlockSpec(memory_space=pltpu.SMEM)
    in_specs = [
        smem, smem,
        pl.BlockSpec((tm, PA_WIDTH), row_map),
        pl.BlockSpec((tm, RNN_WIDTH), row_map),
        pl.BlockSpec((tm, D_MODEL), row_map),
        _const_spec(bucket.shape), _const_spec(wao.shape), _const_spec(wro.shape), _const_spec(wo.shape),
        _const_spec(lng.shape), _const_spec(lnb.shape),
    ]
    scratch = [
        pltpu.VMEM((N_Q_HEADS, WINDOW, 2 * WINDOW), jnp.float32),
        pltpu.VMEM((8, WINDOW + tm, LANES), jnp.bfloat16),
        pltpu.VMEM((8, WINDOW + tm, LANES), jnp.bfloat16),
        pltpu.VMEM((tm, ATTN_WIDTH), jnp.bfloat16),
    ]
    return pl.pallas_call(
        functools.partial(_mixer_kernel, alpha=alpha),
        grid=(batch, nst),
        in_specs=in_specs,
        out_specs=pl.BlockSpec((tm, D_MODEL), row_map),
        out_shape=jax.ShapeDtypeStruct((batch * seq, D_MODEL), jnp.float32),
        scratch_shapes=scratch,
        compiler_params=pltpu.CompilerParams(
            dimension_semantics=("arbitrary", "arbitrary"), vmem_limit_bytes=VMEM_LIMIT),
        name="mixer",
    )(relb, sinks, pa, yr, x2d, bucket, wao, wro, wo, lng, lnb)


def kernel(x, w_in, conv_w, conv_b, w_r, b_r, w_i, b_i, lru_lambda, sinks, w_attn_out, w_rnn_out, w_out,
           ln_g, ln_b, rel_bias):
    batch, seq, d = x.shape
    depth = w_in.shape[0]
    alpha = (2.0 * depth) ** 0.25
    bf16 = jnp.bfloat16
    bucket = jnp.asarray(_t5_bucket_table())
    x2d = x.reshape(batch * seq, d)
    for l in range(depth):
        w_l = w_in[l].astype(bf16)
        pa, xb = _plain_projection(x2d, w_l)
        wri = jnp.concatenate([w_r[l], w_i[l]], axis=-1).astype(bf16)
        bri = jnp.stack([b_r[l], b_i[l]])
        yr = _rnn_projection(xb, w_l, w_l, wri, bri, lru_lambda[l][None, :],
                             conv_w[l], conv_b[l][None, :], seq=seq)
        x2d = _mixer(pa, yr, x2d, bucket, rel_bias, sinks[l],
                     w_attn_out[l].astype(bf16), w_rnn_out[l].astype(bf16), w_out[l].astype(bf16),
                     ln_g[l][None, :], ln_b[l][None, :], batch=batch, seq=seq, alpha=alpha)
    return x2d.reshape(batch, seq, d)
```

```python
import functools
import math

import numpy as np
import jax
import jax.numpy as jnp
from jax import lax
from jax.experimental import pallas as pl
from jax.experimental.pallas import tpu as pltpu

D_MODEL = 2048
N_Q_HEADS = 16
N_KV_HEADS = 4
HEAD_DIM = 64
ATTN_WIDTH = N_Q_HEADS * HEAD_DIM
KV_WIDTH = N_KV_HEADS * HEAD_DIM
WINDOW = 128
N_BUCKETS = 32
MAX_DISTANCE = 128
RNN_WIDTH = D_MODEL
RNN_BLOCKS = 16
RNN_BLOCK_DIM = RNN_WIDTH // RNN_BLOCKS
CONV_WIDTH = 4
LRU_C = 8.0
LN_EPS = 1e-5
NEG_INF = -1e30

Q0 = 0
K0 = Q0 + ATTN_WIDTH
V0 = K0 + KV_WIDTH
GA0 = V0 + KV_WIDTH
XR0 = GA0 + ATTN_WIDTH
GR0 = XR0 + RNN_WIDTH
MA0 = GR0 + RNN_WIDTH
MR0 = MA0 + D_MODEL
IN_WIDTH = MR0 + D_MODEL

LANES = 128
SUBLANES = 8
VMEM_LIMIT = 56 * 1024 * 1024

COL_TILE = 512
PLAIN_TM = 1024
RNN_TM = 1024
MIX_TM = 256

PA_Q0 = 0
PA_K0 = PA_Q0 + ATTN_WIDTH
PA_V0 = PA_K0 + KV_WIDTH
PA_GA0 = PA_V0 + KV_WIDTH
PA_MA0 = PA_GA0 + ATTN_WIDTH
PA_MR0 = PA_MA0 + D_MODEL
PA_WIDTH = PA_MR0 + D_MODEL
N_PLAIN_TILES = PA_WIDTH // COL_TILE
N_PRE_RNN_TILES = XR0 // COL_TILE
N_RNN_COL_TILES = (MA0 - XR0) // COL_TILE


def _t5_bucket_table():
    qi = np.arange(WINDOW)[:, None]
    kj = np.arange(2 * WINDOW)[None, :]
    dist = qi + WINDOW - kj
    max_exact = N_BUCKETS // 2
    n = np.maximum(dist, 0)
    nf = np.maximum(n, max_exact).astype(np.float32)
    large = max_exact + (np.log(nf / np.float32(max_exact)) / np.float32(math.log(MAX_DISTANCE / max_exact))
                         * np.float32(N_BUCKETS - max_exact)).astype(np.int32)
    large = np.minimum(large, N_BUCKETS - 1)
    bucket = np.where(n < max_exact, n, large)
    in_window = (dist >= 0) & (dist < WINDOW)
    return np.where(in_window, bucket, -1).astype(np.int32)


def _sigmoid(x):
    return 0.5 * jnp.tanh(0.5 * x) + 0.5


def _plain_proj_kernel(x_ref, w_ref, pa_ref):
    n = pl.program_id(1)
    acc = jnp.dot(x_ref[...].astype(jnp.bfloat16), w_ref[...].astype(jnp.bfloat16),
                  preferred_element_type=jnp.float32)
    n_q = ATTN_WIDTH // COL_TILE
    n_qkv = PA_GA0 // COL_TILE
    n_ga = PA_MA0 // COL_TILE
    is_lin = n < n_qkv
    is_silu = (n >= n_qkv) & (n < n_ga)
    p_lin = jnp.where(n < n_q, HEAD_DIM ** -0.5, jnp.where(is_lin, 1.0, 0.0)).astype(jnp.float32)
    p_silu = jnp.where(is_silu, 1.0, 0.0).astype(jnp.float32)
    p_sig = jnp.where(is_lin | is_silu, 0.0, 1.0).astype(jnp.float32)
    sg = _sigmoid(acc)
    pa_ref[...] = (acc * (p_lin + p_silu * sg) + p_sig * sg).astype(pa_ref.dtype)


def _plain_projection(x2d, w_in, layer):
    m, k = x2d.shape
    tm = PLAIN_TM

    def w_map(i, n):
        return (layer, 0, jnp.where(n < N_PRE_RNN_TILES, n, n + N_RNN_COL_TILES))

    return pl.pallas_call(
        _plain_proj_kernel,
        grid=(m // tm, N_PLAIN_TILES),
        in_specs=[pl.BlockSpec((tm, k), lambda i, n: (i, 0)),
                  pl.BlockSpec((None, k, COL_TILE), w_map)],
        out_specs=pl.BlockSpec((tm, COL_TILE), lambda i, n: (i, n)),
        out_shape=jax.ShapeDtypeStruct((m, PA_WIDTH), jnp.bfloat16),
        compiler_params=pltpu.CompilerParams(
            dimension_semantics=("arbitrary", "arbitrary"), vmem_limit_bytes=VMEM_LIMIT),
        name="plain_projection",
    )(x2d, w_in)


def _linear_scan(a, u, carry):
    tm = a.shape[0]
    groups = tm // SUBLANES
    a3 = a.reshape(groups, SUBLANES, LANES)
    u3 = u.reshape(groups, SUBLANES, LANES)
    row = lax.broadcasted_iota(jnp.int32, (groups, SUBLANES, LANES), 1)
    for d in (1, 2, 4):
        keep = row >= d
        a_sh = pltpu.roll(a3, d, axis=1)
        u_sh = pltpu.roll(u3, d, axis=1)
        u3 = jnp.where(keep, a3 * u_sh + u3, u3)
        a3 = jnp.where(keep, a3 * a_sh, a3)
    hs = []
    for g in range(groups):
        hg = u3[g] + a3[g] * carry
        hs.append(hg)
        carry = hg[SUBLANES - 1:SUBLANES, :]
    return jnp.concatenate(hs, axis=0), carry


def _rnn_proj_kernel(x_ref, wx_ref, wg_ref, wri_ref, bri_ref, lam_ref, convw_ref, convb_ref,
                     yr_ref, hist_scr, hcar_scr, *, tiles_per_seq):
    tm = RNN_TM
    f32 = jnp.float32
    bf16 = jnp.bfloat16
    first = (pl.program_id(1) % tiles_per_seq) == 0

    hist_scr[0:SUBLANES, :] = jnp.where(first, 0.0, hist_scr[tm:tm + SUBLANES, :])
    carry_all = jnp.where(first, 0.0, hcar_scr[...])

    xb = x_ref[...].astype(bf16)
    hist_scr[SUBLANES:SUBLANES + tm, :] = jnp.dot(xb, wx_ref[...].astype(bf16), preferred_element_type=f32)
    g = jnp.dot(xb, wg_ref[...].astype(bf16), preferred_element_type=f32)

    neg_lam = -lam_ref[...]
    log_a_scale = (-LRU_C) * (jnp.maximum(neg_lam, 0.0) + jnp.log1p(jnp.exp(-jnp.abs(neg_lam))))
    for b in range(COL_TILE // RNN_BLOCK_DIM):
        c0 = b * RNN_BLOCK_DIM
        c1 = c0 + RNN_BLOCK_DIM
        xc = convb_ref[:, c0:c1]
        for kk in range(CONV_WIDTH):
            off = SUBLANES - (CONV_WIDTH - 1) + kk
            xc = xc + convw_ref[kk:kk + 1, c0:c1] * hist_scr[off:off + tm, c0:c1]
        ri = jnp.dot(xc.astype(bf16), wri_ref[b], preferred_element_type=f32)
        r = _sigmoid(ri[:, :RNN_BLOCK_DIM] + bri_ref[0:1, c0:c1])
        ig = _sigmoid(ri[:, RNN_BLOCK_DIM:] + bri_ref[1:2, c0:c1])
        log_a = r * log_a_scale[:, c0:c1]
        a = jnp.exp(log_a)
        one_m_a2 = jnp.tanh(log_a) * (-1.0 - a * a)
        mult = jnp.where(one_m_a2 > 0.0, one_m_a2 * lax.rsqrt(one_m_a2), 0.0)
        h, carry = _linear_scan(a, mult * (ig * xc), carry_all[:, c0:c1])
        hcar_scr[:, c0:c1] = carry
        gb = g[:, c0:c1]
        yr_ref[:, c0:c1] = (h * (gb * _sigmoid(gb))).astype(yr_ref.dtype)


def _rnn_projection(x2d, w_in, layer, wri, bri, lam, convw, convb, *, seq):
    m, k = x2d.shape
    tm = RNN_TM
    nblk = COL_TILE // RNN_BLOCK_DIM
    col = lambda c, i: (0, c)
    return pl.pallas_call(
        functools.partial(_rnn_proj_kernel, tiles_per_seq=seq // tm),
        grid=(RNN_WIDTH // COL_TILE, m // tm),
        in_specs=[pl.BlockSpec((tm, k), lambda c, i: (i, 0)),
                  pl.BlockSpec((None, k, COL_TILE), lambda c, i: (layer, 0, XR0 // COL_TILE + c)),
                  pl.BlockSpec((None, k, COL_TILE), lambda c, i: (layer, 0, GR0 // COL_TILE + c)),
                  pl.BlockSpec((nblk, RNN_BLOCK_DIM, 2 * RNN_BLOCK_DIM), lambda c, i: (c, 0, 0)),
                  pl.BlockSpec((2, COL_TILE), col),
                  pl.BlockSpec((1, COL_TILE), col),
                  pl.BlockSpec((CONV_WIDTH, COL_TILE), col),
                  pl.BlockSpec((1, COL_TILE), col)],
        out_specs=pl.BlockSpec((tm, COL_TILE), lambda c, i: (i, c)),
        out_shape=jax.ShapeDtypeStruct((m, RNN_WIDTH), jnp.bfloat16),
        scratch_shapes=[pltpu.VMEM((SUBLANES + tm, COL_TILE), jnp.float32),
                        pltpu.VMEM((1, COL_TILE), jnp.float32)],
        compiler_params=pltpu.CompilerParams(
            dimension_semantics=("arbitrary", "arbitrary"), vmem_limit_bytes=VMEM_LIMIT),
        name="rnn_projection",
    )(x2d, w_in, w_in, wri, bri, lam, convw, convb)


def _mixer_kernel(relb_ref, sinks_ref,
                  pa_ref, yr_ref, x_ref, bucket_ref, wao_ref, wro_ref, wo_ref, lng_ref, lnb_ref,
                  out_ref,
                  bias_scr, kx_scr, vx_scr, attn_scr,
                  *, alpha):
    tm = MIX_TM
    nqb = tm // WINDOW
    first_tile = pl.program_id(1) == 0
    f32 = jnp.float32
    bf16 = jnp.bfloat16

    @pl.when((pl.program_id(0) == 0) & first_tile)
    def _():
        bucket = bucket_ref[...]
        for h in range(N_Q_HEADS):
            acc = jnp.full((WINDOW, 2 * WINDOW), NEG_INF, f32)
            for b in range(N_BUCKETS):
                acc = jnp.where(bucket == b, relb_ref[b, h], acc)
            bias_scr[h] = acc

    for scr in (kx_scr, vx_scr):
        prev = scr[:, tm:tm + WINDOW, :]
        scr[:, 0:WINDOW, :] = jnp.where(first_tile, jnp.zeros_like(prev), prev)

    lane = lax.broadcasted_iota(jnp.int32, (tm, LANES), 1)
    lo = lane < HEAD_DIM
    for p in range(N_KV_HEADS // 2):
        for src0, scr in ((PA_K0, kx_scr), (PA_V0, vx_scr)):
            t = pa_ref[:, src0 + p * LANES: src0 + (p + 1) * LANES].astype(f32)
            tr = pltpu.roll(t, HEAD_DIM, axis=1)
            zero = jnp.zeros_like(t)
            scr[4 * p + 0, WINDOW:WINDOW + tm, :] = jnp.where(lo, t, zero).astype(bf16)
            scr[4 * p + 1, WINDOW:WINDOW + tm, :] = jnp.where(lo, zero, tr).astype(bf16)
            scr[4 * p + 2, WINDOW:WINDOW + tm, :] = jnp.where(lo, tr, zero).astype(bf16)
            scr[4 * p + 3, WINDOW:WINDOW + tm, :] = jnp.where(lo, zero, t).astype(bf16)

    lo_q = lax.broadcasted_iota(jnp.int32, (WINDOW, LANES), 1) < HEAD_DIM
    col = lax.broadcasted_iota(jnp.int32, (1, 2 * WINDOW), 1)
    nt_dims = (((1,), (1,)), ((), ()))
    for j in range(nqb):
        r0 = j * WINDOW
        if j == 0:
            pad_mask = jnp.where(first_tile & (col < WINDOW), NEG_INF, 0.0).astype(f32)
        for kh in range(N_KV_HEADS):
            ka = kx_scr[2 * kh + 0, r0:r0 + 2 * WINDOW, :]
            kb = kx_scr[2 * kh + 1, r0:r0 + 2 * WINDOW, :]
            va = vx_scr[2 * kh + 0, r0:r0 + 2 * WINDOW, :]
            vb = vx_scr[2 * kh + 1, r0:r0 + 2 * WINDOW, :]
            for pr in range(2):
                pair = 2 * kh + pr
                qp = pa_ref[r0:r0 + WINDOW, PA_Q0 + pair * LANES: PA_Q0 + (pair + 1) * LANES]
                ps = []
                rs = []
                for par, kk in ((0, ka), (1, kb)):
                    h = 2 * pair + par
                    s = lax.dot_general(qp, kk, nt_dims, preferred_element_type=f32)
                    s = s + bias_scr[h]
                    if j == 0:
                        s = s + pad_mask
                    sink = sinks_ref[h]
                    m = jnp.maximum(jnp.max(s, axis=-1, keepdims=True), sink)
                    e = jnp.exp(s - m)
                    den = jnp.sum(e, axis=-1, keepdims=True) + jnp.exp(sink - m)
                    ps.append(e.astype(bf16))
                    rs.append(1.0 / den)
                o = (jnp.dot(ps[0], va, preferred_element_type=f32)
                     + jnp.dot(ps[1], vb, preferred_element_type=f32))
                o = o * jnp.where(lo_q, rs[0], rs[1])
                ga = pa_ref[r0:r0 + WINDOW, PA_GA0 + pair * LANES: PA_GA0 + (pair + 1) * LANES]
                attn_scr[r0:r0 + WINDOW, pair * LANES:(pair + 1) * LANES] = (o * ga.astype(f32)).astype(bf16)

    y_attn = jnp.dot(attn_scr[...], wao_ref[...], preferred_element_type=f32)
    y_rnn = jnp.dot(yr_ref[...], wro_ref[...], preferred_element_type=f32)
    mixed = (pa_ref[:, PA_MA0:PA_MA0 + D_MODEL].astype(f32) * y_attn
             + pa_ref[:, PA_MR0:PA_MR0 + D_MODEL].astype(f32) * y_rnn)
    out = jnp.dot(mixed.astype(bf16), wo_ref[...], preferred_element_type=f32)
    z = alpha * x_ref[...] + out
    mu = jnp.mean(z, axis=-1, keepdims=True)
    zc = z - mu
    var = jnp.mean(zc * zc, axis=-1, keepdims=True)
    out_ref[...] = zc * lax.rsqrt(var + LN_EPS) * lng_ref[...] + lnb_ref[...]


def _const_spec(shape):
    nd = len(shape)
    return pl.BlockSpec(shape, lambda b, s: (0,) * nd, pipeline_mode=pl.Buffered(1))


def _mixer(pa, yr, x2d, bucket, relb, sinks, wao, wro, wo, lng, lnb, *, batch, seq, alpha):
    tm = MIX_TM
    nst = seq // tm
    row_map = lambda b, s: (b * nst + s, 0)
    smem = pl.BlockSpec(memory_space=pltpu.SMEM)
    in_specs = [
        smem, smem,
        pl.BlockSpec((tm, PA_WIDTH), row_map),
        pl.BlockSpec((tm, RNN_WIDTH), row_map),
        pl.BlockSpec((tm, D_MODEL), row_map),
        _const_spec(bucket.shape), _const_spec(wao.shape), _const_spec(wro.shape), _const_spec(wo.shape),
        _const_spec(lng.shape), _const_spec(lnb.shape),
    ]
    scratch = [
        pltpu.VMEM((N_Q_HEADS, WINDOW, 2 * WINDOW), jnp.float32),
        pltpu.VMEM((8, WINDOW + tm, LANES), jnp.bfloat16),
        pltpu.VMEM((8, WINDOW + tm, LANES), jnp.bfloat16),
        pltpu.VMEM((tm, ATTN_WIDTH), jnp.bfloat16),
    ]
    return pl.pallas_call(
        functools.partial(_mixer_kernel, alpha=alpha),
        grid=(batch, nst),
        in_specs=in_specs,
        out_specs=pl.BlockSpec((tm, D_MODEL), row_map),
        out_shape=jax.ShapeDtypeStruct((batch * seq, D_MODEL), jnp.float32),
        scratch_shapes=scratch,
        compiler_params=pltpu.CompilerParams(
            dimension_semantics=("arbitrary", "arbitrary"), vmem_limit_bytes=VMEM_LIMIT),
        name="mixer",
    )(relb, sinks, pa, yr, x2d, bucket, wao, wro, wo, lng, lnb)


def kernel(x, w_in, conv_w, conv_b, w_r, b_r, w_i, b_i, lru_lambda, sinks, w_attn_out, w_rnn_out, w_out,
           ln_g, ln_b, rel_bias):
    batch, seq, d = x.shape
    depth = w_in.shape[0]
    alpha = (2.0 * depth) ** 0.25
    bf16 = jnp.bfloat16
    bucket = jnp.asarray(_t5_bucket_table())
    x2d = x.reshape(batch * seq, d)
    for l in range(depth):
        pa = _plain_projection(x2d, w_in, l)
        wri = jnp.concatenate([w_r[l], w_i[l]], axis=-1).astype(bf16)
        bri = jnp.stack([b_r[l], b_i[l]])
        yr = _rnn_projection(x2d, w_in, l, wri, bri, lru_lambda[l][None, :],
                             conv_w[l], conv_b[l][None, :], seq=seq)
        x2d = _mixer(pa, yr, x2d, bucket, rel_bias, sinks[l],
                     w_attn_out[l].astype(bf16), w_rnn_out[l].astype(bf16), w_out[l].astype(bf16),
                     ln_g[l][None, :], ln_b[l][None, :], batch=batch, seq=seq, alpha=alpha)
    return x2d.reshape(batch, seq, d)
```

```python
import functools
import math

import numpy as np
import jax
import jax.numpy as jnp
from jax import lax
from jax.experimental import pallas as pl
from jax.experimental.pallas import tpu as pltpu

D_MODEL = 2048
N_Q_HEADS = 16
N_KV_HEADS = 4
HEAD_DIM = 64
ATTN_WIDTH = N_Q_HEADS * HEAD_DIM
KV_WIDTH = N_KV_HEADS * HEAD_DIM
WINDOW = 128
N_BUCKETS = 32
MAX_DISTANCE = 128
RNN_WIDTH = D_MODEL
RNN_BLOCKS = 16
RNN_BLOCK_DIM = RNN_WIDTH // RNN_BLOCKS
CONV_WIDTH = 4
LRU_C = 8.0
LN_EPS = 1e-5
NEG_INF = -1e30

Q0 = 0
K0 = Q0 + ATTN_WIDTH
V0 = K0 + KV_WIDTH
GA0 = V0 + KV_WIDTH
XR0 = GA0 + ATTN_WIDTH
GR0 = XR0 + RNN_WIDTH
MA0 = GR0 + RNN_WIDTH
MR0 = MA0 + D_MODEL
IN_WIDTH = MR0 + D_MODEL

LANES = 128
SUBLANES = 8
VMEM_LIMIT = 56 * 1024 * 1024

COL_TILE = 512
PLAIN_TM = 1024
RNN_TM = 1024
MIX_TM = 256

PA_Q0 = 0
PA_K0 = PA_Q0 + ATTN_WIDTH
PA_V0 = PA_K0 + KV_WIDTH
PA_GA0 = PA_V0 + KV_WIDTH
PA_MA0 = PA_GA0 + ATTN_WIDTH
PA_MR0 = PA_MA0 + D_MODEL
PA_WIDTH = PA_MR0 + D_MODEL
N_PLAIN_TILES = PA_WIDTH // COL_TILE
N_PRE_RNN_TILES = XR0 // COL_TILE
N_RNN_COL_TILES = (MA0 - XR0) // COL_TILE


def _t5_bucket_table():
    qi = np.arange(WINDOW)[:, None]
    kj = np.arange(2 * WINDOW)[None, :]
    dist = qi + WINDOW - kj
    max_exact = N_BUCKETS // 2
    n = np.maximum(dist, 0)
    nf = np.maximum(n, max_exact).astype(np.float32)
    large = max_exact + (np.log(nf / np.float32(max_exact)) / np.float32(math.log(MAX_DISTANCE / max_exact))
                         * np.float32(N_BUCKETS - max_exact)).astype(np.int32)
    large = np.minimum(large, N_BUCKETS - 1)
    bucket = np.where(n < max_exact, n, large)
    in_window = (dist >= 0) & (dist < WINDOW)
    return np.where(in_window, bucket, -1).astype(np.int32)


def _sigmoid(x):
    return 0.5 * jnp.tanh(0.5 * x) + 0.5


def _plain_proj_kernel(x_ref, w_ref, pa_ref):
    n = pl.program_id(1)
    acc = jnp.dot(x_ref[...].astype(jnp.bfloat16), w_ref[...].astype(jnp.bfloat16),
                  preferred_element_type=jnp.float32)
    n_q = ATTN_WIDTH // COL_TILE
    n_qkv = PA_GA0 // COL_TILE
    n_ga = PA_MA0 // COL_TILE
    is_lin = n < n_qkv
    is_silu = (n >= n_qkv) & (n < n_ga)
    p_lin = jnp.where(n < n_q, HEAD_DIM ** -0.5, jnp.where(is_lin, 1.0, 0.0)).astype(jnp.float32)
    p_silu = jnp.where(is_silu, 1.0, 0.0).astype(jnp.float32)
    p_sig = jnp.where(is_lin | is_silu, 0.0, 1.0).astype(jnp.float32)
    sg = _sigmoid(acc)
    pa_ref[...] = (acc * (p_lin + p_silu * sg) + p_sig * sg).astype(pa_ref.dtype)


def _plain_projection(x2d, w_in, layer):
    m, k = x2d.shape
    tm = PLAIN_TM

    def w_map(i, n):
        return (layer, 0, jnp.where(n < N_PRE_RNN_TILES, n, n + N_RNN_COL_TILES))

    return pl.pallas_call(
        _plain_proj_kernel,
        grid=(m // tm, N_PLAIN_TILES),
        in_specs=[pl.BlockSpec((tm, k), lambda i, n: (i, 0)),
                  pl.BlockSpec((None, k, COL_TILE), w_map)],
        out_specs=pl.BlockSpec((tm, COL_TILE), lambda i, n: (i, n)),
        out_shape=jax.ShapeDtypeStruct((m, PA_WIDTH), jnp.bfloat16),
        compiler_params=pltpu.CompilerParams(
            dimension_semantics=("arbitrary", "arbitrary"), vmem_limit_bytes=VMEM_LIMIT),
        name="plain_projection",
    )(x2d, w_in)


def _linear_scan(a, u, carry):
    tm = a.shape[0]
    groups = tm // SUBLANES
    a3 = a.reshape(groups, SUBLANES, LANES)
    u3 = u.reshape(groups, SUBLANES, LANES)
    row = lax.broadcasted_iota(jnp.int32, (groups, SUBLANES, LANES), 1)
    for d in (1, 2, 4):
        keep = row >= d
        a_sh = pltpu.roll(a3, d, axis=1)
        u_sh = pltpu.roll(u3, d, axis=1)
        u3 = jnp.where(keep, a3 * u_sh + u3, u3)
        a3 = jnp.where(keep, a3 * a_sh, a3)
    hs = []
    for g in range(groups):
        hg = u3[g] + a3[g] * carry
        hs.append(hg)
        carry = hg[SUBLANES - 1:SUBLANES, :]
    return jnp.concatenate(hs, axis=0), carry


def _rnn_proj_kernel(x_ref, wx_ref, wg_ref, wri_ref, bri_ref, lam_ref, convw_ref, convb_ref,
                     yr_ref, hist_scr, hcar_scr, *, tiles_per_seq):
    tm = RNN_TM
    f32 = jnp.float32
    bf16 = jnp.bfloat16
    first = (pl.program_id(1) % tiles_per_seq) == 0

    hist_scr[0:SUBLANES, :] = jnp.where(first, 0.0, hist_scr[tm:tm + SUBLANES, :])
    carry_all = jnp.where(first, 0.0, hcar_scr[...])

    xb = x_ref[...].astype(bf16)
    hist_scr[SUBLANES:SUBLANES + tm, :] = jnp.dot(xb, wx_ref[...].astype(bf16), preferred_element_type=f32)
    g = jnp.dot(xb, wg_ref[...].astype(bf16), preferred_element_type=f32)

    neg_lam = -lam_ref[...]
    log_a_scale = (-LRU_C) * (jnp.maximum(neg_lam, 0.0) + jnp.log1p(jnp.exp(-jnp.abs(neg_lam))))
    for b in range(COL_TILE // RNN_BLOCK_DIM):
        c0 = b * RNN_BLOCK_DIM
        c1 = c0 + RNN_BLOCK_DIM
        xc = convb_ref[:, c0:c1]
        for kk in range(CONV_WIDTH):
            off = SUBLANES - (CONV_WIDTH - 1) + kk
            xc = xc + convw_ref[kk:kk + 1, c0:c1] * hist_scr[off:off + tm, c0:c1]
        ri = jnp.dot(xc.astype(bf16), wri_ref[b], preferred_element_type=f32)
        r = _sigmoid(ri[:, :RNN_BLOCK_DIM] + bri_ref[0:1, c0:c1])
        ig = _sigmoid(ri[:, RNN_BLOCK_DIM:] + bri_ref[1:2, c0:c1])
        log_a = r * log_a_scale[:, c0:c1]
        a = jnp.exp(log_a)
        one_m_a2 = jnp.tanh(log_a) * (-1.0 - a * a)
        mult = jnp.where(one_m_a2 > 0.0, one_m_a2 * lax.rsqrt(one_m_a2), 0.0)
        h, carry = _linear_scan(a, mult * (ig * xc), carry_all[:, c0:c1])
        hcar_scr[:, c0:c1] = carry
        gb = g[:, c0:c1]
        yr_ref[:, c0:c1] = (h * (gb * _sigmoid(gb))).astype(yr_ref.dtype)


def _rnn_projection(x2d, w_in, layer, wri, bri, lam, convw, convb, *, seq):
    m, k = x2d.shape
    tm = RNN_TM
    nblk = COL_TILE // RNN_BLOCK_DIM
    col = lambda c, i: (0, c)
    return pl.pallas_call(
        functools.partial(_rnn_proj_kernel, tiles_per_seq=seq // tm),
        grid=(RNN_WIDTH // COL_TILE, m // tm),
        in_specs=[pl.BlockSpec((tm, k), lambda c, i: (i, 0)),
                  pl.BlockSpec((None, k, COL_TILE), lambda c, i: (layer, 0, XR0 // COL_TILE + c)),
                  pl.BlockSpec((None, k, COL_TILE), lambda c, i: (layer, 0, GR0 // COL_TILE + c)),
                  pl.BlockSpec((nblk, RNN_BLOCK_DIM, 2 * RNN_BLOCK_DIM), lambda c, i: (c, 0, 0)),
                  pl.BlockSpec((2, COL_TILE), col),
                  pl.BlockSpec((1, COL_TILE), col),
                  pl.BlockSpec((CONV_WIDTH, COL_TILE), col),
                  pl.BlockSpec((1, COL_TILE), col)],
        out_specs=pl.BlockSpec((tm, COL_TILE), lambda c, i: (i, c)),
        out_shape=jax.ShapeDtypeStruct((m, RNN_WIDTH), jnp.bfloat16),
        scratch_shapes=[pltpu.VMEM((SUBLANES + tm, COL_TILE), jnp.float32),
                        pltpu.VMEM((1, COL_TILE), jnp.float32)],
        compiler_params=pltpu.CompilerParams(
            dimension_semantics=("arbitrary", "arbitrary"), vmem_limit_bytes=VMEM_LIMIT),
        name="rnn_projection",
    )(x2d, w_in, w_in, wri, bri, lam, convw, convb)


def _mixer_kernel(relb_ref, sinks_ref,
                  pa_ref, yr_ref, x_ref, bucket_ref, wao_ref, wro_ref, wo_ref, lng_ref, lnb_ref,
                  out_ref,
                  bias_scr, kx_scr, vx_scr, attn_scr,
                  *, alpha):
    tm = MIX_TM
    nqb = tm // WINDOW
    first_tile = pl.program_id(1) == 0
    f32 = jnp.float32
    bf16 = jnp.bfloat16

    @pl.when((pl.program_id(0) == 0) & first_tile)
    def _():
        bucket = bucket_ref[...]
        for t in range(N_Q_HEADS):
            h = 4 * (t // 4) + 2 * (t % 2) + (t // 2) % 2
            acc = jnp.full((WINDOW, 2 * WINDOW), NEG_INF, f32)
            for b in range(N_BUCKETS):
                acc = jnp.where(bucket == b, relb_ref[b, h], acc)
            bias_scr[t * WINDOW:(t + 1) * WINDOW, :] = acc

    for scr in (kx_scr, vx_scr):
        prev = scr[:, tm:tm + WINDOW, :]
        scr[:, 0:WINDOW, :] = jnp.where(first_tile, jnp.zeros_like(prev), prev)

    u32 = jnp.uint32
    lo_u = lax.broadcasted_iota(jnp.int32, (tm // 2, LANES), 1) < HEAD_DIM
    zero_u = jnp.zeros((tm // 2, LANES), u32)
    for p in range(N_KV_HEADS // 2):
        for src0, scr in ((PA_K0, kx_scr), (PA_V0, vx_scr)):
            t = pltpu.bitcast(pa_ref[:, src0 + p * LANES: src0 + (p + 1) * LANES], u32)
            tr = pltpu.roll(t, HEAD_DIM, axis=1)
            variants = (jnp.where(lo_u, t, zero_u),
                        jnp.where(lo_u, zero_u, tr),
                        jnp.where(lo_u, tr, zero_u),
                        jnp.where(lo_u, zero_u, t))
            for v, val in enumerate(variants):
                scr[4 * p + v, WINDOW:WINDOW + tm, :] = pltpu.bitcast(val, bf16)

    lo_q = lax.broadcasted_iota(jnp.int32, (WINDOW, LANES), 1) < HEAD_DIM
    col = lax.broadcasted_iota(jnp.int32, (1, 2 * WINDOW), 1)
    ones_blk = jnp.ones((2 * WINDOW, LANES), bf16)
    nt_dims = (((1,), (1,)), ((), ()))
    for j in range(nqb):
        r0 = j * WINDOW
        s_parts = []
        for kh in range(N_KV_HEADS):
            qq = jnp.concatenate(
                [pa_ref[r0:r0 + WINDOW, PA_Q0 + (2 * kh + pr) * LANES: PA_Q0 + (2 * kh + pr + 1) * LANES]
                 for pr in range(2)], axis=0)
            for par in range(2):
                kk = kx_scr[2 * kh + par, r0:r0 + 2 * WINDOW, :]
                s_parts.append(lax.dot_general(qq, kk, nt_dims, preferred_element_type=f32))
        s = jnp.concatenate(s_parts, axis=0) + bias_scr[...]
        if j == 0:
            s = s + jnp.where(first_tile & (col < WINDOW), NEG_INF, 0.0).astype(f32)
        m = jnp.max(s, axis=-1, keepdims=True)
        p = jnp.exp(s - m).astype(bf16)
        for kh in range(N_KV_HEADS):
            o_ext = []
            for par in range(2):
                t0 = (2 * kh + par) * 2 * WINDOW
                vv = jnp.concatenate([vx_scr[2 * kh + par, r0:r0 + 2 * WINDOW, :], ones_blk], axis=1)
                o_ext.append(jnp.dot(p[t0:t0 + 2 * WINDOW, :], vv, preferred_element_type=f32))
            for pr in range(2):
                pair = 2 * kh + pr
                rows = slice(pr * WINDOW, (pr + 1) * WINDOW)
                rden = []
                for par in range(2):
                    t0 = (2 * kh + par) * 2 * WINDOW + pr * WINDOW
                    sink_term = jnp.exp(sinks_ref[2 * pair + par] - m[t0:t0 + WINDOW, :])
                    rden.append(1.0 / (o_ext[par][rows, LANES:] + sink_term))
                o = (o_ext[0][rows, :LANES] + o_ext[1][rows, :LANES]) * jnp.where(lo_q, rden[0], rden[1])
                ga = pa_ref[r0:r0 + WINDOW, PA_GA0 + pair * LANES: PA_GA0 + (pair + 1) * LANES]
                attn_scr[r0:r0 + WINDOW, pair * LANES:(pair + 1) * LANES] = (o * ga.astype(f32)).astype(bf16)

    y_attn = jnp.dot(attn_scr[...], wao_ref[...], preferred_element_type=f32)
    y_rnn = jnp.dot(yr_ref[...], wro_ref[...], preferred_element_type=f32)
    mixed = (pa_ref[:, PA_MA0:PA_MA0 + D_MODEL].astype(f32) * y_attn
             + pa_ref[:, PA_MR0:PA_MR0 + D_MODEL].astype(f32) * y_rnn)
    out = jnp.dot(mixed.astype(bf16), wo_ref[...], preferred_element_type=f32)
    z = alpha * x_ref[...] + out
    mu = jnp.mean(z, axis=-1, keepdims=True)
    zc = z - mu
    var = jnp.mean(zc * zc, axis=-1, keepdims=True)
    out_ref[...] = zc * lax.rsqrt(var + LN_EPS) * lng_ref[...] + lnb_ref[...]


def _const_spec(shape):
    nd = len(shape)
    return pl.BlockSpec(shape, lambda b, s: (0,) * nd, pipeline_mode=pl.Buffered(1))


def _mixer(pa, yr, x2d, bucket, relb, sinks, wao, wro, wo, lng, lnb, *, batch, seq, alpha):
    tm = MIX_TM
    nst = seq // tm
    row_map = lambda b, s: (b * nst + s, 0)
    smem = pl.BlockSpec(memory_space=pltpu.SMEM)
    in_specs = [
        smem, smem,
        pl.BlockSpec((tm, PA_WIDTH), row_map),
        pl.BlockSpec((tm, RNN_WIDTH), row_map),
        pl.BlockSpec((tm, D_MODEL), row_map),
        _const_spec(bucket.shape), _const_spec(wao.shape), _const_spec(wro.shape), _const_spec(wo.shape),
        _const_spec(lng.shape), _const_spec(lnb.shape),
    ]
    scratch = [
        pltpu.VMEM((N_Q_HEADS * WINDOW, 2 * WINDOW), jnp.float32),
        pltpu.VMEM((8, WINDOW + tm, LANES), jnp.bfloat16),
        pltpu.VMEM((8, WINDOW + tm, LANES), jnp.bfloat16),
        pltpu.VMEM((tm, ATTN_WIDTH), jnp.bfloat16),
    ]
    return pl.pallas_call(
        functools.partial(_mixer_kernel, alpha=alpha),
        grid=(batch, nst),
        in_specs=in_specs,
        out_specs=pl.BlockSpec((tm, D_MODEL), row_map),
        out_shape=jax.ShapeDtypeStruct((batch * seq, D_MODEL), jnp.float32),
        scratch_shapes=scratch,
        compiler_params=pltpu.CompilerParams(
            dimension_semantics=("arbitrary", "arbitrary"), vmem_limit_bytes=VMEM_LIMIT),
        name="mixer",
    )(relb, sinks, pa, yr, x2d, bucket, wao, wro, wo, lng, lnb)


def kernel(x, w_in, conv_w, conv_b, w_r, b_r, w_i, b_i, lru_lambda, sinks, w_attn_out, w_rnn_out, w_out,
           ln_g, ln_b, rel_bias):
    batch, seq, d = x.shape
    depth = w_in.shape[0]
    alpha = (2.0 * depth) ** 0.25
    bf16 = jnp.bfloat16
    bucket = jnp.asarray(_t5_bucket_table())
    x2d = x.reshape(batch * seq, d)
    for l in range(depth):
        pa = _plain_projection(x2d, w_in, l)
        wri = jnp.concatenate([w_r[l], w_i[l]], axis=-1).astype(bf16)
        bri = jnp.stack([b_r[l], b_i[l]])
        yr = _rnn_projection(x2d, w_in, l, wri, bri, lru_lambda[l][None, :],
                             conv_w[l], conv_b[l][None, :], seq=seq)
        x2d = _mixer(pa, yr, x2d, bucket, rel_bias, sinks[l],
                     w_attn_out[l].astype(bf16), w_rnn_out[l].astype(bf16), w_out[l].astype(bf16),
                     ln_g[l][None, :], ln_b[l][None, :], batch=batch, seq=seq, alpha=alpha)
    return x2d.reshape(batch, seq, d)
```

```python
import functools
import math

import numpy as np
import jax
import jax.numpy as jnp
from jax import lax
from jax.experimental import pallas as pl
from jax.experimental.pallas import tpu as pltpu

D_MODEL = 2048
N_Q_HEADS = 16
N_KV_HEADS = 4
HEAD_DIM = 64
ATTN_WIDTH = N_Q_HEADS * HEAD_DIM
KV_WIDTH = N_KV_HEADS * HEAD_DIM
WINDOW = 128
N_BUCKETS = 32
MAX_DISTANCE = 128
RNN_WIDTH = D_MODEL
RNN_BLOCKS = 16
RNN_BLOCK_DIM = RNN_WIDTH // RNN_BLOCKS
CONV_WIDTH = 4
LRU_C = 8.0
LN_EPS = 1e-5
NEG_INF = -1e30

Q0 = 0
K0 = Q0 + ATTN_WIDTH
V0 = K0 + KV_WIDTH
GA0 = V0 + KV_WIDTH
XR0 = GA0 + ATTN_WIDTH
GR0 = XR0 + RNN_WIDTH
MA0 = GR0 + RNN_WIDTH
MR0 = MA0 + D_MODEL
IN_WIDTH = MR0 + D_MODEL

LANES = 128
SUBLANES = 8
VMEM_LIMIT = 56 * 1024 * 1024

COL_TILE = 512
PLAIN_TM = 2048
RNN_TM = 1024
MIX_TM = 256

PA_Q0 = 0
PA_K0 = PA_Q0 + ATTN_WIDTH
PA_V0 = PA_K0 + KV_WIDTH
PA_GA0 = PA_V0 + KV_WIDTH
PA_MA0 = PA_GA0 + ATTN_WIDTH
PA_MR0 = PA_MA0 + D_MODEL
PA_WIDTH = PA_MR0 + D_MODEL
N_PLAIN_TILES = PA_WIDTH // COL_TILE
N_PRE_RNN_TILES = XR0 // COL_TILE
N_RNN_COL_TILES = (MA0 - XR0) // COL_TILE


def _t5_bucket_table():
    qi = np.arange(WINDOW)[:, None]
    kj = np.arange(2 * WINDOW)[None, :]
    dist = qi + WINDOW - kj
    max_exact = N_BUCKETS // 2
    n = np.maximum(dist, 0)
    nf = np.maximum(n, max_exact).astype(np.float32)
    large = max_exact + (np.log(nf / np.float32(max_exact)) / np.float32(math.log(MAX_DISTANCE / max_exact))
                         * np.float32(N_BUCKETS - max_exact)).astype(np.int32)
    large = np.minimum(large, N_BUCKETS - 1)
    bucket = np.where(n < max_exact, n, large)
    in_window = (dist >= 0) & (dist < WINDOW)
    return np.where(in_window, bucket, -1).astype(np.int32)


def _sigmoid(x):
    return 0.5 * jnp.tanh(0.5 * x) + 0.5


def _plain_proj_kernel(x_ref, w_ref, pa_ref):
    pa_ref[...] = jnp.dot(x_ref[...].astype(jnp.bfloat16), w_ref[...].astype(jnp.bfloat16),
                          preferred_element_type=jnp.float32).astype(pa_ref.dtype)


def _plain_projection(x2d, w_in, layer):
    m, k = x2d.shape
    tm = PLAIN_TM

    def w_map(i, n):
        return (layer, 0, jnp.where(n < N_PRE_RNN_TILES, n, n + N_RNN_COL_TILES))

    return pl.pallas_call(
        _plain_proj_kernel,
        grid=(m // tm, N_PLAIN_TILES),
        in_specs=[pl.BlockSpec((tm, k), lambda i, n: (i, 0)),
                  pl.BlockSpec((None, k, COL_TILE), w_map)],
        out_specs=pl.BlockSpec((tm, COL_TILE), lambda i, n: (i, n)),
        out_shape=jax.ShapeDtypeStruct((m, PA_WIDTH), jnp.bfloat16),
        compiler_params=pltpu.CompilerParams(
            dimension_semantics=("arbitrary", "arbitrary"), vmem_limit_bytes=VMEM_LIMIT),
        name="plain_projection",
    )(x2d, w_in)


def _linear_scan(a, u, carry):
    tm = a.shape[0]
    groups = tm // SUBLANES
    a3 = a.reshape(groups, SUBLANES, LANES)
    u3 = u.reshape(groups, SUBLANES, LANES)
    row = lax.broadcasted_iota(jnp.int32, (groups, SUBLANES, LANES), 1)
    for d in (1, 2, 4):
        keep = row >= d
        a_sh = pltpu.roll(a3, d, axis=1)
        u_sh = pltpu.roll(u3, d, axis=1)
        u3 = jnp.where(keep, a3 * u_sh + u3, u3)
        a3 = jnp.where(keep, a3 * a_sh, a3)
    hs = []
    for g in range(groups):
        hg = u3[g] + a3[g] * carry
        hs.append(hg)
        carry = hg[SUBLANES - 1:SUBLANES, :]
    return jnp.concatenate(hs, axis=0), carry


def _rnn_proj_kernel(x_ref, wx_ref, wg_ref, wri_ref, bri_ref, lam_ref, convw_ref, convb_ref,
                     yr_ref, hist_scr, hcar_scr, *, tiles_per_seq):
    tm = RNN_TM
    f32 = jnp.float32
    bf16 = jnp.bfloat16
    first = (pl.program_id(1) % tiles_per_seq) == 0

    hist_scr[0:SUBLANES, :] = jnp.where(first, 0.0, hist_scr[tm:tm + SUBLANES, :])
    carry_all = jnp.where(first, 0.0, hcar_scr[...])

    xb = x_ref[...].astype(bf16)
    hist_scr[SUBLANES:SUBLANES + tm, :] = jnp.dot(xb, wx_ref[...].astype(bf16), preferred_element_type=f32)
    g = jnp.dot(xb, wg_ref[...].astype(bf16), preferred_element_type=f32)

    neg_lam = -lam_ref[...]
    log_a_scale = (-LRU_C) * (jnp.maximum(neg_lam, 0.0) + jnp.log1p(jnp.exp(-jnp.abs(neg_lam))))
    for b in range(COL_TILE // RNN_BLOCK_DIM):
        c0 = b * RNN_BLOCK_DIM
        c1 = c0 + RNN_BLOCK_DIM
        xc = convb_ref[:, c0:c1]
        for kk in range(CONV_WIDTH):
            off = SUBLANES - (CONV_WIDTH - 1) + kk
            xc = xc + convw_ref[kk:kk + 1, c0:c1] * hist_scr[off:off + tm, c0:c1]
        ri = jnp.dot(xc.astype(bf16), wri_ref[b], preferred_element_type=f32)
        r = _sigmoid(ri[:, :RNN_BLOCK_DIM] + bri_ref[0:1, c0:c1])
        ig = _sigmoid(ri[:, RNN_BLOCK_DIM:] + bri_ref[1:2, c0:c1])
        log_a = r * log_a_scale[:, c0:c1]
        a = jnp.exp(log_a)
        one_m_a2 = jnp.tanh(log_a) * (-1.0 - a * a)
        mult = jnp.where(one_m_a2 > 0.0, one_m_a2 * lax.rsqrt(one_m_a2), 0.0)
        h, carry = _linear_scan(a, mult * (ig * xc), carry_all[:, c0:c1])
        hcar_scr[:, c0:c1] = carry
        gb = g[:, c0:c1]
        yr_ref[:, c0:c1] = (h * (gb * _sigmoid(gb))).astype(yr_ref.dtype)


def _rnn_projection(x2d, w_in, layer, wri, bri, lam, convw, convb, *, seq):
    m, k = x2d.shape
    tm = RNN_TM
    nblk = COL_TILE // RNN_BLOCK_DIM
    col = lambda c, i: (0, c)
    return pl.pallas_call(
        functools.partial(_rnn_proj_kernel, tiles_per_seq=seq // tm),
        grid=(RNN_WIDTH // COL_TILE, m // tm),
        in_specs=[pl.BlockSpec((tm, k), lambda c, i: (i, 0)),
                  pl.BlockSpec((None, k, COL_TILE), lambda c, i: (layer, 0, XR0 // COL_TILE + c)),
                  pl.BlockSpec((None, k, COL_TILE), lambda c, i: (layer, 0, GR0 // COL_TILE + c)),
                  pl.BlockSpec((nblk, RNN_BLOCK_DIM, 2 * RNN_BLOCK_DIM), lambda c, i: (c, 0, 0)),
                  pl.BlockSpec((2, COL_TILE), col),
                  pl.BlockSpec((1, COL_TILE), col),
                  pl.BlockSpec((CONV_WIDTH, COL_TILE), col),
                  pl.BlockSpec((1, COL_TILE), col)],
        out_specs=pl.BlockSpec((tm, COL_TILE), lambda c, i: (i, c)),
        out_shape=jax.ShapeDtypeStruct((m, RNN_WIDTH), jnp.bfloat16),
        scratch_shapes=[pltpu.VMEM((SUBLANES + tm, COL_TILE), jnp.float32),
                        pltpu.VMEM((1, COL_TILE), jnp.float32)],
        compiler_params=pltpu.CompilerParams(
            dimension_semantics=("arbitrary", "arbitrary"), vmem_limit_bytes=VMEM_LIMIT),
        name="rnn_projection",
    )(x2d, w_in, w_in, wri, bri, lam, convw, convb)


def _mixer_kernel(relb_ref, sinks_ref,
                  pa_ref, yr_ref, x_ref, bucket_ref, wao_ref, wro_ref, wo_ref, lng_ref, lnb_ref,
                  out_ref,
                  bias_scr, kx_scr, vx_scr, attn_scr,
                  *, alpha):
    tm = MIX_TM
    nqb = tm // WINDOW
    first_tile = pl.program_id(1) == 0
    f32 = jnp.float32
    bf16 = jnp.bfloat16

    @pl.when((pl.program_id(0) == 0) & first_tile)
    def _():
        bucket = bucket_ref[...]
        for t in range(N_Q_HEADS):
            h = 4 * (t // 4) + 2 * (t % 2) + (t // 2) % 2
            acc = jnp.full((WINDOW, 2 * WINDOW), NEG_INF, f32)
            for b in range(N_BUCKETS):
                acc = jnp.where(bucket == b, relb_ref[b, h], acc)
            bias_scr[t * WINDOW:(t + 1) * WINDOW, :] = acc

    for scr in (kx_scr, vx_scr):
        prev = scr[:, tm:tm + WINDOW, :]
        scr[:, 0:WINDOW, :] = jnp.where(first_tile, jnp.zeros_like(prev), prev)

    u32 = jnp.uint32
    lo_u = lax.broadcasted_iota(jnp.int32, (tm // 2, LANES), 1) < HEAD_DIM
    zero_u = jnp.zeros((tm // 2, LANES), u32)
    for p in range(N_KV_HEADS // 2):
        for src0, scr in ((PA_K0, kx_scr), (PA_V0, vx_scr)):
            t = pltpu.bitcast(pa_ref[:, src0 + p * LANES: src0 + (p + 1) * LANES], u32)
            tr = pltpu.roll(t, HEAD_DIM, axis=1)
            variants = (jnp.where(lo_u, t, zero_u),
                        jnp.where(lo_u, zero_u, tr),
                        jnp.where(lo_u, tr, zero_u),
                        jnp.where(lo_u, zero_u, t))
            for v, val in enumerate(variants):
                scr[4 * p + v, WINDOW:WINDOW + tm, :] = pltpu.bitcast(val, bf16)

    lo_q = lax.broadcasted_iota(jnp.int32, (WINDOW, LANES), 1) < HEAD_DIM
    col = lax.broadcasted_iota(jnp.int32, (1, 2 * WINDOW), 1)
    ones_blk = jnp.ones((2 * WINDOW, LANES), bf16)
    nt_dims = (((1,), (1,)), ((), ()))
    for j in range(nqb):
        r0 = j * WINDOW
        s_parts = []
        for kh in range(N_KV_HEADS):
            qq = jnp.concatenate(
                [pa_ref[r0:r0 + WINDOW, PA_Q0 + (2 * kh + pr) * LANES: PA_Q0 + (2 * kh + pr + 1) * LANES]
                 for pr in range(2)], axis=0) * jnp.asarray(HEAD_DIM ** -0.5, bf16)
            for par in range(2):
                kk = kx_scr[2 * kh + par, r0:r0 + 2 * WINDOW, :]
                s_parts.append(lax.dot_general(qq, kk, nt_dims, preferred_element_type=f32))
        s = jnp.concatenate(s_parts, axis=0) + bias_scr[...]
        if j == 0:
            s = s + jnp.where(first_tile & (col < WINDOW), NEG_INF, 0.0).astype(f32)
        m = jnp.max(s, axis=-1, keepdims=True)
        p = jnp.exp(s - m).astype(bf16)
        for kh in range(N_KV_HEADS):
            o_ext = []
            for par in range(2):
                t0 = (2 * kh + par) * 2 * WINDOW
                vv = jnp.concatenate([vx_scr[2 * kh + par, r0:r0 + 2 * WINDOW, :], ones_blk], axis=1)
                o_ext.append(jnp.dot(p[t0:t0 + 2 * WINDOW, :], vv, preferred_element_type=f32))
            for pr in range(2):
                pair = 2 * kh + pr
                rows = slice(pr * WINDOW, (pr + 1) * WINDOW)
                rden = []
                for par in range(2):
                    t0 = (2 * kh + par) * 2 * WINDOW + pr * WINDOW
                    sink_term = jnp.exp(sinks_ref[2 * pair + par] - m[t0:t0 + WINDOW, :])
                    rden.append(1.0 / (o_ext[par][rows, LANES:] + sink_term))
                o = (o_ext[0][rows, :LANES] + o_ext[1][rows, :LANES]) * jnp.where(lo_q, rden[0], rden[1])
                ga = pa_ref[r0:r0 + WINDOW, PA_GA0 + pair * LANES: PA_GA0 + (pair + 1) * LANES].astype(f32)
                attn_scr[r0:r0 + WINDOW, pair * LANES:(pair + 1) * LANES] = (
                    o * (ga * _sigmoid(ga))).astype(bf16)

    y_attn = jnp.dot(attn_scr[...], wao_ref[...], preferred_element_type=f32)
    y_rnn = jnp.dot(yr_ref[...], wro_ref[...], preferred_element_type=f32)
    mixed = (_sigmoid(pa_ref[:, PA_MA0:PA_MA0 + D_MODEL].astype(f32)) * y_attn
             + _sigmoid(pa_ref[:, PA_MR0:PA_MR0 + D_MODEL].astype(f32)) * y_rnn)
    out = jnp.dot(mixed.astype(bf16), wo_ref[...], preferred_element_type=f32)
    z = alpha * x_ref[...] + out
    mu = jnp.mean(z, axis=-1, keepdims=True)
    zc = z - mu
    var = jnp.mean(zc * zc, axis=-1, keepdims=True)
    out_ref[...] = zc * lax.rsqrt(var + LN_EPS) * lng_ref[...] + lnb_ref[...]


def _const_spec(shape):
    nd = len(shape)
    return pl.BlockSpec(shape, lambda b, s: (0,) * nd, pipeline_mode=pl.Buffered(1))


def _mixer(pa, yr, x2d, bucket, relb, sinks, wao, wro, wo, lng, lnb, *, batch, seq, alpha):
    tm = MIX_TM
    nst = seq // tm
    row_map = lambda b, s: (b * nst + s, 0)
    smem = pl.BlockSpec(memory_space=pltpu.SMEM)
    in_specs = [
        smem, smem,
        pl.BlockSpec((tm, PA_WIDTH), row_map),
        pl.BlockSpec((tm, RNN_WIDTH), row_map),
        pl.BlockSpec((tm, D_MODEL), row_map),
        _const_spec(bucket.shape), _const_spec(wao.shape), _const_spec(wro.shape), _const_spec(wo.shape),
        _const_spec(lng.shape), _const_spec(lnb.shape),
    ]
    scratch = [
        pltpu.VMEM((N_Q_HEADS * WINDOW, 2 * WINDOW), jnp.float32),
        pltpu.VMEM((8, WINDOW + tm, LANES), jnp.bfloat16),
        pltpu.VMEM((8, WINDOW + tm, LANES), jnp.bfloat16),
        pltpu.VMEM((tm, ATTN_WIDTH), jnp.bfloat16),
    ]
    return pl.pallas_call(
        functools.partial(_mixer_kernel, alpha=alpha),
        grid=(batch, nst),
        in_specs=in_specs,
        out_specs=pl.BlockSpec((tm, D_MODEL), row_map),
        out_shape=jax.ShapeDtypeStruct((batch * seq, D_MODEL), jnp.float32),
        scratch_shapes=scratch,
        compiler_params=pltpu.CompilerParams(
            dimension_semantics=("arbitrary", "arbitrary"), vmem_limit_bytes=VMEM_LIMIT),
        name="mixer",
    )(relb, sinks, pa, yr, x2d, bucket, wao, wro, wo, lng, lnb)


def kernel(x, w_in, conv_w, conv_b, w_r, b_r, w_i, b_i, lru_lambda, sinks, w_attn_out, w_rnn_out, w_out,
           ln_g, ln_b, rel_bias):
    batch, seq, d = x.shape
    depth = w_in.shape[0]
    alpha = (2.0 * depth) ** 0.25
    bf16 = jnp.bfloat16
    bucket = jnp.asarray(_t5_bucket_table())
    x2d = x.reshape(batch * seq, d)
    for l in range(depth):
        pa = _plain_projection(x2d, w_in, l)
        wri = jnp.concatenate([w_r[l], w_i[l]], axis=-1).astype(bf16)
        bri = jnp.stack([b_r[l], b_i[l]])
        yr = _rnn_projection(x2d, w_in, l, wri, bri, lru_lambda[l][None, :],
                             conv_w[l], conv_b[l][None, :], seq=seq)
        x2d = _mixer(pa, yr, x2d, bucket, rel_bias, sinks[l],
                     w_attn_out[l].astype(bf16), w_rnn_out[l].astype(bf16), w_out[l].astype(bf16),
                     ln_g[l][None, :], ln_b[l][None, :], batch=batch, seq=seq, alpha=alpha)
    return x2d.reshape(batch, seq, d)
```

```python
import functools
import math

import numpy as np
import jax
import jax.numpy as jnp
from jax import lax
from jax.experimental import pallas as pl
from jax.experimental.pallas import tpu as pltpu

D_MODEL = 2048
N_Q_HEADS = 16
N_KV_HEADS = 4
HEAD_DIM = 64
ATTN_WIDTH = N_Q_HEADS * HEAD_DIM
KV_WIDTH = N_KV_HEADS * HEAD_DIM
WINDOW = 128
N_BUCKETS = 32
MAX_DISTANCE = 128
RNN_WIDTH = D_MODEL
RNN_BLOCKS = 16
RNN_BLOCK_DIM = RNN_WIDTH // RNN_BLOCKS
CONV_WIDTH = 4
LRU_C = 8.0
LN_EPS = 1e-5
NEG_INF = -1e30

Q0 = 0
K0 = Q0 + ATTN_WIDTH
V0 = K0 + KV_WIDTH
GA0 = V0 + KV_WIDTH
XR0 = GA0 + ATTN_WIDTH
GR0 = XR0 + RNN_WIDTH
MA0 = GR0 + RNN_WIDTH
MR0 = MA0 + D_MODEL
IN_WIDTH = MR0 + D_MODEL

LANES = 128
SUBLANES = 8
VMEM_LIMIT = 60 * 1024 * 1024

COL_TILE = 512
PLAIN_TM = 2048
RNN_TM = 1024
MIX_TM = 256
CAST_ROWS = 128

PA_Q0 = 0
PA_K0 = PA_Q0 + ATTN_WIDTH
PA_V0 = PA_K0 + KV_WIDTH
PA_GA0 = PA_V0 + KV_WIDTH
PA_MA0 = PA_GA0 + ATTN_WIDTH
PA_MR0 = PA_MA0 + D_MODEL
PA_WIDTH = PA_MR0 + D_MODEL
N_PLAIN_TILES = PA_WIDTH // COL_TILE
N_PRE_RNN_TILES = XR0 // COL_TILE
N_RNN_COL_TILES = (MA0 - XR0) // COL_TILE


def _t5_bucket_table():
    qi = np.arange(WINDOW)[:, None]
    kj = np.arange(2 * WINDOW)[None, :]
    dist = qi + WINDOW - kj
    max_exact = N_BUCKETS // 2
    n = np.maximum(dist, 0)
    nf = np.maximum(n, max_exact).astype(np.float32)
    large = max_exact + (np.log(nf / np.float32(max_exact)) / np.float32(math.log(MAX_DISTANCE / max_exact))
                         * np.float32(N_BUCKETS - max_exact)).astype(np.int32)
    large = np.minimum(large, N_BUCKETS - 1)
    bucket = np.where(n < max_exact, n, large)
    in_window = (dist >= 0) & (dist < WINDOW)
    return np.where(in_window, bucket, -1).astype(np.int32)


def _sigmoid(x):
    return 0.5 * jnp.tanh(0.5 * x) + 0.5


def _plain_proj_kernel(x_ref, w_ref, wao_ref, wro_ref, wo_ref, pa_ref, wao_bf_ref, wro_bf_ref, wo_bf_ref):
    pa_ref[...] = jnp.dot(x_ref[...].astype(jnp.bfloat16), w_ref[...].astype(jnp.bfloat16),
                          preferred_element_type=jnp.float32).astype(pa_ref.dtype)
    wao_bf_ref[...] = wao_ref[...].astype(wao_bf_ref.dtype)
    wro_bf_ref[...] = wro_ref[...].astype(wro_bf_ref.dtype)
    wo_bf_ref[...] = wo_ref[...].astype(wo_bf_ref.dtype)


def _plain_projection(x2d, w_in, wao, wro, wo, layer):
    m, k = x2d.shape
    tm = PLAIN_TM
    grid = (m // tm, N_PLAIN_TILES)

    def w_map(i, n):
        return (layer, 0, jnp.where(n < N_PRE_RNN_TILES, n, n + N_RNN_COL_TILES))

    weights = (wao, wro, wo)
    n_blocks = [w.shape[1] // CAST_ROWS for w in weights]
    assert sum(n_blocks) <= grid[0] * grid[1]
    cast_in, cast_out = [], []
    start = 0
    for w, nb in zip(weights, n_blocks):
        def blk(i, n, start=start, nb=nb):
            return jnp.clip(i * N_PLAIN_TILES + n - start, 0, nb - 1)
        cast_in.append(pl.BlockSpec((None, CAST_ROWS, w.shape[2]), lambda i, n, blk=blk: (layer, blk(i, n), 0)))
        cast_out.append(pl.BlockSpec((CAST_ROWS, w.shape[2]), lambda i, n, blk=blk: (blk(i, n), 0)))
        start += nb

    return pl.pallas_call(
        _plain_proj_kernel,
        grid=grid,
        in_specs=[pl.BlockSpec((tm, k), lambda i, n: (i, 0)),
                  pl.BlockSpec((None, k, COL_TILE), w_map)] + cast_in,
        out_specs=[pl.BlockSpec((tm, COL_TILE), lambda i, n: (i, n))] + cast_out,
        out_shape=[jax.ShapeDtypeStruct((m, PA_WIDTH), jnp.bfloat16)]
                  + [jax.ShapeDtypeStruct(w.shape[1:], jnp.bfloat16) for w in weights],
        compiler_params=pltpu.CompilerParams(
            dimension_semantics=("arbitrary", "arbitrary"), vmem_limit_bytes=VMEM_LIMIT),
        name="plain_projection",
    )(x2d, w_in, wao, wro, wo)


def _linear_scan(a, u, carry):
    tm = a.shape[0]
    groups = tm // SUBLANES
    a3 = a.reshape(groups, SUBLANES, LANES)
    u3 = u.reshape(groups, SUBLANES, LANES)
    row = lax.broadcasted_iota(jnp.int32, (groups, SUBLANES, LANES), 1)
    for d in (1, 2, 4):
        keep = row >= d
        a_sh = pltpu.roll(a3, d, axis=1)
        u_sh = pltpu.roll(u3, d, axis=1)
        u3 = jnp.where(keep, a3 * u_sh + u3, u3)
        a3 = jnp.where(keep, a3 * a_sh, a3)
    hs = []
    for g in range(groups):
        hg = u3[g] + a3[g] * carry
        hs.append(hg)
        carry = hg[SUBLANES - 1:SUBLANES, :]
    return jnp.concatenate(hs, axis=0), carry


def _rnn_proj_kernel(x_ref, wx_ref, wg_ref, wr_ref, wi_ref, br_ref, bi_ref, lam_ref, convw_ref, convb_ref,
                     yr_ref, hist_scr, hcar_scr, *, tiles_per_seq):
    tm = RNN_TM
    f32 = jnp.float32
    bf16 = jnp.bfloat16
    first = (pl.program_id(1) % tiles_per_seq) == 0

    hist_scr[0:SUBLANES, :] = jnp.where(first, 0.0, hist_scr[tm:tm + SUBLANES, :])
    carry_all = jnp.where(first, 0.0, hcar_scr[...])

    xb = x_ref[...].astype(bf16)
    hist_scr[SUBLANES:SUBLANES + tm, :] = jnp.dot(xb, wx_ref[...].astype(bf16), preferred_element_type=f32)
    g = jnp.dot(xb, wg_ref[...].astype(bf16), preferred_element_type=f32)

    neg_lam = -lam_ref[...]
    half_scale = (-0.5 * LRU_C) * (jnp.maximum(neg_lam, 0.0) + jnp.log1p(jnp.exp(-jnp.abs(neg_lam))))
    half_br = 0.5 * br_ref[...]
    half_bi = 0.5 * bi_ref[...]
    for b in range(COL_TILE // RNN_BLOCK_DIM):
        c0 = b * RNN_BLOCK_DIM
        c1 = c0 + RNN_BLOCK_DIM
        xc = convb_ref[:, c0:c1]
        for kk in range(CONV_WIDTH):
            off = SUBLANES - (CONV_WIDTH - 1) + kk
            xc = xc + convw_ref[kk:kk + 1, c0:c1] * hist_scr[off:off + tm, c0:c1]
        w_gates = (0.5 * jnp.concatenate([wr_ref[b], wi_ref[b]], axis=1)).astype(bf16)
        ri = jnp.dot(xc.astype(bf16), w_gates, preferred_element_type=f32)
        tanh_r = jnp.tanh(ri[:, :RNN_BLOCK_DIM] + half_br[:, c0:c1])
        tanh_i = jnp.tanh(ri[:, RNN_BLOCK_DIM:] + half_bi[:, c0:c1])
        log_a = tanh_r * half_scale[:, c0:c1] + half_scale[:, c0:c1]
        a = jnp.exp(log_a)
        one_m_a2 = jnp.tanh(log_a) * (-1.0 - a * a)
        mult = jnp.where(one_m_a2 > 0.0, one_m_a2 * lax.rsqrt(one_m_a2), 0.0)
        u = (mult * xc) * (0.5 * tanh_i + 0.5)
        h, carry = _linear_scan(a, u, carry_all[:, c0:c1])
        hcar_scr[:, c0:c1] = carry
        gb = g[:, c0:c1]
        yr_ref[:, c0:c1] = (h * (gb * _sigmoid(gb))).astype(yr_ref.dtype)


def _rnn_projection(x2d, w_in, layer, w_r, w_i, b_r, b_i, lam, convw, convb, *, seq):
    m, k = x2d.shape
    tm = RNN_TM
    nblk = COL_TILE // RNN_BLOCK_DIM
    gate_w = pl.BlockSpec((None, nblk, RNN_BLOCK_DIM, RNN_BLOCK_DIM), lambda c, i: (layer, c, 0, 0))
    row = pl.BlockSpec((None, 1, COL_TILE), lambda c, i: (layer, 0, c))
    return pl.pallas_call(
        functools.partial(_rnn_proj_kernel, tiles_per_seq=seq // tm),
        grid=(RNN_WIDTH // COL_TILE, m // tm),
        in_specs=[pl.BlockSpec((tm, k), lambda c, i: (i, 0)),
                  pl.BlockSpec((None, k, COL_TILE), lambda c, i: (layer, 0, XR0 // COL_TILE + c)),
                  pl.BlockSpec((None, k, COL_TILE), lambda c, i: (layer, 0, GR0 // COL_TILE + c)),
                  gate_w, gate_w, row, row, row,
                  pl.BlockSpec((None, CONV_WIDTH, COL_TILE), lambda c, i: (layer, 0, c)),
                  row],
        out_specs=pl.BlockSpec((tm, COL_TILE), lambda c, i: (i, c)),
        out_shape=jax.ShapeDtypeStruct((m, RNN_WIDTH), jnp.bfloat16),
        scratch_shapes=[pltpu.VMEM((SUBLANES + tm, COL_TILE), jnp.float32),
                        pltpu.VMEM((1, COL_TILE), jnp.float32)],
        compiler_params=pltpu.CompilerParams(
            dimension_semantics=("arbitrary", "arbitrary"), vmem_limit_bytes=VMEM_LIMIT),
        name="rnn_projection",
    )(x2d, w_in, w_in, w_r, w_i, b_r, b_i, lam, convw, convb)


def _mixer_kernel(relb_ref, sinks_ref,
                  pa_ref, yr_ref, x_ref, bucket_ref, wao_ref, wro_ref, wo_ref, lng_ref, lnb_ref,
                  out_ref,
                  bias_scr, kx_scr, vx_scr, attn_scr,
                  *, alpha):
    tm = MIX_TM
    nqb = tm // WINDOW
    first_tile = pl.program_id(1) == 0
    f32 = jnp.float32
    bf16 = jnp.bfloat16

    @pl.when((pl.program_id(0) == 0) & first_tile)
    def _():
        bucket = bucket_ref[...]
        for t in range(N_Q_HEADS):
            h = 4 * (t // 4) + 2 * (t % 2) + (t // 2) % 2
            acc = jnp.full((WINDOW, 2 * WINDOW), NEG_INF, f32)
            for b in range(N_BUCKETS):
                acc = jnp.where(bucket == b, relb_ref[b, h], acc)
            bias_scr[t * WINDOW:(t + 1) * WINDOW, :] = acc

    for scr in (kx_scr, vx_scr):
        prev = scr[:, tm:tm + WINDOW, :]
        scr[:, 0:WINDOW, :] = jnp.where(first_tile, jnp.zeros_like(prev), prev)

    u32 = jnp.uint32
    lo_u = lax.broadcasted_iota(jnp.int32, (tm // 2, LANES), 1) < HEAD_DIM
    zero_u = jnp.zeros((tm // 2, LANES), u32)
    for p in range(N_KV_HEADS // 2):
        for src0, scr in ((PA_K0, kx_scr), (PA_V0, vx_scr)):
            t = pltpu.bitcast(pa_ref[:, src0 + p * LANES: src0 + (p + 1) * LANES], u32)
            tr = pltpu.roll(t, HEAD_DIM, axis=1)
            variants = (jnp.where(lo_u, t, zero_u),
                        jnp.where(lo_u, zero_u, tr),
                        jnp.where(lo_u, tr, zero_u),
                        jnp.where(lo_u, zero_u, t))
            for v, val in enumerate(variants):
                scr[4 * p + v, WINDOW:WINDOW + tm, :] = pltpu.bitcast(val, bf16)

    lo_q = lax.broadcasted_iota(jnp.int32, (WINDOW, LANES), 1) < HEAD_DIM
    col = lax.broadcasted_iota(jnp.int32, (1, 2 * WINDOW), 1)
    ones_blk = jnp.ones((2 * WINDOW, LANES), bf16)
    nt_dims = (((1,), (1,)), ((), ()))
    for j in range(nqb):
        r0 = j * WINDOW
        s_parts = []
        for kh in range(N_KV_HEADS):
            qq = jnp.concatenate(
                [pa_ref[r0:r0 + WINDOW, PA_Q0 + (2 * kh + pr) * LANES: PA_Q0 + (2 * kh + pr + 1) * LANES]
                 for pr in range(2)], axis=0) * jnp.asarray(HEAD_DIM ** -0.5, bf16)
            for par in range(2):
                kk = kx_scr[2 * kh + par, r0:r0 + 2 * WINDOW, :]
                s_parts.append(lax.dot_general(qq, kk, nt_dims, preferred_element_type=f32))
        s = jnp.concatenate(s_parts, axis=0) + bias_scr[...]
        if j == 0:
            s = s + jnp.where(first_tile & (col < WINDOW), NEG_INF, 0.0).astype(f32)
        m = jnp.max(s, axis=-1, keepdims=True)
        p = jnp.exp(s - m).astype(bf16)
        for kh in range(N_KV_HEADS):
            o_ext = []
            for par in range(2):
                t0 = (2 * kh + par) * 2 * WINDOW
                vv = jnp.concatenate([vx_scr[2 * kh + par, r0:r0 + 2 * WINDOW, :], ones_blk], axis=1)
                o_ext.append(jnp.dot(p[t0:t0 + 2 * WINDOW, :], vv, preferred_element_type=f32))
            for pr in range(2):
                pair = 2 * kh + pr
                rows = slice(pr * WINDOW, (pr + 1) * WINDOW)
                rden = []
                for par in range(2):
                    t0 = (2 * kh + par) * 2 * WINDOW + pr * WINDOW
                    sink_term = jnp.exp(sinks_ref[2 * pair + par] - m[t0:t0 + WINDOW, :])
                    rden.append(1.0 / (o_ext[par][rows, LANES:] + sink_term))
                o = (o_ext[0][rows, :LANES] + o_ext[1][rows, :LANES]) * jnp.where(lo_q, rden[0], rden[1])
                ga = pa_ref[r0:r0 + WINDOW, PA_GA0 + pair * LANES: PA_GA0 + (pair + 1) * LANES].astype(f32)
                attn_scr[r0:r0 + WINDOW, pair * LANES:(pair + 1) * LANES] = (
                    o * (ga * _sigmoid(ga))).astype(bf16)

    y_attn = jnp.dot(attn_scr[...], wao_ref[...], preferred_element_type=f32)
    y_rnn = jnp.dot(yr_ref[...], wro_ref[...], preferred_element_type=f32)
    mixed = (_sigmoid(pa_ref[:, PA_MA0:PA_MA0 + D_MODEL].astype(f32)) * y_attn
             + _sigmoid(pa_ref[:, PA_MR0:PA_MR0 + D_MODEL].astype(f32)) * y_rnn)
    out = jnp.dot(mixed.astype(bf16), wo_ref[...], preferred_element_type=f32)
    z = alpha * x_ref[...] + out
    mu = jnp.mean(z, axis=-1, keepdims=True)
    zc = z - mu
    var = jnp.mean(zc * zc, axis=-1, keepdims=True)
    out_ref[...] = zc * lax.rsqrt(var + LN_EPS) * lng_ref[...] + lnb_ref[...]


def _const_spec(shape):
    nd = len(shape)
    return pl.BlockSpec(shape, lambda b, s: (0,) * nd, pipeline_mode=pl.Buffered(1))


def _mixer(pa, yr, x2d, bucket, relb, sinks, wao, wro, wo, lng, lnb, *, batch, seq, alpha):
    tm = MIX_TM
    nst = seq // tm
    row_map = lambda b, s: (b * nst + s, 0)
    smem = pl.BlockSpec(memory_space=pltpu.SMEM)
    in_specs = [
        smem, smem,
        pl.BlockSpec((tm, PA_WIDTH), row_map),
        pl.BlockSpec((tm, RNN_WIDTH), row_map),
        pl.BlockSpec((tm, D_MODEL), row_map),
        _const_spec(bucket.shape), _const_spec(wao.shape), _const_spec(wro.shape), _const_spec(wo.shape),
        _const_spec(lng.shape), _const_spec(lnb.shape),
    ]
    scratch = [
        pltpu.VMEM((N_Q_HEADS * WINDOW, 2 * WINDOW), jnp.float32),
        pltpu.VMEM((8, WINDOW + tm, LANES), jnp.bfloat16),
        pltpu.VMEM((8, WINDOW + tm, LANES), jnp.bfloat16),
        pltpu.VMEM((tm, ATTN_WIDTH), jnp.bfloat16),
    ]
    return pl.pallas_call(
        functools.partial(_mixer_kernel, alpha=alpha),
        grid=(batch, nst),
        in_specs=in_specs,
        out_specs=pl.BlockSpec((tm, D_MODEL), row_map),
        out_shape=jax.ShapeDtypeStruct((batch * seq, D_MODEL), jnp.float32),
        scratch_shapes=scratch,
        compiler_params=pltpu.CompilerParams(
            dimension_semantics=("arbitrary", "arbitrary"), vmem_limit_bytes=VMEM_LIMIT),
        name="mixer",
    )(relb, sinks, pa, yr, x2d, bucket, wao, wro, wo, lng, lnb)


def kernel(x, w_in, conv_w, conv_b, w_r, b_r, w_i, b_i, lru_lambda, sinks, w_attn_out, w_rnn_out, w_out,
           ln_g, ln_b, rel_bias):
    batch, seq, d = x.shape
    depth = w_in.shape[0]
    alpha = (2.0 * depth) ** 0.25
    bucket = jnp.asarray(_t5_bucket_table())
    x2d = x.reshape(batch * seq, d)
    as_rows = lambda v: v.reshape(depth, 1, v.shape[-1])
    b_r, b_i, lam, conv_b = as_rows(b_r), as_rows(b_i), as_rows(lru_lambda), as_rows(conv_b)
    for l in range(depth):
        pa, wao, wro, wo = _plain_projection(x2d, w_in, w_attn_out, w_rnn_out, w_out, l)
        yr = _rnn_projection(x2d, w_in, l, w_r, w_i, b_r, b_i, lam, conv_w, conv_b, seq=seq)
        x2d = _mixer(pa, yr, x2d, bucket, rel_bias, sinks[l], wao, wro, wo,
                     ln_g[l][None, :], ln_b[l][None, :], batch=batch, seq=seq, alpha=alpha)
    return x2d.reshape(batch, seq, d)
```

```python
import functools
import math

import numpy as np
import jax
import jax.numpy as jnp
from jax import lax
from jax.experimental import pallas as pl
from jax.experimental.pallas import tpu as pltpu

D_MODEL = 2048
N_Q_HEADS = 16
N_KV_HEADS = 4
HEAD_DIM = 64
ATTN_WIDTH = N_Q_HEADS * HEAD_DIM
KV_WIDTH = N_KV_HEADS * HEAD_DIM
WINDOW = 128
N_BUCKETS = 32
MAX_DISTANCE = 128
RNN_WIDTH = D_MODEL
RNN_BLOCKS = 16
RNN_BLOCK_DIM = RNN_WIDTH // RNN_BLOCKS
CONV_WIDTH = 4
LRU_C = 8.0
LN_EPS = 1e-5
NEG_INF = -1e30

Q0 = 0
K0 = Q0 + ATTN_WIDTH
V0 = K0 + KV_WIDTH
GA0 = V0 + KV_WIDTH
XR0 = GA0 + ATTN_WIDTH
GR0 = XR0 + RNN_WIDTH
MA0 = GR0 + RNN_WIDTH
MR0 = MA0 + D_MODEL
IN_WIDTH = MR0 + D_MODEL

LANES = 128
SUBLANES = 8
VMEM_LIMIT = 60 * 1024 * 1024

COL_TILE = 512
PLAIN_TM = 2048
RNN_TM = 1024
MIX_TM = 256
CAST_ROWS = 128

PA_Q0 = 0
PA_K0 = PA_Q0 + ATTN_WIDTH
PA_V0 = PA_K0 + KV_WIDTH
PA_GA0 = PA_V0 + KV_WIDTH
PA_MA0 = PA_GA0 + ATTN_WIDTH
PA_MR0 = PA_MA0 + D_MODEL
PA_WIDTH = PA_MR0 + D_MODEL
N_PLAIN_TILES = PA_WIDTH // COL_TILE
N_PRE_RNN_TILES = XR0 // COL_TILE
N_RNN_COL_TILES = (MA0 - XR0) // COL_TILE


def _t5_bucket_table():
    qi = np.arange(WINDOW)[:, None]
    kj = np.arange(2 * WINDOW)[None, :]
    dist = qi + WINDOW - kj
    max_exact = N_BUCKETS // 2
    n = np.maximum(dist, 0)
    nf = np.maximum(n, max_exact).astype(np.float32)
    large = max_exact + (np.log(nf / np.float32(max_exact)) / np.float32(math.log(MAX_DISTANCE / max_exact))
                         * np.float32(N_BUCKETS - max_exact)).astype(np.int32)
    large = np.minimum(large, N_BUCKETS - 1)
    bucket = np.where(n < max_exact, n, large)
    in_window = (dist >= 0) & (dist < WINDOW)
    return np.where(in_window, bucket, -1).astype(np.int32)


def _sigmoid(x):
    return 0.5 * jnp.tanh(0.5 * x) + 0.5


def _plain_proj_kernel(x_ref, w_ref, wao_ref, wro_ref, wo_ref, pa_ref, wao_bf_ref, wro_bf_ref, wo_bf_ref):
    pa_ref[...] = jnp.dot(x_ref[...].astype(jnp.bfloat16), w_ref[...].astype(jnp.bfloat16),
                          preferred_element_type=jnp.float32).astype(pa_ref.dtype)
    wao_bf_ref[...] = wao_ref[...].astype(wao_bf_ref.dtype)
    wro_bf_ref[...] = wro_ref[...].astype(wro_bf_ref.dtype)
    wo_bf_ref[...] = wo_ref[...].astype(wo_bf_ref.dtype)


def _plain_projection(x2d, w_in, wao, wro, wo, layer):
    m, k = x2d.shape
    tm = PLAIN_TM
    grid = (m // tm, N_PLAIN_TILES)

    def w_map(i, n):
        return (layer, 0, jnp.where(n < N_PRE_RNN_TILES, n, n + N_RNN_COL_TILES))

    weights = (wao, wro, wo)
    n_blocks = [w.shape[1] // CAST_ROWS for w in weights]
    assert sum(n_blocks) <= grid[0] * grid[1]
    cast_in, cast_out = [], []
    start = 0
    for w, nb in zip(weights, n_blocks):
        def blk(i, n, start=start, nb=nb):
            return jnp.clip(i * N_PLAIN_TILES + n - start, 0, nb - 1)
        cast_in.append(pl.BlockSpec((None, CAST_ROWS, w.shape[2]), lambda i, n, blk=blk: (layer, blk(i, n), 0)))
        cast_out.append(pl.BlockSpec((CAST_ROWS, w.shape[2]), lambda i, n, blk=blk: (blk(i, n), 0)))
        start += nb

    return pl.pallas_call(
        _plain_proj_kernel,
        grid=grid,
        in_specs=[pl.BlockSpec((tm, k), lambda i, n: (i, 0)),
                  pl.BlockSpec((None, k, COL_TILE), w_map)] + cast_in,
        out_specs=[pl.BlockSpec((tm, COL_TILE), lambda i, n: (i, n))] + cast_out,
        out_shape=[jax.ShapeDtypeStruct((m, PA_WIDTH), jnp.bfloat16)]
                  + [jax.ShapeDtypeStruct(w.shape[1:], jnp.bfloat16) for w in weights],
        compiler_params=pltpu.CompilerParams(
            dimension_semantics=("arbitrary", "arbitrary"), vmem_limit_bytes=VMEM_LIMIT),
        name="plain_projection",
    )(x2d, w_in, wao, wro, wo)


def _linear_scan(a, u, carry):
    tm = a.shape[0]
    groups = tm // SUBLANES
    a3 = a.reshape(groups, SUBLANES, LANES)
    u3 = u.reshape(groups, SUBLANES, LANES)
    row = lax.broadcasted_iota(jnp.int32, (groups, SUBLANES, LANES), 1)
    for d in (1, 2, 4):
        keep = row >= d
        a_sh = pltpu.roll(a3, d, axis=1)
        u_sh = pltpu.roll(u3, d, axis=1)
        u3 = jnp.where(keep, a3 * u_sh + u3, u3)
        a3 = jnp.where(keep, a3 * a_sh, a3)
    hs = []
    for g in range(groups):
        hg = u3[g] + a3[g] * carry
        hs.append(hg)
        carry = hg[SUBLANES - 1:SUBLANES, :]
    return jnp.concatenate(hs, axis=0), carry


def _rnn_proj_kernel(x_ref, wx_ref, wg_ref, wr_ref, wi_ref, br_ref, bi_ref, lam_ref, convw_ref, convb_ref,
                     yr_ref, hist_scr, hcar_scr, *, tiles_per_seq):
    tm = RNN_TM
    f32 = jnp.float32
    bf16 = jnp.bfloat16
    first = (pl.program_id(1) % tiles_per_seq) == 0

    hist_scr[0:SUBLANES, :] = jnp.where(first, 0.0, hist_scr[tm:tm + SUBLANES, :])
    carry_all = jnp.where(first, 0.0, hcar_scr[...])

    xb = x_ref[...].astype(bf16)
    hist_scr[SUBLANES:SUBLANES + tm, :] = jnp.dot(xb, wx_ref[...].astype(bf16), preferred_element_type=f32)
    wg = wg_ref[...].astype(bf16)
    g_top = jnp.dot(xb[:tm // 2], wg, preferred_element_type=f32)

    neg_lam = -lam_ref[...]
    half_scale = (-0.5 * LRU_C) * (jnp.maximum(neg_lam, 0.0) + jnp.log1p(jnp.exp(-jnp.abs(neg_lam))))
    half_br = 0.5 * br_ref[...]
    half_bi = 0.5 * bi_ref[...]
    for b in range(COL_TILE // RNN_BLOCK_DIM):
        c0 = b * RNN_BLOCK_DIM
        c1 = c0 + RNN_BLOCK_DIM
        xc = convb_ref[:, c0:c1]
        for kk in range(CONV_WIDTH):
            off = SUBLANES - (CONV_WIDTH - 1) + kk
            xc = xc + convw_ref[kk:kk + 1, c0:c1] * hist_scr[off:off + tm, c0:c1]
        w_gates = (0.5 * jnp.concatenate([wr_ref[b], wi_ref[b]], axis=1)).astype(bf16)
        ri = jnp.dot(xc.astype(bf16), w_gates, preferred_element_type=f32)
        if b == 0:
            g = jnp.concatenate([g_top, jnp.dot(xb[tm // 2:], wg, preferred_element_type=f32)], axis=0)
        tanh_r = jnp.tanh(ri[:, :RNN_BLOCK_DIM] + half_br[:, c0:c1])
        tanh_i = jnp.tanh(ri[:, RNN_BLOCK_DIM:] + half_bi[:, c0:c1])
        log_a = tanh_r * half_scale[:, c0:c1] + half_scale[:, c0:c1]
        a = jnp.exp(log_a)
        one_m_a2 = jnp.tanh(log_a) * (-1.0 - a * a)
        mult = jnp.where(one_m_a2 > 0.0, one_m_a2 * lax.rsqrt(one_m_a2), 0.0)
        u = (mult * xc) * (0.5 * tanh_i + 0.5)
        h, carry = _linear_scan(a, u, carry_all[:, c0:c1])
        hcar_scr[:, c0:c1] = carry
        gb = g[:, c0:c1]
        yr_ref[:, c0:c1] = (h * (gb * _sigmoid(gb))).astype(yr_ref.dtype)


def _rnn_projection(x2d, w_in, layer, w_r, w_i, b_r, b_i, lam, convw, convb, *, seq):
    m, k = x2d.shape
    tm = RNN_TM
    nblk = COL_TILE // RNN_BLOCK_DIM
    gate_w = pl.BlockSpec((None, nblk, RNN_BLOCK_DIM, RNN_BLOCK_DIM), lambda c, i: (layer, c, 0, 0))
    row = pl.BlockSpec((None, 1, COL_TILE), lambda c, i: (layer, 0, c))
    return pl.pallas_call(
        functools.partial(_rnn_proj_kernel, tiles_per_seq=seq // tm),
        grid=(RNN_WIDTH // COL_TILE, m // tm),
        in_specs=[pl.BlockSpec((tm, k), lambda c, i: (i, 0)),
                  pl.BlockSpec((None, k, COL_TILE), lambda c, i: (layer, 0, XR0 // COL_TILE + c)),
                  pl.BlockSpec((None, k, COL_TILE), lambda c, i: (layer, 0, GR0 // COL_TILE + c)),
                  gate_w, gate_w, row, row, row,
                  pl.BlockSpec((None, CONV_WIDTH, COL_TILE), lambda c, i: (layer, 0, c)),
                  row],
        out_specs=pl.BlockSpec((tm, COL_TILE), lambda c, i: (i, c)),
        out_shape=jax.ShapeDtypeStruct((m, RNN_WIDTH), jnp.bfloat16),
        scratch_shapes=[pltpu.VMEM((SUBLANES + tm, COL_TILE), jnp.float32),
                        pltpu.VMEM((1, COL_TILE), jnp.float32)],
        compiler_params=pltpu.CompilerParams(
            dimension_semantics=("arbitrary", "arbitrary"), vmem_limit_bytes=VMEM_LIMIT),
        name="rnn_projection",
    )(x2d, w_in, w_in, w_r, w_i, b_r, b_i, lam, convw, convb)


def _mixer_kernel(relb_ref, sinks_ref,
                  pa_ref, yr_ref, x_ref, bucket_ref, wao_ref, wro_ref, wo_ref, lng_ref, lnb_ref,
                  out_ref,
                  bias_scr, kx_scr, vx_scr, attn_scr,
                  *, alpha):
    tm = MIX_TM
    nqb = tm // WINDOW
    first_tile = pl.program_id(1) == 0
    f32 = jnp.float32
    bf16 = jnp.bfloat16

    @pl.when((pl.program_id(0) == 0) & first_tile)
    def _():
        bucket = bucket_ref[...]
        for t in range(N_Q_HEADS):
            h = 4 * (t // 4) + 2 * (t % 2) + (t // 2) % 2
            acc = jnp.full((WINDOW, 2 * WINDOW), NEG_INF, f32)
            for b in range(N_BUCKETS):
                acc = jnp.where(bucket == b, relb_ref[b, h], acc)
            bias_scr[t * WINDOW:(t + 1) * WINDOW, :] = acc

    for scr in (kx_scr, vx_scr):
        prev = scr[:, tm:tm + WINDOW, :]
        scr[:, 0:WINDOW, :] = jnp.where(first_tile, jnp.zeros_like(prev), prev)

    u32 = jnp.uint32
    lo_u = lax.broadcasted_iota(jnp.int32, (tm // 2, LANES), 1) < HEAD_DIM
    zero_u = jnp.zeros((tm // 2, LANES), u32)
    for p in range(N_KV_HEADS // 2):
        for src0, scr in ((PA_K0, kx_scr), (PA_V0, vx_scr)):
            t = pltpu.bitcast(pa_ref[:, src0 + p * LANES: src0 + (p + 1) * LANES], u32)
            tr = pltpu.roll(t, HEAD_DIM, axis=1)
            variants = (jnp.where(lo_u, t, zero_u),
                        jnp.where(lo_u, zero_u, tr),
                        jnp.where(lo_u, tr, zero_u),
                        jnp.where(lo_u, zero_u, t))
            for v, val in enumerate(variants):
                scr[4 * p + v, WINDOW:WINDOW + tm, :] = pltpu.bitcast(val, bf16)

    lo_q = lax.broadcasted_iota(jnp.int32, (WINDOW, LANES), 1) < HEAD_DIM
    col = lax.broadcasted_iota(jnp.int32, (1, 2 * WINDOW), 1)
    ones_blk = jnp.ones((2 * WINDOW, LANES), bf16)
    nt_dims = (((1,), (1,)), ((), ()))
    for j in range(nqb):
        r0 = j * WINDOW
        s_parts = []
        for kh in range(N_KV_HEADS):
            qq = jnp.concatenate(
                [pa_ref[r0:r0 + WINDOW, PA_Q0 + (2 * kh + pr) * LANES: PA_Q0 + (2 * kh + pr + 1) * LANES]
                 for pr in range(2)], axis=0) * jnp.asarray(HEAD_DIM ** -0.5, bf16)
            for par in range(2):
                kk = kx_scr[2 * kh + par, r0:r0 + 2 * WINDOW, :]
                s_parts.append(lax.dot_general(qq, kk, nt_dims, preferred_element_type=f32))
        s = jnp.concatenate(s_parts, axis=0) + bias_scr[...]
        if j == 0:
            s = s + jnp.where(first_tile & (col < WINDOW), NEG_INF, 0.0).astype(f32)
        m = jnp.max(s, axis=-1, keepdims=True)
        p = jnp.exp(s - m).astype(bf16)
        for kh in range(N_KV_HEADS):
            o_ext = []
            for par in range(2):
                t0 = (2 * kh + par) * 2 * WINDOW
                vv = jnp.concatenate([vx_scr[2 * kh + par, r0:r0 + 2 * WINDOW, :], ones_blk], axis=1)
                o_ext.append(jnp.dot(p[t0:t0 + 2 * WINDOW, :], vv, preferred_element_type=f32))
            for pr in range(2):
                pair = 2 * kh + pr
                rows = slice(pr * WINDOW, (pr + 1) * WINDOW)
                rden = []
                for par in range(2):
                    t0 = (2 * kh + par) * 2 * WINDOW + pr * WINDOW
                    sink_term = jnp.exp(sinks_ref[2 * pair + par] - m[t0:t0 + WINDOW, :])
                    rden.append(1.0 / (o_ext[par][rows, LANES:] + sink_term))
                o = (o_ext[0][rows, :LANES] + o_ext[1][rows, :LANES]) * jnp.where(lo_q, rden[0], rden[1])
                ga = pa_ref[r0:r0 + WINDOW, PA_GA0 + pair * LANES: PA_GA0 + (pair + 1) * LANES].astype(f32)
                attn_scr[r0:r0 + WINDOW, pair * LANES:(pair + 1) * LANES] = (
                    o * (ga * _sigmoid(ga))).astype(bf16)

    y_attn = jnp.dot(attn_scr[...], wao_ref[...], preferred_element_type=f32)
    y_rnn = jnp.dot(yr_ref[...], wro_ref[...], preferred_element_type=f32)
    mixed = (_sigmoid(pa_ref[:, PA_MA0:PA_MA0 + D_MODEL].astype(f32)) * y_attn
             + _sigmoid(pa_ref[:, PA_MR0:PA_MR0 + D_MODEL].astype(f32)) * y_rnn)
    out = jnp.dot(mixed.astype(bf16), wo_ref[...], preferred_element_type=f32)
    z = alpha * x_ref[...] + out
    mu = jnp.mean(z, axis=-1, keepdims=True)
    zc = z - mu
    var = jnp.mean(zc * zc, axis=-1, keepdims=True)
    out_ref[...] = zc * lax.rsqrt(var + LN_EPS) * lng_ref[...] + lnb_ref[...]


def _const_spec(shape):
    nd = len(shape)
    return pl.BlockSpec(shape, lambda b, s: (0,) * nd, pipeline_mode=pl.Buffered(1))


def _mixer(pa, yr, x2d, bucket, relb, sinks, wao, wro, wo, lng, lnb, *, batch, seq, alpha):
    tm = MIX_TM
    nst = seq // tm
    row_map = lambda b, s: (b * nst + s, 0)
    smem = pl.BlockSpec(memory_space=pltpu.SMEM)
    in_specs = [
        smem, smem,
        pl.BlockSpec((tm, PA_WIDTH), row_map),
        pl.BlockSpec((tm, RNN_WIDTH), row_map),
        pl.BlockSpec((tm, D_MODEL), row_map),
        _const_spec(bucket.shape), _const_spec(wao.shape), _const_spec(wro.shape), _const_spec(wo.shape),
        _const_spec(lng.shape), _const_spec(lnb.shape),
    ]
    scratch = [
        pltpu.VMEM((N_Q_HEADS * WINDOW, 2 * WINDOW), jnp.float32),
        pltpu.VMEM((8, WINDOW + tm, LANES), jnp.bfloat16),
        pltpu.VMEM((8, WINDOW + tm, LANES), jnp.bfloat16),
        pltpu.VMEM((tm, ATTN_WIDTH), jnp.bfloat16),
    ]
    return pl.pallas_call(
        functools.partial(_mixer_kernel, alpha=alpha),
        grid=(batch, nst),
        in_specs=in_specs,
        out_specs=pl.BlockSpec((tm, D_MODEL), row_map),
        out_shape=jax.ShapeDtypeStruct((batch * seq, D_MODEL), jnp.float32),
        scratch_shapes=scratch,
        compiler_params=pltpu.CompilerParams(
            dimension_semantics=("arbitrary", "arbitrary"), vmem_limit_bytes=VMEM_LIMIT),
        name="mixer",
    )(relb, sinks, pa, yr, x2d, bucket, wao, wro, wo, lng, lnb)


def kernel(x, w_in, conv_w, conv_b, w_r, b_r, w_i, b_i, lru_lambda, sinks, w_attn_out, w_rnn_out, w_out,
           ln_g, ln_b, rel_bias):
    batch, seq, d = x.shape
    depth = w_in.shape[0]
    alpha = (2.0 * depth) ** 0.25
    bucket = jnp.asarray(_t5_bucket_table())
    x2d = x.reshape(batch * seq, d)
    as_rows = lambda v: v.reshape(depth, 1, v.shape[-1])
    b_r, b_i, lam, conv_b = as_rows(b_r), as_rows(b_i), as_rows(lru_lambda), as_rows(conv_b)
    for l in range(depth):
        pa, wao, wro, wo = _plain_projection(x2d, w_in, w_attn_out, w_rnn_out, w_out, l)
        yr = _rnn_projection(x2d, w_in, l, w_r, w_i, b_r, b_i, lam, conv_w, conv_b, seq=seq)
        x2d = _mixer(pa, yr, x2d, bucket, rel_bias, sinks[l], wao, wro, wo,
                     ln_g[l][None, :], ln_b[l][None, :], batch=batch, seq=seq, alpha=alpha)
    return x2d.reshape(batch, seq, d)
```

```python
import functools
import math

import numpy as np
import jax
import jax.numpy as jnp
from jax import lax
from jax.experimental import pallas as pl
from jax.experimental.pallas import tpu as pltpu

D_MODEL = 2048
N_Q_HEADS = 16
N_KV_HEADS = 4
HEAD_DIM = 64
ATTN_WIDTH = N_Q_HEADS * HEAD_DIM
KV_WIDTH = N_KV_HEADS * HEAD_DIM
WINDOW = 128
N_BUCKETS = 32
MAX_DISTANCE = 128
RNN_WIDTH = D_MODEL
RNN_BLOCKS = 16
RNN_BLOCK_DIM = RNN_WIDTH // RNN_BLOCKS
CONV_WIDTH = 4
LRU_C = 8.0
LN_EPS = 1e-5
NEG_INF = -1e30

Q0 = 0
K0 = Q0 + ATTN_WIDTH
V0 = K0 + KV_WIDTH
GA0 = V0 + KV_WIDTH
XR0 = GA0 + ATTN_WIDTH
GR0 = XR0 + RNN_WIDTH
MA0 = GR0 + RNN_WIDTH
MR0 = MA0 + D_MODEL
IN_WIDTH = MR0 + D_MODEL

LANES = 128
SUBLANES = 8
VMEM_LIMIT = 60 * 1024 * 1024

COL_TILE = 512
PLAIN_TM = 2048
RNN_TM = 1024
MIX_TM = 256
CAST_ROWS = 64

PA_Q0 = 0
PA_K0 = PA_Q0 + ATTN_WIDTH
PA_V0 = PA_K0 + KV_WIDTH
PA_GA0 = PA_V0 + KV_WIDTH
PA_MA0 = PA_GA0 + ATTN_WIDTH
PA_MR0 = PA_MA0 + D_MODEL
PA_WIDTH = PA_MR0 + D_MODEL
N_PLAIN_TILES = PA_WIDTH // COL_TILE
N_PRE_RNN_TILES = XR0 // COL_TILE
N_RNN_COL_TILES = (MA0 - XR0) // COL_TILE


def _t5_bucket_table():
    qi = np.arange(WINDOW)[:, None]
    kj = np.arange(2 * WINDOW)[None, :]
    dist = qi + WINDOW - kj
    max_exact = N_BUCKETS // 2
    n = np.maximum(dist, 0)
    nf = np.maximum(n, max_exact).astype(np.float32)
    large = max_exact + (np.log(nf / np.float32(max_exact)) / np.float32(math.log(MAX_DISTANCE / max_exact))
                         * np.float32(N_BUCKETS - max_exact)).astype(np.int32)
    large = np.minimum(large, N_BUCKETS - 1)
    bucket = np.where(n < max_exact, n, large)
    in_window = (dist >= 0) & (dist < WINDOW)
    return np.where(in_window, bucket, -1).astype(np.int32)


def _sigmoid(x):
    return 0.5 * jnp.tanh(0.5 * x) + 0.5


def _plain_proj_kernel(x_ref, w_ref, pa_ref):
    pa_ref[...] = jnp.dot(x_ref[...].astype(jnp.bfloat16), w_ref[...].astype(jnp.bfloat16),
                          preferred_element_type=jnp.float32).astype(pa_ref.dtype)


def _plain_projection(x2d, w_in, layer):
    m, k = x2d.shape
    tm = PLAIN_TM

    def w_map(i, n):
        return (layer, 0, jnp.where(n < N_PRE_RNN_TILES, n, n + N_RNN_COL_TILES))

    return pl.pallas_call(
        _plain_proj_kernel,
        grid=(m // tm, N_PLAIN_TILES),
        in_specs=[pl.BlockSpec((tm, k), lambda i, n: (i, 0)),
                  pl.BlockSpec((None, k, COL_TILE), w_map)],
        out_specs=pl.BlockSpec((tm, COL_TILE), lambda i, n: (i, n)),
        out_shape=jax.ShapeDtypeStruct((m, PA_WIDTH), jnp.bfloat16),
        compiler_params=pltpu.CompilerParams(
            dimension_semantics=("arbitrary", "arbitrary"), vmem_limit_bytes=VMEM_LIMIT),
        name="plain_projection",
    )(x2d, w_in)


def _linear_scan(a, u, carry):
    tm = a.shape[0]
    groups = tm // SUBLANES
    a3 = a.reshape(groups, SUBLANES, LANES)
    u3 = u.reshape(groups, SUBLANES, LANES)
    row = lax.broadcasted_iota(jnp.int32, (groups, SUBLANES, LANES), 1)
    for d in (1, 2, 4):
        keep = row >= d
        a_sh = pltpu.roll(a3, d, axis=1)
        u_sh = pltpu.roll(u3, d, axis=1)
        u3 = jnp.where(keep, a3 * u_sh + u3, u3)
        a3 = jnp.where(keep, a3 * a_sh, a3)
    hs = []
    for g in range(groups):
        hg = u3[g] + a3[g] * carry
        hs.append(hg)
        carry = hg[SUBLANES - 1:SUBLANES, :]
    return jnp.concatenate(hs, axis=0), carry


def _rnn_proj_kernel(x_ref, wx_ref, wg_ref, wr_ref, wi_ref, br_ref, bi_ref, lam_ref, convw_ref, convb_ref,
                     wao_ref, wro_ref, wo_ref,
                     yr_ref, wao_bf_ref, wro_bf_ref, wo_bf_ref, hist_scr, hcar_scr, *, tiles_per_seq):
    tm = RNN_TM
    f32 = jnp.float32
    bf16 = jnp.bfloat16
    first = (pl.program_id(1) % tiles_per_seq) == 0

    hist_scr[0:SUBLANES, :] = jnp.where(first, 0.0, hist_scr[tm:tm + SUBLANES, :])
    carry_all = jnp.where(first, 0.0, hcar_scr[...])

    xb = x_ref[...].astype(bf16)
    hist_scr[SUBLANES:SUBLANES + tm, :] = jnp.dot(xb, wx_ref[...].astype(bf16), preferred_element_type=f32)
    wao_bf_ref[...] = wao_ref[...].astype(bf16)
    wro_bf_ref[...] = wro_ref[...].astype(bf16)
    wo_bf_ref[...] = wo_ref[...].astype(bf16)
    wg = wg_ref[...].astype(bf16)
    g_top = jnp.dot(xb[:tm // 2], wg, preferred_element_type=f32)

    neg_lam = -lam_ref[...]
    half_scale = (-0.5 * LRU_C) * (jnp.maximum(neg_lam, 0.0) + jnp.log1p(jnp.exp(-jnp.abs(neg_lam))))
    half_br = 0.5 * br_ref[...]
    half_bi = 0.5 * bi_ref[...]
    for b in range(COL_TILE // RNN_BLOCK_DIM):
        c0 = b * RNN_BLOCK_DIM
        c1 = c0 + RNN_BLOCK_DIM
        xc = convb_ref[:, c0:c1]
        for kk in range(CONV_WIDTH):
            off = SUBLANES - (CONV_WIDTH - 1) + kk
            xc = xc + convw_ref[kk:kk + 1, c0:c1] * hist_scr[off:off + tm, c0:c1]
        w_gates = (0.5 * jnp.concatenate([wr_ref[b], wi_ref[b]], axis=1)).astype(bf16)
        ri = jnp.dot(xc.astype(bf16), w_gates, preferred_element_type=f32)
        if b == 0:
            g = jnp.concatenate([g_top, jnp.dot(xb[tm // 2:], wg, preferred_element_type=f32)], axis=0)
        tanh_r = jnp.tanh(ri[:, :RNN_BLOCK_DIM] + half_br[:, c0:c1])
        tanh_i = jnp.tanh(ri[:, RNN_BLOCK_DIM:] + half_bi[:, c0:c1])
        log_a = tanh_r * half_scale[:, c0:c1] + half_scale[:, c0:c1]
        a = jnp.exp(log_a)
        one_m_a2 = jnp.tanh(log_a) * (-1.0 - a * a)
        mult = jnp.where(one_m_a2 > 0.0, one_m_a2 * lax.rsqrt(one_m_a2), 0.0)
        u = (mult * xc) * (0.5 * tanh_i + 0.5)
        h, carry = _linear_scan(a, u, carry_all[:, c0:c1])
        hcar_scr[:, c0:c1] = carry
        gb = g[:, c0:c1]
        yr_ref[:, c0:c1] = (h * (gb * _sigmoid(gb))).astype(yr_ref.dtype)


def _rnn_projection(x2d, w_in, layer, w_r, w_i, b_r, b_i, lam, convw, convb, wao, wro, wo, *, seq):
    m, k = x2d.shape
    tm = RNN_TM
    nblk = COL_TILE // RNN_BLOCK_DIM
    grid = (RNN_WIDTH // COL_TILE, m // tm)
    gate_w = pl.BlockSpec((None, nblk, RNN_BLOCK_DIM, RNN_BLOCK_DIM), lambda c, i: (layer, c, 0, 0))
    row = pl.BlockSpec((None, 1, COL_TILE), lambda c, i: (layer, 0, c))

    weights = (wao, wro, wo)
    cast_in, cast_out = [], []
    for w in weights:
        nb = w.shape[1] // CAST_ROWS
        assert nb <= grid[0] * grid[1]
        blk = lambda c, i, nb=nb: jnp.minimum(c * grid[1] + i, nb - 1)
        cast_in.append(pl.BlockSpec((None, CAST_ROWS, w.shape[2]), lambda c, i, blk=blk: (layer, blk(c, i), 0)))
        cast_out.append(pl.BlockSpec((CAST_ROWS, w.shape[2]), lambda c, i, blk=blk: (blk(c, i), 0)))

    return pl.pallas_call(
        functools.partial(_rnn_proj_kernel, tiles_per_seq=seq // tm),
        grid=grid,
        in_specs=[pl.BlockSpec((tm, k), lambda c, i: (i, 0)),
                  pl.BlockSpec((None, k, COL_TILE), lambda c, i: (layer, 0, XR0 // COL_TILE + c)),
                  pl.BlockSpec((None, k, COL_TILE), lambda c, i: (layer, 0, GR0 // COL_TILE + c)),
                  gate_w, gate_w, row, row, row,
                  pl.BlockSpec((None, CONV_WIDTH, COL_TILE), lambda c, i: (layer, 0, c)),
                  row] + cast_in,
        out_specs=[pl.BlockSpec((tm, COL_TILE), lambda c, i: (i, c))] + cast_out,
        out_shape=[jax.ShapeDtypeStruct((m, RNN_WIDTH), jnp.bfloat16)]
                  + [jax.ShapeDtypeStruct(w.shape[1:], jnp.bfloat16) for w in weights],
        scratch_shapes=[pltpu.VMEM((SUBLANES + tm, COL_TILE), jnp.float32),
                        pltpu.VMEM((1, COL_TILE), jnp.float32)],
        compiler_params=pltpu.CompilerParams(
            dimension_semantics=("arbitrary", "arbitrary"), vmem_limit_bytes=VMEM_LIMIT),
        name="rnn_projection",
    )(x2d, w_in, w_in, w_r, w_i, b_r, b_i, lam, convw, convb, wao, wro, wo)


def _mixer_kernel(relb_ref, sinks_ref,
                  pa_ref, yr_ref, x_ref, bucket_ref, wao_ref, wro_ref, wo_ref, lng_ref, lnb_ref,
                  out_ref,
                  bias_scr, kx_scr, vx_scr, attn_scr,
                  *, alpha):
    tm = MIX_TM
    nqb = tm // WINDOW
    first_tile = pl.program_id(1) == 0
    f32 = jnp.float32
    bf16 = jnp.bfloat16

    @pl.when((pl.program_id(0) == 0) & first_tile)
    def _():
        bucket = bucket_ref[...]
        for t in range(N_Q_HEADS):
            h = 4 * (t // 4) + 2 * (t % 2) + (t // 2) % 2
            acc = jnp.full((WINDOW, 2 * WINDOW), NEG_INF, f32)
            for b in range(N_BUCKETS):
                acc = jnp.where(bucket == b, relb_ref[b, h], acc)
            bias_scr[t * WINDOW:(t + 1) * WINDOW, :] = acc

    for scr in (kx_scr, vx_scr):
        prev = scr[:, tm:tm + WINDOW, :]
        scr[:, 0:WINDOW, :] = jnp.where(first_tile, jnp.zeros_like(prev), prev)

    u32 = jnp.uint32
    lo_u = lax.broadcasted_iota(jnp.int32, (tm // 2, LANES), 1) < HEAD_DIM
    zero_u = jnp.zeros((tm // 2, LANES), u32)
    for p in range(N_KV_HEADS // 2):
        for src0, scr in ((PA_K0, kx_scr), (PA_V0, vx_scr)):
            t = pltpu.bitcast(pa_ref[:, src0 + p * LANES: src0 + (p + 1) * LANES], u32)
            tr = pltpu.roll(t, HEAD_DIM, axis=1)
            variants = (jnp.where(lo_u, t, zero_u),
                        jnp.where(lo_u, zero_u, tr),
                        jnp.where(lo_u, tr, zero_u),
                        jnp.where(lo_u, zero_u, t))
            for v, val in enumerate(variants):
                scr[4 * p + v, WINDOW:WINDOW + tm, :] = pltpu.bitcast(val, bf16)

    lo_q = lax.broadcasted_iota(jnp.int32, (WINDOW, LANES), 1) < HEAD_DIM
    col = lax.broadcasted_iota(jnp.int32, (1, 2 * WINDOW), 1)
    ones_blk = jnp.ones((2 * WINDOW, LANES), bf16)
    nt_dims = (((1,), (1,)), ((), ()))
    y_rnn_parts = []
    for j in range(nqb):
        r0 = j * WINDOW
        s_parts = []
        for kh in range(N_KV_HEADS):
            qq = jnp.concatenate(
                [pa_ref[r0:r0 + WINDOW, PA_Q0 + (2 * kh + pr) * LANES: PA_Q0 + (2 * kh + pr + 1) * LANES]
                 for pr in range(2)], axis=0) * jnp.asarray(HEAD_DIM ** -0.5, bf16)
            for par in range(2):
                kk = kx_scr[2 * kh + par, r0:r0 + 2 * WINDOW, :]
                s_parts.append(lax.dot_general(qq, kk, nt_dims, preferred_element_type=f32))
        cw = D_MODEL // nqb
        y_rnn_parts.append(jnp.dot(yr_ref[...], wro_ref[:, j * cw:(j + 1) * cw], preferred_element_type=f32))
        s = jnp.concatenate(s_parts, axis=0) + bias_scr[...]
        if j == 0:
            s = s + jnp.where(first_tile & (col < WINDOW), NEG_INF, 0.0).astype(f32)
        m = jnp.max(s, axis=-1, keepdims=True)
        p = jnp.exp(s - m).astype(bf16)
        for kh in range(N_KV_HEADS):
            o_ext = []
            for par in range(2):
                t0 = (2 * kh + par) * 2 * WINDOW
                vv = jnp.concatenate([vx_scr[2 * kh + par, r0:r0 + 2 * WINDOW, :], ones_blk], axis=1)
                o_ext.append(jnp.dot(p[t0:t0 + 2 * WINDOW, :], vv, preferred_element_type=f32))
            for pr in range(2):
                pair = 2 * kh + pr
                rows = slice(pr * WINDOW, (pr + 1) * WINDOW)
                rden = []
                for par in range(2):
                    t0 = (2 * kh + par) * 2 * WINDOW + pr * WINDOW
                    sink_term = jnp.exp(sinks_ref[2 * pair + par] - m[t0:t0 + WINDOW, :])
                    rden.append(1.0 / (o_ext[par][rows, LANES:] + sink_term))
                o = (o_ext[0][rows, :LANES] + o_ext[1][rows, :LANES]) * jnp.where(lo_q, rden[0], rden[1])
                ga = pa_ref[r0:r0 + WINDOW, PA_GA0 + pair * LANES: PA_GA0 + (pair + 1) * LANES].astype(f32)
                attn_scr[r0:r0 + WINDOW, pair * LANES:(pair + 1) * LANES] = (
                    o * (ga * _sigmoid(ga))).astype(bf16)

    y_attn = jnp.dot(attn_scr[...], wao_ref[...], preferred_element_type=f32)
    y_rnn = jnp.concatenate(y_rnn_parts, axis=1)
    mixed = (_sigmoid(pa_ref[:, PA_MA0:PA_MA0 + D_MODEL].astype(f32)) * y_attn
             + _sigmoid(pa_ref[:, PA_MR0:PA_MR0 + D_MODEL].astype(f32)) * y_rnn).astype(bf16)
    zs = []
    row_sum = jnp.zeros((tm, 1), f32)
    for c in range(D_MODEL // COL_TILE):
        k0 = c * COL_TILE
        z = alpha * x_ref[:, k0:k0 + COL_TILE] + jnp.dot(mixed, wo_ref[:, k0:k0 + COL_TILE],
                                                         preferred_element_type=f32)
        row_sum = row_sum + jnp.sum(z, axis=-1, keepdims=True)
        zs.append(z)
    zc = jnp.concatenate(zs, axis=1) - row_sum * (1.0 / D_MODEL)
    var = jnp.mean(zc * zc, axis=-1, keepdims=True)
    out_ref[...] = zc * lax.rsqrt(var + LN_EPS) * lng_ref[...] + lnb_ref[...]


def _const_spec(shape):
    nd = len(shape)
    return pl.BlockSpec(shape, lambda b, s: (0,) * nd, pipeline_mode=pl.Buffered(1))


def _mixer(pa, yr, x2d, bucket, relb, sinks, wao, wro, wo, lng, lnb, *, batch, seq, alpha):
    tm = MIX_TM
    nst = seq // tm
    row_map = lambda b, s: (b * nst + s, 0)
    smem = pl.BlockSpec(memory_space=pltpu.SMEM)
    in_specs = [
        smem, smem,
        pl.BlockSpec((tm, PA_WIDTH), row_map),
        pl.BlockSpec((tm, RNN_WIDTH), row_map),
        pl.BlockSpec((tm, D_MODEL), row_map),
        _const_spec(bucket.shape), _const_spec(wao.shape), _const_spec(wro.shape), _const_spec(wo.shape),
        _const_spec(lng.shape), _const_spec(lnb.shape),
    ]
    scratch = [
        pltpu.VMEM((N_Q_HEADS * WINDOW, 2 * WINDOW), jnp.float32),
        pltpu.VMEM((8, WINDOW + tm, LANES), jnp.bfloat16),
        pltpu.VMEM((8, WINDOW + tm, LANES), jnp.bfloat16),
        pltpu.VMEM((tm, ATTN_WIDTH), jnp.bfloat16),
    ]
    return pl.pallas_call(
        functools.partial(_mixer_kernel, alpha=alpha),
        grid=(batch, nst),
        in_specs=in_specs,
        out_specs=pl.BlockSpec((tm, D_MODEL), row_map),
        out_shape=jax.ShapeDtypeStruct((batch * seq, D_MODEL), jnp.float32),
        scratch_shapes=scratch,
        compiler_params=pltpu.CompilerParams(
            dimension_semantics=("arbitrary", "arbitrary"), vmem_limit_bytes=VMEM_LIMIT),
        name="mixer",
    )(relb, sinks, pa, yr, x2d, bucket, wao, wro, wo, lng, lnb)


def kernel(x, w_in, conv_w, conv_b, w_r, b_r, w_i, b_i, lru_lambda, sinks, w_attn_out, w_rnn_out, w_out,
           ln_g, ln_b, rel_bias):
    batch, seq, d = x.shape
    depth = w_in.shape[0]
    alpha = (2.0 * depth) ** 0.25
    bucket = jnp.asarray(_t5_bucket_table())
    x2d = x.reshape(batch * seq, d)
    as_rows = lambda v: v.reshape(depth, 1, v.shape[-1])
    b_r, b_i, lam, conv_b = as_rows(b_r), as_rows(b_i), as_rows(lru_lambda), as_rows(conv_b)
    for l in range(depth):
        pa = _plain_projection(x2d, w_in, l)
        yr, wao, wro, wo = _rnn_projection(x2d, w_in, l, w_r, w_i, b_r, b_i, lam, conv_w, conv_b,
                                           w_attn_out, w_rnn_out, w_out, seq=seq)
        x2d = _mixer(pa, yr, x2d, bucket, rel_bias, sinks[l], wao, wro, wo,
                     ln_g[l][None, :], ln_b[l][None, :], batch=batch, seq=seq, alpha=alpha)
    return x2d.reshape(batch, seq, d)
```

```python
import functools
import math

import numpy as np
import jax
import jax.numpy as jnp
from jax import lax
from jax.experimental import pallas as pl
from jax.experimental.pallas import tpu as pltpu

D_MODEL = 2048
N_Q_HEADS = 16
N_KV_HEADS = 4
HEAD_DIM = 64
ATTN_WIDTH = N_Q_HEADS * HEAD_DIM
KV_WIDTH = N_KV_HEADS * HEAD_DIM
WINDOW = 128
N_BUCKETS = 32
MAX_DISTANCE = 128
RNN_WIDTH = D_MODEL
RNN_BLOCKS = 16
RNN_BLOCK_DIM = RNN_WIDTH // RNN_BLOCKS
CONV_WIDTH = 4
LRU_C = 8.0
LN_EPS = 1e-5
NEG_INF = -1e30

Q0 = 0
K0 = Q0 + ATTN_WIDTH
V0 = K0 + KV_WIDTH
GA0 = V0 + KV_WIDTH
XR0 = GA0 + ATTN_WIDTH
GR0 = XR0 + RNN_WIDTH
MA0 = GR0 + RNN_WIDTH
MR0 = MA0 + D_MODEL
IN_WIDTH = MR0 + D_MODEL

LANES = 128
SUBLANES = 8
VMEM_LIMIT = 60 * 1024 * 1024

COL_TILE = 512
PLAIN_TM = 1024
RNN_TM = 1024
MIX_TM = 256
CAST_ROWS = 64

PA_Q0 = 0
PA_K0 = PA_Q0 + ATTN_WIDTH
PA_V0 = PA_K0 + KV_WIDTH
PA_GA0 = PA_V0 + KV_WIDTH
PA_WIDTH = PA_GA0 + ATTN_WIDTH
PM_MA0 = 0
PM_MR0 = PM_MA0 + D_MODEL
PM_WIDTH = PM_MR0 + D_MODEL
PA_TILE = PA_WIDTH // 2
PM_TILE = PM_WIDTH // 4


def _t5_bucket_table():
    qi = np.arange(WINDOW)[:, None]
    kj = np.arange(2 * WINDOW)[None, :]
    dist = qi + WINDOW - kj
    max_exact = N_BUCKETS // 2
    n = np.maximum(dist, 0)
    nf = np.maximum(n, max_exact).astype(np.float32)
    large = max_exact + (np.log(nf / np.float32(max_exact)) / np.float32(math.log(MAX_DISTANCE / max_exact))
                         * np.float32(N_BUCKETS - max_exact)).astype(np.int32)
    large = np.minimum(large, N_BUCKETS - 1)
    bucket = np.where(n < max_exact, n, large)
    in_window = (dist >= 0) & (dist < WINDOW)
    return np.where(in_window, bucket, -1).astype(np.int32)


def _sigmoid(x):
    return 0.5 * jnp.tanh(0.5 * x) + 0.5


def _plain_proj_kernel(x_ref, w_ref, pa_ref):
    pa_ref[...] = jnp.dot(x_ref[...].astype(jnp.bfloat16), w_ref[0].astype(jnp.bfloat16),
                          preferred_element_type=jnp.float32).astype(pa_ref.dtype)


def _plain_projection(x2d, w_in, layer, col0, width, tn, name):
    m, k = x2d.shape
    tm = PLAIN_TM
    return pl.pallas_call(
        _plain_proj_kernel,
        grid=(m // tm, width // tn),
        in_specs=[pl.BlockSpec((tm, k), lambda i, n: (i, 0)),
                  pl.BlockSpec((pl.Element(1), pl.Element(k), pl.Element(tn)),
                               lambda i, n: (layer, 0, pl.multiple_of(col0 + n * tn, LANES)))],
        out_specs=pl.BlockSpec((tm, tn), lambda i, n: (i, n)),
        out_shape=jax.ShapeDtypeStruct((m, width), jnp.bfloat16),
        compiler_params=pltpu.CompilerParams(
            dimension_semantics=("arbitrary", "arbitrary"), vmem_limit_bytes=VMEM_LIMIT),
        name=name,
    )(x2d, w_in)


def _linear_scan(a, u, carry):
    tm = a.shape[0]
    groups = tm // SUBLANES
    a3 = a.reshape(groups, SUBLANES, LANES)
    u3 = u.reshape(groups, SUBLANES, LANES)
    row = lax.broadcasted_iota(jnp.int32, (groups, SUBLANES, LANES), 1)
    for d in (1, 2, 4):
        keep = row >= d
        a_sh = pltpu.roll(a3, d, axis=1)
        u_sh = pltpu.roll(u3, d, axis=1)
        u3 = jnp.where(keep, a3 * u_sh + u3, u3)
        a3 = jnp.where(keep, a3 * a_sh, a3)
    hs = []
    for g in range(groups):
        hg = u3[g] + a3[g] * carry
        hs.append(hg)
        carry = hg[SUBLANES - 1:SUBLANES, :]
    return jnp.concatenate(hs, axis=0), carry


def _rnn_proj_kernel(x_ref, wx_ref, wg_ref, wr_ref, wi_ref, br_ref, bi_ref, lam_ref, convw_ref, convb_ref,
                     wao_ref, wro_ref, wo_ref,
                     yr_ref, wao_bf_ref, wro_bf_ref, wo_bf_ref, hist_scr, hcar_scr, *, tiles_per_seq):
    tm = RNN_TM
    f32 = jnp.float32
    bf16 = jnp.bfloat16
    first = (pl.program_id(1) % tiles_per_seq) == 0

    hist_scr[0:SUBLANES, :] = jnp.where(first, 0.0, hist_scr[tm:tm + SUBLANES, :])
    carry_all = jnp.where(first, 0.0, hcar_scr[...])

    xb = x_ref[...].astype(bf16)
    hist_scr[SUBLANES:SUBLANES + tm, :] = jnp.dot(xb, wx_ref[...].astype(bf16), preferred_element_type=f32)
    wao_bf_ref[...] = wao_ref[...].astype(bf16)
    wro_bf_ref[...] = wro_ref[...].astype(bf16)
    wo_bf_ref[...] = wo_ref[...].astype(bf16)
    wg = wg_ref[...].astype(bf16)
    g_top = jnp.dot(xb[:tm // 2], wg, preferred_element_type=f32)

    neg_lam = -lam_ref[...]
    half_scale = (-0.5 * LRU_C) * (jnp.maximum(neg_lam, 0.0) + jnp.log1p(jnp.exp(-jnp.abs(neg_lam))))
    half_br = 0.5 * br_ref[...]
    half_bi = 0.5 * bi_ref[...]
    for b in range(COL_TILE // RNN_BLOCK_DIM):
        c0 = b * RNN_BLOCK_DIM
        c1 = c0 + RNN_BLOCK_DIM
        xc = convb_ref[:, c0:c1]
        for kk in range(CONV_WIDTH):
            off = SUBLANES - (CONV_WIDTH - 1) + kk
            xc = xc + convw_ref[kk:kk + 1, c0:c1] * hist_scr[off:off + tm, c0:c1]
        w_gates = (0.5 * jnp.concatenate([wr_ref[b], wi_ref[b]], axis=1)).astype(bf16)
        ri = jnp.dot(xc.astype(bf16), w_gates, preferred_element_type=f32)
        if b == 0:
            g = jnp.concatenate([g_top, jnp.dot(xb[tm // 2:], wg, preferred_element_type=f32)], axis=0)
        tanh_r = jnp.tanh(ri[:, :RNN_BLOCK_DIM] + half_br[:, c0:c1])
        tanh_i = jnp.tanh(ri[:, RNN_BLOCK_DIM:] + half_bi[:, c0:c1])
        log_a = tanh_r * half_scale[:, c0:c1] + half_scale[:, c0:c1]
        a = jnp.exp(log_a)
        one_m_a2 = jnp.tanh(log_a) * (-1.0 - a * a)
        mult = jnp.where(one_m_a2 > 0.0, one_m_a2 * lax.rsqrt(one_m_a2), 0.0)
        u = (mult * xc) * (0.5 * tanh_i + 0.5)
        h, carry = _linear_scan(a, u, carry_all[:, c0:c1])
        hcar_scr[:, c0:c1] = carry
        gb = g[:, c0:c1]
        yr_ref[:, c0:c1] = (h * (gb * _sigmoid(gb))).astype(yr_ref.dtype)


def _rnn_projection(x2d, w_in, layer, w_r, w_i, b_r, b_i, lam, convw, convb, wao, wro, wo, *, seq):
    m, k = x2d.shape
    tm = RNN_TM
    nblk = COL_TILE // RNN_BLOCK_DIM
    grid = (RNN_WIDTH // COL_TILE, m // tm)
    gate_w = pl.BlockSpec((None, nblk, RNN_BLOCK_DIM, RNN_BLOCK_DIM), lambda c, i: (layer, c, 0, 0))
    row = pl.BlockSpec((None, 1, COL_TILE), lambda c, i: (layer, 0, c))

    weights = (wao, wro, wo)
    cast_in, cast_out = [], []
    for w in weights:
        nb = w.shape[1] // CAST_ROWS
        assert nb <= grid[0] * grid[1]
        blk = lambda c, i, nb=nb: jnp.minimum(c * grid[1] + i, nb - 1)
        cast_in.append(pl.BlockSpec((None, CAST_ROWS, w.shape[2]), lambda c, i, blk=blk: (layer, blk(c, i), 0)))
        cast_out.append(pl.BlockSpec((CAST_ROWS, w.shape[2]), lambda c, i, blk=blk: (blk(c, i), 0)))

    return pl.pallas_call(
        functools.partial(_rnn_proj_kernel, tiles_per_seq=seq // tm),
        grid=grid,
        in_specs=[pl.BlockSpec((tm, k), lambda c, i: (i, 0)),
                  pl.BlockSpec((None, k, COL_TILE), lambda c, i: (layer, 0, XR0 // COL_TILE + c)),
                  pl.BlockSpec((None, k, COL_TILE), lambda c, i: (layer, 0, GR0 // COL_TILE + c)),
                  gate_w, gate_w, row, row, row,
                  pl.BlockSpec((None, CONV_WIDTH, COL_TILE), lambda c, i: (layer, 0, c)),
                  row] + cast_in,
        out_specs=[pl.BlockSpec((tm, COL_TILE), lambda c, i: (i, c))] + cast_out,
        out_shape=[jax.ShapeDtypeStruct((m, RNN_WIDTH), jnp.bfloat16)]
                  + [jax.ShapeDtypeStruct(w.shape[1:], jnp.bfloat16) for w in weights],
        scratch_shapes=[pltpu.VMEM((SUBLANES + tm, COL_TILE), jnp.float32),
                        pltpu.VMEM((1, COL_TILE), jnp.float32)],
        compiler_params=pltpu.CompilerParams(
            dimension_semantics=("arbitrary", "arbitrary"), vmem_limit_bytes=VMEM_LIMIT),
        name="rnn_projection",
    )(x2d, w_in, w_in, w_r, w_i, b_r, b_i, lam, convw, convb, wao, wro, wo)


def _mixer_kernel(relb_ref, sinks_ref,
                  pa_ref, pm_ref, yr_ref, x_ref, bucket_ref, wao_ref, wro_ref, wo_ref, lng_ref, lnb_ref,
                  out_ref,
                  bias_scr, kx_scr, vx_scr, attn_scr,
                  *, alpha):
    tm = MIX_TM
    nqb = tm // WINDOW
    first_tile = pl.program_id(1) == 0
    f32 = jnp.float32
    bf16 = jnp.bfloat16

    @pl.when((pl.program_id(0) == 0) & first_tile)
    def _():
        bucket = bucket_ref[...]
        for t in range(N_Q_HEADS):
            h = 4 * (t // 4) + 2 * (t % 2) + (t // 2) % 2
            acc = jnp.full((WINDOW, 2 * WINDOW), NEG_INF, f32)
            for b in range(N_BUCKETS):
                acc = jnp.where(bucket == b, relb_ref[b, h], acc)
            bias_scr[t * WINDOW:(t + 1) * WINDOW, :] = acc

    for scr in (kx_scr, vx_scr):
        prev = scr[:, tm:tm + WINDOW, :]
        scr[:, 0:WINDOW, :] = jnp.where(first_tile, jnp.zeros_like(prev), prev)

    u32 = jnp.uint32
    lo_u = lax.broadcasted_iota(jnp.int32, (tm // 2, LANES), 1) < HEAD_DIM
    zero_u = jnp.zeros((tm // 2, LANES), u32)
    for p in range(N_KV_HEADS // 2):
        for src0, scr in ((PA_K0, kx_scr), (PA_V0, vx_scr)):
            t = pltpu.bitcast(pa_ref[:, src0 + p * LANES: src0 + (p + 1) * LANES], u32)
            tr = pltpu.roll(t, HEAD_DIM, axis=1)
            variants = (jnp.where(lo_u, t, zero_u),
                        jnp.where(lo_u, zero_u, tr),
                        jnp.where(lo_u, tr, zero_u),
                        jnp.where(lo_u, zero_u, t))
            for v, val in enumerate(variants):
                scr[4 * p + v, WINDOW:WINDOW + tm, :] = pltpu.bitcast(val, bf16)

    lo_q = lax.broadcasted_iota(jnp.int32, (WINDOW, LANES), 1) < HEAD_DIM
    col = lax.broadcasted_iota(jnp.int32, (1, 2 * WINDOW), 1)
    ones_blk = jnp.ones((2 * WINDOW, LANES), bf16)
    nt_dims = (((1,), (1,)), ((), ()))
    y_rnn_parts = []
    for j in range(nqb):
        r0 = j * WINDOW
        s_parts = []
        for kh in range(N_KV_HEADS):
            qq = jnp.concatenate(
                [pa_ref[r0:r0 + WINDOW, PA_Q0 + (2 * kh + pr) * LANES: PA_Q0 + (2 * kh + pr + 1) * LANES]
                 for pr in range(2)], axis=0) * jnp.asarray(HEAD_DIM ** -0.5, bf16)
            for par in range(2):
                kk = kx_scr[2 * kh + par, r0:r0 + 2 * WINDOW, :]
                s_parts.append(lax.dot_general(qq, kk, nt_dims, preferred_element_type=f32))
        cw = D_MODEL // nqb
        y_rnn_parts.append(jnp.dot(yr_ref[...], wro_ref[:, j * cw:(j + 1) * cw], preferred_element_type=f32))
        s = jnp.concatenate(s_parts, axis=0) + bias_scr[...]
        if j == 0:
            s = s + jnp.where(first_tile & (col < WINDOW), NEG_INF, 0.0).astype(f32)
        m = jnp.max(s, axis=-1, keepdims=True)
        p = jnp.exp(s - m).astype(bf16)
        for kh in range(N_KV_HEADS):
            o_ext = []
            for par in range(2):
                t0 = (2 * kh + par) * 2 * WINDOW
                vv = jnp.concatenate([vx_scr[2 * kh + par, r0:r0 + 2 * WINDOW, :], ones_blk], axis=1)
                o_ext.append(jnp.dot(p[t0:t0 + 2 * WINDOW, :], vv, preferred_element_type=f32))
            for pr in range(2):
                pair = 2 * kh + pr
                rows = slice(pr * WINDOW, (pr + 1) * WINDOW)
                rden = []
                for par in range(2):
                    t0 = (2 * kh + par) * 2 * WINDOW + pr * WINDOW
                    sink_term = jnp.exp(sinks_ref[2 * pair + par] - m[t0:t0 + WINDOW, :])
                    rden.append(1.0 / (o_ext[par][rows, LANES:] + sink_term))
                o = (o_ext[0][rows, :LANES] + o_ext[1][rows, :LANES]) * jnp.where(lo_q, rden[0], rden[1])
                ga = pa_ref[r0:r0 + WINDOW, PA_GA0 + pair * LANES: PA_GA0 + (pair + 1) * LANES].astype(f32)
                attn_scr[r0:r0 + WINDOW, pair * LANES:(pair + 1) * LANES] = (
                    o * (ga * _sigmoid(ga))).astype(bf16)

    y_attn = jnp.dot(attn_scr[...], wao_ref[...], preferred_element_type=f32)
    y_rnn = jnp.concatenate(y_rnn_parts, axis=1)
    mixed = (_sigmoid(pm_ref[:, PM_MA0:PM_MA0 + D_MODEL].astype(f32)) * y_attn
             + _sigmoid(pm_ref[:, PM_MR0:PM_MR0 + D_MODEL].astype(f32)) * y_rnn).astype(bf16)
    zs = []
    row_sum = jnp.zeros((tm, 1), f32)
    for c in range(D_MODEL // COL_TILE):
        k0 = c * COL_TILE
        z = alpha * x_ref[:, k0:k0 + COL_TILE] + jnp.dot(mixed, wo_ref[:, k0:k0 + COL_TILE],
                                                         preferred_element_type=f32)
        row_sum = row_sum + jnp.sum(z, axis=-1, keepdims=True)
        zs.append(z)
    zc = jnp.concatenate(zs, axis=1) - row_sum * (1.0 / D_MODEL)
    var = jnp.mean(zc * zc, axis=-1, keepdims=True)
    out_ref[...] = zc * lax.rsqrt(var + LN_EPS) * lng_ref[...] + lnb_ref[...]


def _const_spec(shape):
    nd = len(shape)
    return pl.BlockSpec(shape, lambda b, s: (0,) * nd, pipeline_mode=pl.Buffered(1))


def _mixer(pa, pm, yr, x2d, bucket, relb, sinks, wao, wro, wo, lng, lnb, *, batch, seq, alpha):
    tm = MIX_TM
    nst = seq // tm
    row_map = lambda b, s: (b * nst + s, 0)
    smem = pl.BlockSpec(memory_space=pltpu.SMEM)
    in_specs = [
        smem, smem,
        pl.BlockSpec((tm, PA_WIDTH), row_map),
        pl.BlockSpec((tm, PM_WIDTH), row_map),
        pl.BlockSpec((tm, RNN_WIDTH), row_map),
        pl.BlockSpec((tm, D_MODEL), row_map),
        _const_spec(bucket.shape), _const_spec(wao.shape), _const_spec(wro.shape), _const_spec(wo.shape),
        _const_spec(lng.shape), _const_spec(lnb.shape),
    ]
    scratch = [
        pltpu.VMEM((N_Q_HEADS * WINDOW, 2 * WINDOW), jnp.float32),
        pltpu.VMEM((8, WINDOW + tm, LANES), jnp.bfloat16),
        pltpu.VMEM((8, WINDOW + tm, LANES), jnp.bfloat16),
        pltpu.VMEM((tm, ATTN_WIDTH), jnp.bfloat16),
    ]
    return pl.pallas_call(
        functools.partial(_mixer_kernel, alpha=alpha),
        grid=(batch, nst),
        in_specs=in_specs,
        out_specs=pl.BlockSpec((tm, D_MODEL), row_map),
        out_shape=jax.ShapeDtypeStruct((batch * seq, D_MODEL), jnp.float32),
        scratch_shapes=scratch,
        compiler_params=pltpu.CompilerParams(
            dimension_semantics=("arbitrary", "arbitrary"), vmem_limit_bytes=VMEM_LIMIT),
        name="mixer",
    )(relb, sinks, pa, pm, yr, x2d, bucket, wao, wro, wo, lng, lnb)


def kernel(x, w_in, conv_w, conv_b, w_r, b_r, w_i, b_i, lru_lambda, sinks, w_attn_out, w_rnn_out, w_out,
           ln_g, ln_b, rel_bias):
    batch, seq, d = x.shape
    depth = w_in.shape[0]
    alpha = (2.0 * depth) ** 0.25
    bucket = jnp.asarray(_t5_bucket_table())
    x2d = x.reshape(batch * seq, d)
    as_rows = lambda v: v.reshape(depth, 1, v.shape[-1])
    b_r, b_i, lam, conv_b = as_rows(b_r), as_rows(b_i), as_rows(lru_lambda), as_rows(conv_b)
    for l in range(depth):
        pa = _plain_projection(x2d, w_in, l, Q0, PA_WIDTH, PA_TILE, "attn_projection")
        pm = _plain_projection(x2d, w_in, l, MA0, PM_WIDTH, PM_TILE, "merge_gate_projection")
        yr, wao, wro, wo = _rnn_projection(x2d, w_in, l, w_r, w_i, b_r, b_i, lam, conv_w, conv_b,
                                           w_attn_out, w_rnn_out, w_out, seq=seq)
        x2d = _mixer(pa, pm, yr, x2d, bucket, rel_bias, sinks[l], wao, wro, wo,
                     ln_g[l][None, :], ln_b[l][None, :], batch=batch, seq=seq, alpha=alpha)
    return x2d.reshape(batch, seq, d)
```

```python
import functools
import math

import numpy as np
import jax
import jax.numpy as jnp
from jax import lax
from jax.experimental import pallas as pl
from jax.experimental.pallas import tpu as pltpu

D_MODEL = 2048
N_Q_HEADS = 16
N_KV_HEADS = 4
HEAD_DIM = 64
ATTN_WIDTH = N_Q_HEADS * HEAD_DIM
KV_WIDTH = N_KV_HEADS * HEAD_DIM
WINDOW = 128
N_BUCKETS = 32
MAX_DISTANCE = 128
RNN_WIDTH = D_MODEL
RNN_BLOCKS = 16
RNN_BLOCK_DIM = RNN_WIDTH // RNN_BLOCKS
CONV_WIDTH = 4
LRU_C = 8.0
LN_EPS = 1e-5
NEG_INF = -1e30

Q0 = 0
K0 = Q0 + ATTN_WIDTH
V0 = K0 + KV_WIDTH
GA0 = V0 + KV_WIDTH
XR0 = GA0 + ATTN_WIDTH
GR0 = XR0 + RNN_WIDTH
MA0 = GR0 + RNN_WIDTH
MR0 = MA0 + D_MODEL
IN_WIDTH = MR0 + D_MODEL

LANES = 128
SUBLANES = 8
VMEM_LIMIT = 60 * 1024 * 1024

COL_TILE = 512
PLAIN_TM = 2048
RNN_TM = 1024
MIX_TM = 256
CAST_ROWS = 64

PA_Q0 = 0
PA_K0 = PA_Q0 + ATTN_WIDTH
PA_V0 = PA_K0 + KV_WIDTH
PA_GA0 = PA_V0 + KV_WIDTH
PA_MA0 = PA_GA0 + ATTN_WIDTH
PA_MR0 = PA_MA0 + D_MODEL
PA_WIDTH = PA_MR0 + D_MODEL
N_PLAIN_TILES = PA_WIDTH // COL_TILE


def _t5_bucket_table():
    qi = np.arange(WINDOW)[:, None]
    kj = np.arange(2 * WINDOW)[None, :]
    dist = qi + WINDOW - kj
    max_exact = N_BUCKETS // 2
    n = np.maximum(dist, 0)
    nf = np.maximum(n, max_exact).astype(np.float32)
    large = max_exact + (np.log(nf / np.float32(max_exact)) / np.float32(math.log(MAX_DISTANCE / max_exact))
                         * np.float32(N_BUCKETS - max_exact)).astype(np.int32)
    large = np.minimum(large, N_BUCKETS - 1)
    bucket = np.where(n < max_exact, n, large)
    in_window = (dist >= 0) & (dist < WINDOW)
    return np.where(in_window, bucket, -1).astype(np.int32)


def _sigmoid(x):
    return 0.5 * jnp.tanh(0.5 * x) + 0.5


def _plain_proj_kernel(x_ref, w_ref, pa_ref):
    pa_ref[...] = jnp.dot(x_ref[...].astype(jnp.bfloat16), w_ref[...],
                          preferred_element_type=jnp.float32).astype(pa_ref.dtype)


def _plain_projection(x2d, w_plain):
    m, k = x2d.shape
    tm = PLAIN_TM
    return pl.pallas_call(
        _plain_proj_kernel,
        grid=(m // tm, N_PLAIN_TILES),
        in_specs=[pl.BlockSpec((tm, k), lambda i, n: (i, 0)),
                  pl.BlockSpec((k, COL_TILE), lambda i, n: (0, n))],
        out_specs=pl.BlockSpec((tm, COL_TILE), lambda i, n: (i, n)),
        out_shape=jax.ShapeDtypeStruct((m, PA_WIDTH), jnp.bfloat16),
        compiler_params=pltpu.CompilerParams(
            dimension_semantics=("arbitrary", "arbitrary"), vmem_limit_bytes=VMEM_LIMIT),
        name="plain_projection",
    )(x2d, w_plain)


def _linear_scan(a, u, carry):
    tm = a.shape[0]
    groups = tm // SUBLANES
    a3 = a.reshape(groups, SUBLANES, LANES)
    u3 = u.reshape(groups, SUBLANES, LANES)
    row = lax.broadcasted_iota(jnp.int32, (groups, SUBLANES, LANES), 1)
    for d in (1, 2, 4):
        keep = row >= d
        a_sh = pltpu.roll(a3, d, axis=1)
        u_sh = pltpu.roll(u3, d, axis=1)
        u3 = jnp.where(keep, a3 * u_sh + u3, u3)
        a3 = jnp.where(keep, a3 * a_sh, a3)
    hs = []
    for g in range(groups):
        hg = u3[g] + a3[g] * carry
        hs.append(hg)
        carry = hg[SUBLANES - 1:SUBLANES, :]
    return jnp.concatenate(hs, axis=0), carry


def _rnn_proj_kernel(x_ref, wx_ref, wg_ref, wr_ref, wi_ref, br_ref, bi_ref, lam_ref, convw_ref, convb_ref,
                     wao_ref, wro_ref, wo_ref, wpa_ref, wpm_ref,
                     yr_ref, wao_bf_ref, wro_bf_ref, wo_bf_ref, wp_bf_ref, hist_scr, hcar_scr, *, tiles_per_seq):
    tm = RNN_TM
    f32 = jnp.float32
    bf16 = jnp.bfloat16
    first = (pl.program_id(1) % tiles_per_seq) == 0

    hist_scr[0:SUBLANES, :] = jnp.where(first, 0.0, hist_scr[tm:tm + SUBLANES, :])
    carry_all = jnp.where(first, 0.0, hcar_scr[...])

    xb = x_ref[...].astype(bf16)
    hist_scr[SUBLANES:SUBLANES + tm, :] = jnp.dot(xb, wx_ref[...].astype(bf16), preferred_element_type=f32)
    wao_bf_ref[...] = wao_ref[...].astype(bf16)
    wro_bf_ref[...] = wro_ref[...].astype(bf16)
    wo_bf_ref[...] = wo_ref[...].astype(bf16)
    wp_bf_ref[...] = jnp.concatenate([wpa_ref[0], wpm_ref[0]], axis=1).astype(bf16)
    wg = wg_ref[...].astype(bf16)
    g_top = jnp.dot(xb[:tm // 2], wg, preferred_element_type=f32)

    neg_lam = -lam_ref[...]
    half_scale = (-0.5 * LRU_C) * (jnp.maximum(neg_lam, 0.0) + jnp.log1p(jnp.exp(-jnp.abs(neg_lam))))
    half_br = 0.5 * br_ref[...]
    half_bi = 0.5 * bi_ref[...]
    for b in range(COL_TILE // RNN_BLOCK_DIM):
        c0 = b * RNN_BLOCK_DIM
        c1 = c0 + RNN_BLOCK_DIM
        xc = convb_ref[:, c0:c1]
        for kk in range(CONV_WIDTH):
            off = SUBLANES - (CONV_WIDTH - 1) + kk
            xc = xc + convw_ref[kk:kk + 1, c0:c1] * hist_scr[off:off + tm, c0:c1]
        w_gates = (0.5 * jnp.concatenate([wr_ref[b], wi_ref[b]], axis=1)).astype(bf16)
        ri = jnp.dot(xc.astype(bf16), w_gates, preferred_element_type=f32)
        if b == 0:
            g = jnp.concatenate([g_top, jnp.dot(xb[tm // 2:], wg, preferred_element_type=f32)], axis=0)
        tanh_r = jnp.tanh(ri[:, :RNN_BLOCK_DIM] + half_br[:, c0:c1])
        tanh_i = jnp.tanh(ri[:, RNN_BLOCK_DIM:] + half_bi[:, c0:c1])
        log_a = tanh_r * half_scale[:, c0:c1] + half_scale[:, c0:c1]
        a = jnp.exp(log_a)
        one_m_a2 = jnp.tanh(log_a) * (-1.0 - a * a)
        mult = jnp.where(one_m_a2 > 0.0, one_m_a2 * lax.rsqrt(one_m_a2), 0.0)
        u = (mult * xc) * (0.5 * tanh_i + 0.5)
        h, carry = _linear_scan(a, u, carry_all[:, c0:c1])
        hcar_scr[:, c0:c1] = carry
        gb = g[:, c0:c1]
        yr_ref[:, c0:c1] = (h * (gb * _sigmoid(gb))).astype(yr_ref.dtype)


def _rnn_projection(x2d, w_in, layer, w_r, w_i, b_r, b_i, lam, convw, convb, wao, wro, wo, *, seq):
    m, k = x2d.shape
    tm = RNN_TM
    nblk = COL_TILE // RNN_BLOCK_DIM
    grid = (RNN_WIDTH // COL_TILE, m // tm)
    gate_w = pl.BlockSpec((None, nblk, RNN_BLOCK_DIM, RNN_BLOCK_DIM), lambda c, i: (layer, c, 0, 0))
    row = pl.BlockSpec((None, 1, COL_TILE), lambda c, i: (layer, 0, c))

    weights = (wao, wro, wo)
    cast_in, cast_out = [], []
    for w in weights:
        nb = w.shape[1] // CAST_ROWS
        assert nb <= grid[0] * grid[1]
        blk = lambda c, i, nb=nb: jnp.minimum(c * grid[1] + i, nb - 1)
        cast_in.append(pl.BlockSpec((None, CAST_ROWS, w.shape[2]), lambda c, i, blk=blk: (layer, blk(c, i), 0)))
        cast_out.append(pl.BlockSpec((CAST_ROWS, w.shape[2]), lambda c, i, blk=blk: (blk(c, i), 0)))
    w_rows = k // (grid[0] * grid[1])
    for col0, width in ((Q0, XR0 - Q0), (MA0, IN_WIDTH - MA0)):
        cast_in.append(pl.BlockSpec(
            (pl.Element(1), pl.Element(w_rows), pl.Element(width)),
            lambda c, i, col0=col0: (layer, pl.multiple_of((c * grid[1] + i) * w_rows, SUBLANES), col0)))
    cast_out.append(pl.BlockSpec((w_rows, PA_WIDTH), lambda c, i: (c * grid[1] + i, 0)))

    return pl.pallas_call(
        functools.partial(_rnn_proj_kernel, tiles_per_seq=seq // tm),
        grid=grid,
        in_specs=[pl.BlockSpec((tm, k), lambda c, i: (i, 0)),
                  pl.BlockSpec((None, k, COL_TILE), lambda c, i: (layer, 0, XR0 // COL_TILE + c)),
                  pl.BlockSpec((None, k, COL_TILE), lambda c, i: (layer, 0, GR0 // COL_TILE + c)),
                  gate_w, gate_w, row, row, row,
                  pl.BlockSpec((None, CONV_WIDTH, COL_TILE), lambda c, i: (layer, 0, c)),
                  row] + cast_in,
        out_specs=[pl.BlockSpec((tm, COL_TILE), lambda c, i: (i, c))] + cast_out,
        out_shape=[jax.ShapeDtypeStruct((m, RNN_WIDTH), jnp.bfloat16)]
                  + [jax.ShapeDtypeStruct(w.shape[1:], jnp.bfloat16) for w in weights]
                  + [jax.ShapeDtypeStruct((k, PA_WIDTH), jnp.bfloat16)],
        scratch_shapes=[pltpu.VMEM((SUBLANES + tm, COL_TILE), jnp.float32),
                        pltpu.VMEM((1, COL_TILE), jnp.float32)],
        compiler_params=pltpu.CompilerParams(
            dimension_semantics=("arbitrary", "arbitrary"), vmem_limit_bytes=VMEM_LIMIT),
        name="rnn_projection",
    )(x2d, w_in, w_in, w_r, w_i, b_r, b_i, lam, convw, convb, wao, wro, wo, w_in, w_in)


def _mixer_kernel(relb_ref, sinks_ref,
                  pa_ref, yr_ref, x_ref, bucket_ref, wao_ref, wro_ref, wo_ref, lng_ref, lnb_ref,
                  out_ref,
                  bias_scr, kx_scr, vx_scr, attn_scr,
                  *, alpha):
    tm = MIX_TM
    nqb = tm // WINDOW
    first_tile = pl.program_id(1) == 0
    f32 = jnp.float32
    bf16 = jnp.bfloat16

    @pl.when((pl.program_id(0) == 0) & first_tile)
    def _():
        bucket = bucket_ref[...]
        for t in range(N_Q_HEADS):
            h = 4 * (t // 4) + 2 * (t % 2) + (t // 2) % 2
            acc = jnp.full((WINDOW, 2 * WINDOW), NEG_INF, f32)
            for b in range(N_BUCKETS):
                acc = jnp.where(bucket == b, relb_ref[b, h], acc)
            bias_scr[t * WINDOW:(t + 1) * WINDOW, :] = acc

    for scr in (kx_scr, vx_scr):
        prev = scr[:, tm:tm + WINDOW, :]
        scr[:, 0:WINDOW, :] = jnp.where(first_tile, jnp.zeros_like(prev), prev)

    u32 = jnp.uint32
    lo_u = lax.broadcasted_iota(jnp.int32, (tm // 2, LANES), 1) < HEAD_DIM
    zero_u = jnp.zeros((tm // 2, LANES), u32)
    for p in range(N_KV_HEADS // 2):
        for src0, scr in ((PA_K0, kx_scr), (PA_V0, vx_scr)):
            t = pltpu.bitcast(pa_ref[:, src0 + p * LANES: src0 + (p + 1) * LANES], u32)
            tr = pltpu.roll(t, HEAD_DIM, axis=1)
            variants = (jnp.where(lo_u, t, zero_u),
                        jnp.where(lo_u, zero_u, tr),
                        jnp.where(lo_u, tr, zero_u),
                        jnp.where(lo_u, zero_u, t))
            for v, val in enumerate(variants):
                scr[4 * p + v, WINDOW:WINDOW + tm, :] = pltpu.bitcast(val, bf16)

    lo_q = lax.broadcasted_iota(jnp.int32, (WINDOW, LANES), 1) < HEAD_DIM
    col = lax.broadcasted_iota(jnp.int32, (1, 2 * WINDOW), 1)
    ones_blk = jnp.ones((2 * WINDOW, LANES), bf16)
    nt_dims = (((1,), (1,)), ((), ()))
    y_rnn_parts = []
    for j in range(nqb):
        r0 = j * WINDOW
        s_parts = []
        for kh in range(N_KV_HEADS):
            qq = jnp.concatenate(
                [pa_ref[r0:r0 + WINDOW, PA_Q0 + (2 * kh + pr) * LANES: PA_Q0 + (2 * kh + pr + 1) * LANES]
                 for pr in range(2)], axis=0) * jnp.asarray(HEAD_DIM ** -0.5, bf16)
            for par in range(2):
                kk = kx_scr[2 * kh + par, r0:r0 + 2 * WINDOW, :]
                s_parts.append(lax.dot_general(qq, kk, nt_dims, preferred_element_type=f32))
        cw = D_MODEL // nqb
        y_rnn_parts.append(jnp.dot(yr_ref[...], wro_ref[:, j * cw:(j + 1) * cw], preferred_element_type=f32))
        s = jnp.concatenate(s_parts, axis=0) + bias_scr[...]
        if j == 0:
            s = s + jnp.where(first_tile & (col < WINDOW), NEG_INF, 0.0).astype(f32)
        m = jnp.max(s, axis=-1, keepdims=True)
        p = jnp.exp(s - m).astype(bf16)
        for kh in range(N_KV_HEADS):
            o_ext = []
            for par in range(2):
                t0 = (2 * kh + par) * 2 * WINDOW
                vv = jnp.concatenate([vx_scr[2 * kh + par, r0:r0 + 2 * WINDOW, :], ones_blk], axis=1)
                o_ext.append(jnp.dot(p[t0:t0 + 2 * WINDOW, :], vv, preferred_element_type=f32))
            for pr in range(2):
                pair = 2 * kh + pr
                rows = slice(pr * WINDOW, (pr + 1) * WINDOW)
                rden = []
                for par in range(2):
                    t0 = (2 * kh + par) * 2 * WINDOW + pr * WINDOW
                    sink_term = jnp.exp(sinks_ref[2 * pair + par] - m[t0:t0 + WINDOW, :])
                    rden.append(1.0 / (o_ext[par][rows, LANES:] + sink_term))
                o = (o_ext[0][rows, :LANES] + o_ext[1][rows, :LANES]) * jnp.where(lo_q, rden[0], rden[1])
                ga = pa_ref[r0:r0 + WINDOW, PA_GA0 + pair * LANES: PA_GA0 + (pair + 1) * LANES].astype(f32)
                attn_scr[r0:r0 + WINDOW, pair * LANES:(pair + 1) * LANES] = (
                    o * (ga * _sigmoid(ga))).astype(bf16)

    y_attn = jnp.dot(attn_scr[...], wao_ref[...], preferred_element_type=f32)
    y_rnn = jnp.concatenate(y_rnn_parts, axis=1)
    mixed = (_sigmoid(pa_ref[:, PA_MA0:PA_MA0 + D_MODEL].astype(f32)) * y_attn
             + _sigmoid(pa_ref[:, PA_MR0:PA_MR0 + D_MODEL].astype(f32)) * y_rnn).astype(bf16)
    zs = []
    row_sum = jnp.zeros((tm, 1), f32)
    for c in range(D_MODEL // COL_TILE):
        k0 = c * COL_TILE
        z = alpha * x_ref[:, k0:k0 + COL_TILE] + jnp.dot(mixed, wo_ref[:, k0:k0 + COL_TILE],
                                                         preferred_element_type=f32)
        row_sum = row_sum + jnp.sum(z, axis=-1, keepdims=True)
        zs.append(z)
    zc = jnp.concatenate(zs, axis=1) - row_sum * (1.0 / D_MODEL)
    var = jnp.mean(zc * zc, axis=-1, keepdims=True)
    out_ref[...] = zc * lax.rsqrt(var + LN_EPS) * lng_ref[...] + lnb_ref[...]


def _const_spec(shape):
    nd = len(shape)
    return pl.BlockSpec(shape, lambda b, s: (0,) * nd, pipeline_mode=pl.Buffered(1))


def _mixer(pa, yr, x2d, bucket, relb, sinks, wao, wro, wo, lng, lnb, *, batch, seq, alpha):
    tm = MIX_TM
    nst = seq // tm
    row_map = lambda b, s: (b * nst + s, 0)
    smem = pl.BlockSpec(memory_space=pltpu.SMEM)
    in_specs = [
        smem, smem,
        pl.BlockSpec((tm, PA_WIDTH), row_map),
        pl.BlockSpec((tm, RNN_WIDTH), row_map),
        pl.BlockSpec((tm, D_MODEL), row_map),
        _const_spec(bucket.shape), _const_spec(wao.shape), _const_spec(wro.shape), _const_spec(wo.shape),
        _const_spec(lng.shape), _const_spec(lnb.shape),
    ]
    scratch = [
        pltpu.VMEM((N_Q_HEADS * WINDOW, 2 * WINDOW), jnp.float32),
        pltpu.VMEM((8, WINDOW + tm, LANES), jnp.bfloat16),
        pltpu.VMEM((8, WINDOW + tm, LANES), jnp.bfloat16),
        pltpu.VMEM((tm, ATTN_WIDTH), jnp.bfloat16),
    ]
    return pl.pallas_call(
        functools.partial(_mixer_kernel, alpha=alpha),
        grid=(batch, nst),
        in_specs=in_specs,
        out_specs=pl.BlockSpec((tm, D_MODEL), row_map),
        out_shape=jax.ShapeDtypeStruct((batch * seq, D_MODEL), jnp.float32),
        scratch_shapes=scratch,
        compiler_params=pltpu.CompilerParams(
            dimension_semantics=("arbitrary", "arbitrary"), vmem_limit_bytes=VMEM_LIMIT),
        name="mixer",
    )(relb, sinks, pa, yr, x2d, bucket, wao, wro, wo, lng, lnb)


def kernel(x, w_in, conv_w, conv_b, w_r, b_r, w_i, b_i, lru_lambda, sinks, w_attn_out, w_rnn_out, w_out,
           ln_g, ln_b, rel_bias):
    batch, seq, d = x.shape
    depth = w_in.shape[0]
    alpha = (2.0 * depth) ** 0.25
    bucket = jnp.asarray(_t5_bucket_table())
    x2d = x.reshape(batch * seq, d)
    as_rows = lambda v: v.reshape(depth, 1, v.shape[-1])
    b_r, b_i, lam, conv_b = as_rows(b_r), as_rows(b_i), as_rows(lru_lambda), as_rows(conv_b)
    for l in range(depth):
        yr, wao, wro, wo, w_plain = _rnn_projection(x2d, w_in, l, w_r, w_i, b_r, b_i, lam, conv_w, conv_b,
                                                    w_attn_out, w_rnn_out, w_out, seq=seq)
        pa = _plain_projection(x2d, w_plain)
        x2d = _mixer(pa, yr, x2d, bucket, rel_bias, sinks[l], wao, wro, wo,
                     ln_g[l][None, :], ln_b[l][None, :], batch=batch, seq=seq, alpha=alpha)
    return x2d.reshape(batch, seq, d)
```

```python
import functools
import math

import numpy as np
import jax
import jax.numpy as jnp
from jax import lax
from jax.experimental import pallas as pl
from jax.experimental.pallas import tpu as pltpu

D_MODEL = 2048
N_Q_HEADS = 16
N_KV_HEADS = 4
HEAD_DIM = 64
ATTN_WIDTH = N_Q_HEADS * HEAD_DIM
KV_WIDTH = N_KV_HEADS * HEAD_DIM
WINDOW = 128
N_BUCKETS = 32
MAX_DISTANCE = 128
RNN_WIDTH = D_MODEL
RNN_BLOCKS = 16
RNN_BLOCK_DIM = RNN_WIDTH // RNN_BLOCKS
CONV_WIDTH = 4
LRU_C = 8.0
LN_EPS = 1e-5
NEG_INF = -1e30

Q0 = 0
K0 = Q0 + ATTN_WIDTH
V0 = K0 + KV_WIDTH
GA0 = V0 + KV_WIDTH
XR0 = GA0 + ATTN_WIDTH
GR0 = XR0 + RNN_WIDTH
MA0 = GR0 + RNN_WIDTH
MR0 = MA0 + D_MODEL
IN_WIDTH = MR0 + D_MODEL

LANES = 128
SUBLANES = 8
VMEM_LIMIT = 60 * 1024 * 1024

COL_TILE = 512
PLAIN_TM = 2048
RNN_TM = 1024
MIX_TM = 256
CAST_ROWS = 64

PA_Q0 = 0
PA_K0 = PA_Q0 + ATTN_WIDTH
PA_V0 = PA_K0 + KV_WIDTH
PA_GA0 = PA_V0 + KV_WIDTH
PA_MA0 = PA_GA0 + ATTN_WIDTH
PA_MR0 = PA_MA0 + D_MODEL
PA_WIDTH = PA_MR0 + D_MODEL
N_PLAIN_TILES = PA_WIDTH // COL_TILE


def _t5_bucket_table():
    qi = np.arange(WINDOW)[:, None]
    kj = np.arange(2 * WINDOW)[None, :]
    dist = qi + WINDOW - kj
    max_exact = N_BUCKETS // 2
    n = np.maximum(dist, 0)
    nf = np.maximum(n, max_exact).astype(np.float32)
    large = max_exact + (np.log(nf / np.float32(max_exact)) / np.float32(math.log(MAX_DISTANCE / max_exact))
                         * np.float32(N_BUCKETS - max_exact)).astype(np.int32)
    large = np.minimum(large, N_BUCKETS - 1)
    bucket = np.where(n < max_exact, n, large)
    in_window = (dist >= 0) & (dist < WINDOW)
    return np.where(in_window, bucket, -1).astype(np.int32)


def _sigmoid(x):
    return 0.5 * jnp.tanh(0.5 * x) + 0.5


def _plain_proj_kernel(x_ref, w_ref, pa_ref, xraw_scr):
    @pl.when(pl.program_id(1) == 0)
    def _():
        xraw_scr[...] = pltpu.bitcast(x_ref[...].astype(jnp.bfloat16), jnp.uint32)

    pa_ref[...] = jnp.dot(pltpu.bitcast(xraw_scr[...], jnp.bfloat16), w_ref[...],
                          preferred_element_type=jnp.float32).astype(pa_ref.dtype)


def _plain_projection(x2d, w_plain):
    m, k = x2d.shape
    tm = PLAIN_TM
    return pl.pallas_call(
        _plain_proj_kernel,
        grid=(m // tm, N_PLAIN_TILES),
        in_specs=[pl.BlockSpec((tm, k), lambda i, n: (i, 0)),
                  pl.BlockSpec((k, COL_TILE), lambda i, n: (0, n))],
        out_specs=pl.BlockSpec((tm, COL_TILE), lambda i, n: (i, n)),
        out_shape=jax.ShapeDtypeStruct((m, PA_WIDTH), jnp.bfloat16),
        scratch_shapes=[pltpu.VMEM((tm // 2, k), jnp.uint32)],
        compiler_params=pltpu.CompilerParams(
            dimension_semantics=("arbitrary", "arbitrary"), vmem_limit_bytes=VMEM_LIMIT),
        name="plain_projection",
    )(x2d, w_plain)


def _linear_scan(a, u, carry):
    tm = a.shape[0]
    groups = tm // SUBLANES
    a3 = a.reshape(groups, SUBLANES, LANES)
    u3 = u.reshape(groups, SUBLANES, LANES)
    row = lax.broadcasted_iota(jnp.int32, (groups, SUBLANES, LANES), 1)
    for d in (1, 2, 4):
        keep = row >= d
        a_sh = pltpu.roll(a3, d, axis=1)
        u_sh = pltpu.roll(u3, d, axis=1)
        u3 = jnp.where(keep, a3 * u_sh + u3, u3)
        a3 = jnp.where(keep, a3 * a_sh, a3)
    hs = []
    for g in range(groups):
        hg = u3[g] + a3[g] * carry
        hs.append(hg)
        carry = hg[SUBLANES - 1:SUBLANES, :]
    return jnp.concatenate(hs, axis=0), carry


def _rnn_proj_kernel(x_ref, wx_ref, wg_ref, wr_ref, wi_ref, br_ref, bi_ref, lam_ref, convw_ref, convb_ref,
                     wao_ref, wro_ref, wo_ref, wpa_ref, wpm_ref,
                     yr_ref, wao_bf_ref, wro_bf_ref, wo_bf_ref, wp_bf_ref, hist_scr, hcar_scr, *, tiles_per_seq):
    tm = RNN_TM
    f32 = jnp.float32
    bf16 = jnp.bfloat16
    first = (pl.program_id(1) % tiles_per_seq) == 0

    hist_scr[0:SUBLANES, :] = jnp.where(first, 0.0, hist_scr[tm:tm + SUBLANES, :])
    carry_all = jnp.where(first, 0.0, hcar_scr[...])

    xb = x_ref[...].astype(bf16)
    hist_scr[SUBLANES:SUBLANES + tm, :] = jnp.dot(xb, wx_ref[...].astype(bf16), preferred_element_type=f32)
    wao_bf_ref[...] = wao_ref[...].astype(bf16)
    wro_bf_ref[...] = wro_ref[...].astype(bf16)
    wo_bf_ref[...] = wo_ref[...].astype(bf16)
    wp_bf_ref[...] = jnp.concatenate([wpa_ref[0], wpm_ref[0]], axis=1).astype(bf16)
    wg = wg_ref[...].astype(bf16)
    g_top = jnp.dot(xb[:tm // 2], wg, preferred_element_type=f32)

    neg_lam = -lam_ref[...]
    half_scale = (-0.5 * LRU_C) * (jnp.maximum(neg_lam, 0.0) + jnp.log1p(jnp.exp(-jnp.abs(neg_lam))))
    half_br = 0.5 * br_ref[...]
    half_bi = 0.5 * bi_ref[...]
    for b in range(COL_TILE // RNN_BLOCK_DIM):
        c0 = b * RNN_BLOCK_DIM
        c1 = c0 + RNN_BLOCK_DIM
        xc = convb_ref[:, c0:c1]
        for kk in range(CONV_WIDTH):
            off = SUBLANES - (CONV_WIDTH - 1) + kk
            xc = xc + convw_ref[kk:kk + 1, c0:c1] * hist_scr[off:off + tm, c0:c1]
        w_gates = (0.5 * jnp.concatenate([wr_ref[b], wi_ref[b]], axis=1)).astype(bf16)
        ri = jnp.dot(xc.astype(bf16), w_gates, preferred_element_type=f32)
        if b == 0:
            g = jnp.concatenate([g_top, jnp.dot(xb[tm // 2:], wg, preferred_element_type=f32)], axis=0)
        tanh_r = jnp.tanh(ri[:, :RNN_BLOCK_DIM] + half_br[:, c0:c1])
        tanh_i = jnp.tanh(ri[:, RNN_BLOCK_DIM:] + half_bi[:, c0:c1])
        log_a = tanh_r * half_scale[:, c0:c1] + half_scale[:, c0:c1]
        a = jnp.exp(log_a)
        one_m_a2 = jnp.tanh(log_a) * (-1.0 - a * a)
        mult = jnp.where(one_m_a2 > 0.0, one_m_a2 * lax.rsqrt(one_m_a2), 0.0)
        u = (mult * xc) * (0.5 * tanh_i + 0.5)
        h, carry = _linear_scan(a, u, carry_all[:, c0:c1])
        hcar_scr[:, c0:c1] = carry
        gb = g[:, c0:c1]
        yr_ref[:, c0:c1] = (h * (gb * _sigmoid(gb))).astype(yr_ref.dtype)


def _rnn_projection(x2d, w_in, layer, w_r, w_i, b_r, b_i, lam, convw, convb, wao, wro, wo, *, seq):
    m, k = x2d.shape
    tm = RNN_TM
    nblk = COL_TILE // RNN_BLOCK_DIM
    grid = (RNN_WIDTH // COL_TILE, m // tm)
    gate_w = pl.BlockSpec((None, nblk, RNN_BLOCK_DIM, RNN_BLOCK_DIM), lambda c, i: (layer, c, 0, 0))
    row = pl.BlockSpec((None, 1, COL_TILE), lambda c, i: (layer, 0, c))

    weights = (wao, wro, wo)
    cast_in, cast_out = [], []
    for w in weights:
        nb = w.shape[1] // CAST_ROWS
        assert nb <= grid[0] * grid[1]
        blk = lambda c, i, nb=nb: jnp.minimum(c * grid[1] + i, nb - 1)
        cast_in.append(pl.BlockSpec((None, CAST_ROWS, w.shape[2]), lambda c, i, blk=blk: (layer, blk(c, i), 0)))
        cast_out.append(pl.BlockSpec((CAST_ROWS, w.shape[2]), lambda c, i, blk=blk: (blk(c, i), 0)))
    w_rows = k // (grid[0] * grid[1])
    for col0, width in ((Q0, XR0 - Q0), (MA0, IN_WIDTH - MA0)):
        cast_in.append(pl.BlockSpec(
            (pl.Element(1), pl.Element(w_rows), pl.Element(width)),
            lambda c, i, col0=col0: (layer, pl.multiple_of((c * grid[1] + i) * w_rows, SUBLANES), col0)))
    cast_out.append(pl.BlockSpec((w_rows, PA_WIDTH), lambda c, i: (c * grid[1] + i, 0)))

    return pl.pallas_call(
        functools.partial(_rnn_proj_kernel, tiles_per_seq=seq // tm),
        grid=grid,
        in_specs=[pl.BlockSpec((tm, k), lambda c, i: (i, 0)),
                  pl.BlockSpec((None, k, COL_TILE), lambda c, i: (layer, 0, XR0 // COL_TILE + c)),
                  pl.BlockSpec((None, k, COL_TILE), lambda c, i: (layer, 0, GR0 // COL_TILE + c)),
                  gate_w, gate_w, row, row, row,
                  pl.BlockSpec((None, CONV_WIDTH, COL_TILE), lambda c, i: (layer, 0, c)),
                  row] + cast_in,
        out_specs=[pl.BlockSpec((tm, COL_TILE), lambda c, i: (i, c))] + cast_out,
        out_shape=[jax.ShapeDtypeStruct((m, RNN_WIDTH), jnp.bfloat16)]
                  + [jax.ShapeDtypeStruct(w.shape[1:], jnp.bfloat16) for w in weights]
                  + [jax.ShapeDtypeStruct((k, PA_WIDTH), jnp.bfloat16)],
        scratch_shapes=[pltpu.VMEM((SUBLANES + tm, COL_TILE), jnp.float32),
                        pltpu.VMEM((1, COL_TILE), jnp.float32)],
        compiler_params=pltpu.CompilerParams(
            dimension_semantics=("arbitrary", "arbitrary"), vmem_limit_bytes=VMEM_LIMIT),
        name="rnn_projection",
    )(x2d, w_in, w_in, w_r, w_i, b_r, b_i, lam, convw, convb, wao, wro, wo, w_in, w_in)


def _mixer_kernel(relb_ref, sinks_ref,
                  pa_ref, yr_ref, x_ref, bucket_ref, wao_ref, wro_ref, wo_ref, lng_ref, lnb_ref,
                  out_ref,
                  bias_scr, kx_scr, vx_scr, attn_scr,
                  *, alpha):
    tm = MIX_TM
    nqb = tm // WINDOW
    first_tile = pl.program_id(1) == 0
    f32 = jnp.float32
    bf16 = jnp.bfloat16

    @pl.when((pl.program_id(0) == 0) & first_tile)
    def _():
        bucket = bucket_ref[...]
        for t in range(N_Q_HEADS):
            h = 4 * (t // 4) + 2 * (t % 2) + (t // 2) % 2
            acc = jnp.full((WINDOW, 2 * WINDOW), NEG_INF, f32)
            for b in range(N_BUCKETS):
                acc = jnp.where(bucket == b, relb_ref[b, h], acc)
            bias_scr[t * WINDOW:(t + 1) * WINDOW, :] = acc

    for scr in (kx_scr, vx_scr):
        prev = scr[:, tm:tm + WINDOW, :]
        scr[:, 0:WINDOW, :] = jnp.where(first_tile, jnp.zeros_like(prev), prev)

    u32 = jnp.uint32
    lo_u = lax.broadcasted_iota(jnp.int32, (tm // 2, LANES), 1) < HEAD_DIM
    zero_u = jnp.zeros((tm // 2, LANES), u32)
    for p in range(N_KV_HEADS // 2):
        for src0, scr in ((PA_K0, kx_scr), (PA_V0, vx_scr)):
            t = pltpu.bitcast(pa_ref[:, src0 + p * LANES: src0 + (p + 1) * LANES], u32)
            tr = pltpu.roll(t, HEAD_DIM, axis=1)
            variants = (jnp.where(lo_u, t, zero_u),
                        jnp.where(lo_u, zero_u, tr),
                        jnp.where(lo_u, tr, zero_u),
                        jnp.where(lo_u, zero_u, t))
            for v, val in enumerate(variants):
                scr[4 * p + v, WINDOW:WINDOW + tm, :] = pltpu.bitcast(val, bf16)

    lo_q = lax.broadcasted_iota(jnp.int32, (WINDOW, LANES), 1) < HEAD_DIM
    col = lax.broadcasted_iota(jnp.int32, (1, 2 * WINDOW), 1)
    ones_blk = jnp.ones((2 * WINDOW, LANES), bf16)
    nt_dims = (((1,), (1,)), ((), ()))
    y_rnn_parts = []
    for j in range(nqb):
        r0 = j * WINDOW
        s_parts = []
        for kh in range(N_KV_HEADS):
            qq = jnp.concatenate(
                [pa_ref[r0:r0 + WINDOW, PA_Q0 + (2 * kh + pr) * LANES: PA_Q0 + (2 * kh + pr + 1) * LANES]
                 for pr in range(2)], axis=0) * jnp.asarray(HEAD_DIM ** -0.5, bf16)
            for par in range(2):
                kk = kx_scr[2 * kh + par, r0:r0 + 2 * WINDOW, :]
                s_parts.append(lax.dot_general(qq, kk, nt_dims, preferred_element_type=f32))
        cw = D_MODEL // nqb
        y_rnn_parts.append(jnp.dot(yr_ref[...], wro_ref[:, j * cw:(j + 1) * cw], preferred_element_type=f32))
        s = jnp.concatenate(s_parts, axis=0) + bias_scr[...]
        if j == 0:
            s = s + jnp.where(first_tile & (col < WINDOW), NEG_INF, 0.0).astype(f32)
        m = jnp.max(s, axis=-1, keepdims=True)
        p = jnp.exp(s - m).astype(bf16)
        for kh in range(N_KV_HEADS):
            o_ext = []
            for par in range(2):
                t0 = (2 * kh + par) * 2 * WINDOW
                vv = jnp.concatenate([vx_scr[2 * kh + par, r0:r0 + 2 * WINDOW, :], ones_blk], axis=1)
                o_ext.append(jnp.dot(p[t0:t0 + 2 * WINDOW, :], vv, preferred_element_type=f32))
            for pr in range(2):
                pair = 2 * kh + pr
                rows = slice(pr * WINDOW, (pr + 1) * WINDOW)
                rden = []
                for par in range(2):
                    t0 = (2 * kh + par) * 2 * WINDOW + pr * WINDOW
                    sink_term = jnp.exp(sinks_ref[2 * pair + par] - m[t0:t0 + WINDOW, :])
                    rden.append(1.0 / (o_ext[par][rows, LANES:] + sink_term))
                o = (o_ext[0][rows, :LANES] + o_ext[1][rows, :LANES]) * jnp.where(lo_q, rden[0], rden[1])
                ga = pa_ref[r0:r0 + WINDOW, PA_GA0 + pair * LANES: PA_GA0 + (pair + 1) * LANES].astype(f32)
                attn_scr[r0:r0 + WINDOW, pair * LANES:(pair + 1) * LANES] = (
                    o * (ga * _sigmoid(ga))).astype(bf16)

    y_attn = jnp.dot(attn_scr[...], wao_ref[...], preferred_element_type=f32)
    y_rnn = jnp.concatenate(y_rnn_parts, axis=1)
    mixed = (_sigmoid(pa_ref[:, PA_MA0:PA_MA0 + D_MODEL].astype(f32)) * y_attn
             + _sigmoid(pa_ref[:, PA_MR0:PA_MR0 + D_MODEL].astype(f32)) * y_rnn).astype(bf16)
    zs = []
    row_sum = jnp.zeros((tm, 1), f32)
    for c in range(D_MODEL // COL_TILE):
        k0 = c * COL_TILE
        z = alpha * x_ref[:, k0:k0 + COL_TILE] + jnp.dot(mixed, wo_ref[:, k0:k0 + COL_TILE],
                                                         preferred_element_type=f32)
        row_sum = row_sum + jnp.sum(z, axis=-1, keepdims=True)
        zs.append(z)
    zc = jnp.concatenate(zs, axis=1) - row_sum * (1.0 / D_MODEL)
    var = jnp.mean(zc * zc, axis=-1, keepdims=True)
    out_ref[...] = zc * lax.rsqrt(var + LN_EPS) * lng_ref[...] + lnb_ref[...]


def _const_spec(shape):
    nd = len(shape)
    return pl.BlockSpec(shape, lambda b, s: (0,) * nd, pipeline_mode=pl.Buffered(1))


def _mixer(pa, yr, x2d, bucket, relb, sinks, wao, wro, wo, lng, lnb, *, batch, seq, alpha):
    tm = MIX_TM
    nst = seq // tm
    row_map = lambda b, s: (b * nst + s, 0)
    smem = pl.BlockSpec(memory_space=pltpu.SMEM)
    in_specs = [
        smem, smem,
        pl.BlockSpec((tm, PA_WIDTH), row_map),
        pl.BlockSpec((tm, RNN_WIDTH), row_map),
        pl.BlockSpec((tm, D_MODEL), row_map),
        _const_spec(bucket.shape), _const_spec(wao.shape), _const_spec(wro.shape), _const_spec(wo.shape),
        _const_spec(lng.shape), _const_spec(lnb.shape),
    ]
    scratch = [
        pltpu.VMEM((N_Q_HEADS * WINDOW, 2 * WINDOW), jnp.float32),
        pltpu.VMEM((8, WINDOW + tm, LANES), jnp.bfloat16),
        pltpu.VMEM((8, WINDOW + tm, LANES), jnp.bfloat16),
        pltpu.VMEM((tm, ATTN_WIDTH), jnp.bfloat16),
    ]
    return pl.pallas_call(
        functools.partial(_mixer_kernel, alpha=alpha),
        grid=(batch, nst),
        in_specs=in_specs,
        out_specs=pl.BlockSpec((tm, D_MODEL), row_map),
        out_shape=jax.ShapeDtypeStruct((batch * seq, D_MODEL), jnp.float32),
        scratch_shapes=scratch,
        compiler_params=pltpu.CompilerParams(
            dimension_semantics=("arbitrary", "arbitrary"), vmem_limit_bytes=VMEM_LIMIT),
        name="mixer",
    )(relb, sinks, pa, yr, x2d, bucket, wao, wro, wo, lng, lnb)


def kernel(x, w_in, conv_w, conv_b, w_r, b_r, w_i, b_i, lru_lambda, sinks, w_attn_out, w_rnn_out, w_out,
           ln_g, ln_b, rel_bias):
    batch, seq, d = x.shape
    depth = w_in.shape[0]
    alpha = (2.0 * depth) ** 0.25
    bucket = jnp.asarray(_t5_bucket_table())
    x2d = x.reshape(batch * seq, d)
    as_rows = lambda v: v.reshape(depth, 1, v.shape[-1])
    b_r, b_i, lam, conv_b = as_rows(b_r), as_rows(b_i), as_rows(lru_lambda), as_rows(conv_b)
    for l in range(depth):
        yr, wao, wro, wo, w_plain = _rnn_projection(x2d, w_in, l, w_r, w_i, b_r, b_i, lam, conv_w, conv_b,
                                                    w_attn_out, w_rnn_out, w_out, seq=seq)
        pa = _plain_projection(x2d, w_plain)
        x2d = _mixer(pa, yr, x2d, bucket, rel_bias, sinks[l], wao, wro, wo,
                     ln_g[l][None, :], ln_b[l][None, :], batch=batch, seq=seq, alpha=alpha)
    return x2d.reshape(batch, seq, d)
```

```python
import functools
import math

import numpy as np
import jax
import jax.numpy as jnp
from jax import lax
from jax.experimental import pallas as pl
from jax.experimental.pallas import tpu as pltpu

D_MODEL = 2048
N_Q_HEADS = 16
N_KV_HEADS = 4
HEAD_DIM = 64
ATTN_WIDTH = N_Q_HEADS * HEAD_DIM
KV_WIDTH = N_KV_HEADS * HEAD_DIM
WINDOW = 128
N_BUCKETS = 32
MAX_DISTANCE = 128
RNN_WIDTH = D_MODEL
RNN_BLOCKS = 16
RNN_BLOCK_DIM = RNN_WIDTH // RNN_BLOCKS
CONV_WIDTH = 4
LRU_C = 8.0
LN_EPS = 1e-5
NEG_INF = -1e30

Q0 = 0
K0 = Q0 + ATTN_WIDTH
V0 = K0 + KV_WIDTH
GA0 = V0 + KV_WIDTH
XR0 = GA0 + ATTN_WIDTH
GR0 = XR0 + RNN_WIDTH
MA0 = GR0 + RNN_WIDTH
MR0 = MA0 + D_MODEL
IN_WIDTH = MR0 + D_MODEL

LANES = 128
SUBLANES = 8
VMEM_LIMIT = 60 * 1024 * 1024

COL_TILE = 512
PLAIN_TM = 2048
RNN_TM = 1024
MIX_TM = 256
CAST_ROWS = 64

PA_Q0 = 0
PA_K0 = PA_Q0 + ATTN_WIDTH
PA_V0 = PA_K0 + KV_WIDTH
PA_GA0 = PA_V0 + KV_WIDTH
PA_MA0 = PA_GA0 + ATTN_WIDTH
PA_MR0 = PA_MA0 + D_MODEL
PA_WIDTH = PA_MR0 + D_MODEL
N_PLAIN_TILES = PA_WIDTH // COL_TILE


def _t5_bucket_table():
    qi = np.arange(WINDOW)[:, None]
    kj = np.arange(2 * WINDOW)[None, :]
    dist = qi + WINDOW - kj
    max_exact = N_BUCKETS // 2
    n = np.maximum(dist, 0)
    nf = np.maximum(n, max_exact).astype(np.float32)
    large = max_exact + (np.log(nf / np.float32(max_exact)) / np.float32(math.log(MAX_DISTANCE / max_exact))
                         * np.float32(N_BUCKETS - max_exact)).astype(np.int32)
    large = np.minimum(large, N_BUCKETS - 1)
    bucket = np.where(n < max_exact, n, large)
    in_window = (dist >= 0) & (dist < WINDOW)
    return np.where(in_window, bucket, -1).astype(np.int32)


def _sigmoid(x):
    return 0.5 * jnp.tanh(0.5 * x) + 0.5


def _plain_proj_kernel(x_ref, w_ref, pa_ref):
    pa_ref[...] = jnp.dot(x_ref[...].astype(jnp.bfloat16), w_ref[...],
                          preferred_element_type=jnp.float32).astype(pa_ref.dtype)


def _plain_projection(x2d, w_plain):
    m, k = x2d.shape
    tm = PLAIN_TM
    return pl.pallas_call(
        _plain_proj_kernel,
        grid=(m // tm, N_PLAIN_TILES),
        in_specs=[pl.BlockSpec((tm, k), lambda i, n: (i, 0)),
                  pl.BlockSpec((None, k, COL_TILE), lambda i, n: (n, 0, 0))],
        out_specs=pl.BlockSpec((None, tm, COL_TILE), lambda i, n: (n, i, 0)),
        out_shape=jax.ShapeDtypeStruct((N_PLAIN_TILES, m, COL_TILE), jnp.bfloat16),
        compiler_params=pltpu.CompilerParams(
            dimension_semantics=("arbitrary", "arbitrary"), vmem_limit_bytes=VMEM_LIMIT),
        name="plain_projection",
    )(x2d, w_plain)


def _linear_scan(a, u, carry):
    tm = a.shape[0]
    groups = tm // SUBLANES
    a3 = a.reshape(groups, SUBLANES, LANES)
    u3 = u.reshape(groups, SUBLANES, LANES)
    row = lax.broadcasted_iota(jnp.int32, (groups, SUBLANES, LANES), 1)
    for d in (1, 2, 4):
        keep = row >= d
        a_sh = pltpu.roll(a3, d, axis=1)
        u_sh = pltpu.roll(u3, d, axis=1)
        u3 = jnp.where(keep, a3 * u_sh + u3, u3)
        a3 = jnp.where(keep, a3 * a_sh, a3)
    hs = []
    for g in range(groups):
        hg = u3[g] + a3[g] * carry
        hs.append(hg)
        carry = hg[SUBLANES - 1:SUBLANES, :]
    return jnp.concatenate(hs, axis=0), carry


def _rnn_proj_kernel(x_ref, wx_ref, wg_ref, wr_ref, wi_ref, br_ref, bi_ref, lam_ref, convw_ref, convb_ref,
                     wao_ref, wro_ref, wo_ref, wpa_ref, wpm_ref,
                     yr_ref, wao_bf_ref, wro_bf_ref, wo_bf_ref, wp_bf_ref, hist_scr, hcar_scr, *, tiles_per_seq):
    tm = RNN_TM
    f32 = jnp.float32
    bf16 = jnp.bfloat16
    first = (pl.program_id(1) % tiles_per_seq) == 0

    hist_scr[0:SUBLANES, :] = jnp.where(first, 0.0, hist_scr[tm:tm + SUBLANES, :])
    carry_all = jnp.where(first, 0.0, hcar_scr[...])

    xb = x_ref[...].astype(bf16)
    hist_scr[SUBLANES:SUBLANES + tm, :] = jnp.dot(xb, wx_ref[...].astype(bf16), preferred_element_type=f32)
    wao_bf_ref[...] = wao_ref[...].astype(bf16)
    wro_bf_ref[...] = wro_ref[...].astype(bf16)
    wo_bf_ref[...] = wo_ref[...].astype(bf16)
    w_plain = jnp.concatenate([wpa_ref[0], wpm_ref[0]], axis=1).astype(bf16)
    for t in range(N_PLAIN_TILES):
        wp_bf_ref[t] = w_plain[:, t * COL_TILE:(t + 1) * COL_TILE]
    wg = wg_ref[...].astype(bf16)
    g_top = jnp.dot(xb[:tm // 2], wg, preferred_element_type=f32)

    neg_lam = -lam_ref[...]
    half_scale = (-0.5 * LRU_C) * (jnp.maximum(neg_lam, 0.0) + jnp.log1p(jnp.exp(-jnp.abs(neg_lam))))
    half_br = 0.5 * br_ref[...]
    half_bi = 0.5 * bi_ref[...]
    for b in range(COL_TILE // RNN_BLOCK_DIM):
        c0 = b * RNN_BLOCK_DIM
        c1 = c0 + RNN_BLOCK_DIM
        xc = convb_ref[:, c0:c1]
        for kk in range(CONV_WIDTH):
            off = SUBLANES - (CONV_WIDTH - 1) + kk
            xc = xc + convw_ref[kk:kk + 1, c0:c1] * hist_scr[off:off + tm, c0:c1]
        w_gates = (0.5 * jnp.concatenate([wr_ref[b], wi_ref[b]], axis=1)).astype(bf16)
        ri = jnp.dot(xc.astype(bf16), w_gates, preferred_element_type=f32)
        if b == 0:
            g = jnp.concatenate([g_top, jnp.dot(xb[tm // 2:], wg, preferred_element_type=f32)], axis=0)
        tanh_r = jnp.tanh(ri[:, :RNN_BLOCK_DIM] + half_br[:, c0:c1])
        tanh_i = jnp.tanh(ri[:, RNN_BLOCK_DIM:] + half_bi[:, c0:c1])
        log_a = tanh_r * half_scale[:, c0:c1] + half_scale[:, c0:c1]
        a = jnp.exp(log_a)
        one_m_a2 = jnp.tanh(log_a) * (-1.0 - a * a)
        mult = jnp.where(one_m_a2 > 0.0, one_m_a2 * lax.rsqrt(one_m_a2), 0.0)
        u = (mult * xc) * (0.5 * tanh_i + 0.5)
        h, carry = _linear_scan(a, u, carry_all[:, c0:c1])
        hcar_scr[:, c0:c1] = carry
        gb = g[:, c0:c1]
        yr_ref[:, c0:c1] = (h * (gb * _sigmoid(gb))).astype(yr_ref.dtype)


def _rnn_projection(x2d, w_in, layer, w_r, w_i, b_r, b_i, lam, convw, convb, wao, wro, wo, *, seq):
    m, k = x2d.shape
    tm = RNN_TM
    nblk = COL_TILE // RNN_BLOCK_DIM
    grid = (RNN_WIDTH // COL_TILE, m // tm)
    gate_w = pl.BlockSpec((None, nblk, RNN_BLOCK_DIM, RNN_BLOCK_DIM), lambda c, i: (layer, c, 0, 0))
    row = pl.BlockSpec((None, 1, COL_TILE), lambda c, i: (layer, 0, c))

    weights = (wao, wro, wo)
    cast_in, cast_out = [], []
    for w in weights:
        nb = w.shape[1] // CAST_ROWS
        assert nb <= grid[0] * grid[1]
        blk = lambda c, i, nb=nb: jnp.minimum(c * grid[1] + i, nb - 1)
        cast_in.append(pl.BlockSpec((None, CAST_ROWS, w.shape[2]), lambda c, i, blk=blk: (layer, blk(c, i), 0)))
        cast_out.append(pl.BlockSpec((CAST_ROWS, w.shape[2]), lambda c, i, blk=blk: (blk(c, i), 0)))
    w_rows = k // (grid[0] * grid[1])
    for col0, width in ((Q0, XR0 - Q0), (MA0, IN_WIDTH - MA0)):
        cast_in.append(pl.BlockSpec(
            (pl.Element(1), pl.Element(w_rows), pl.Element(width)),
            lambda c, i, col0=col0: (layer, pl.multiple_of((c * grid[1] + i) * w_rows, SUBLANES), col0)))
    cast_out.append(pl.BlockSpec((N_PLAIN_TILES, w_rows, COL_TILE), lambda c, i: (0, c * grid[1] + i, 0)))

    return pl.pallas_call(
        functools.partial(_rnn_proj_kernel, tiles_per_seq=seq // tm),
        grid=grid,
        in_specs=[pl.BlockSpec((tm, k), lambda c, i: (i, 0)),
                  pl.BlockSpec((None, k, COL_TILE), lambda c, i: (layer, 0, XR0 // COL_TILE + c)),
                  pl.BlockSpec((None, k, COL_TILE), lambda c, i: (layer, 0, GR0 // COL_TILE + c)),
                  gate_w, gate_w, row, row, row,
                  pl.BlockSpec((None, CONV_WIDTH, COL_TILE), lambda c, i: (layer, 0, c)),
                  row] + cast_in,
        out_specs=[pl.BlockSpec((tm, COL_TILE), lambda c, i: (i, c))] + cast_out,
        out_shape=[jax.ShapeDtypeStruct((m, RNN_WIDTH), jnp.bfloat16)]
                  + [jax.ShapeDtypeStruct(w.shape[1:], jnp.bfloat16) for w in weights]
                  + [jax.ShapeDtypeStruct((N_PLAIN_TILES, k, COL_TILE), jnp.bfloat16)],
        scratch_shapes=[pltpu.VMEM((SUBLANES + tm, COL_TILE), jnp.float32),
                        pltpu.VMEM((1, COL_TILE), jnp.float32)],
        compiler_params=pltpu.CompilerParams(
            dimension_semantics=("arbitrary", "arbitrary"), vmem_limit_bytes=VMEM_LIMIT),
        name="rnn_projection",
    )(x2d, w_in, w_in, w_r, w_i, b_r, b_i, lam, convw, convb, wao, wro, wo, w_in, w_in)


def _mixer_kernel(relb_ref, sinks_ref,
                  pa_ref, yr_ref, x_ref, bucket_ref, wao_ref, wro_ref, wo_ref, lng_ref, lnb_ref,
                  out_ref,
                  bias_scr, kx_scr, vx_scr, attn_scr,
                  *, alpha):
    tm = MIX_TM
    nqb = tm // WINDOW
    first_tile = pl.program_id(1) == 0
    f32 = jnp.float32
    bf16 = jnp.bfloat16

    def pa(rows, c0, width):
        parts = []
        while width > 0:
            t, off = divmod(c0, COL_TILE)
            w = min(width, COL_TILE - off)
            parts.append(pa_ref[t, rows, off:off + w])
            c0, width = c0 + w, width - w
        return parts[0] if len(parts) == 1 else jnp.concatenate(parts, axis=1)

    @pl.when((pl.program_id(0) == 0) & first_tile)
    def _():
        bucket = bucket_ref[...]
        for t in range(N_Q_HEADS):
            h = 4 * (t // 4) + 2 * (t % 2) + (t // 2) % 2
            acc = jnp.full((WINDOW, 2 * WINDOW), NEG_INF, f32)
            for b in range(N_BUCKETS):
                acc = jnp.where(bucket == b, relb_ref[b, h], acc)
            bias_scr[t * WINDOW:(t + 1) * WINDOW, :] = acc

    for scr in (kx_scr, vx_scr):
        prev = scr[:, tm:tm + WINDOW, :]
        scr[:, 0:WINDOW, :] = jnp.where(first_tile, jnp.zeros_like(prev), prev)

    u32 = jnp.uint32
    lo_u = lax.broadcasted_iota(jnp.int32, (tm // 2, LANES), 1) < HEAD_DIM
    zero_u = jnp.zeros((tm // 2, LANES), u32)
    for p in range(N_KV_HEADS // 2):
        for src0, scr in ((PA_K0, kx_scr), (PA_V0, vx_scr)):
            t = pltpu.bitcast(pa(slice(None), src0 + p * LANES, LANES), u32)
            tr = pltpu.roll(t, HEAD_DIM, axis=1)
            variants = (jnp.where(lo_u, t, zero_u),
                        jnp.where(lo_u, zero_u, tr),
                        jnp.where(lo_u, tr, zero_u),
                        jnp.where(lo_u, zero_u, t))
            for v, val in enumerate(variants):
                scr[4 * p + v, WINDOW:WINDOW + tm, :] = pltpu.bitcast(val, bf16)

    lo_q = lax.broadcasted_iota(jnp.int32, (WINDOW, LANES), 1) < HEAD_DIM
    col = lax.broadcasted_iota(jnp.int32, (1, 2 * WINDOW), 1)
    ones_blk = jnp.ones((2 * WINDOW, LANES), bf16)
    nt_dims = (((1,), (1,)), ((), ()))
    y_rnn_parts = []
    for j in range(nqb):
        r0 = j * WINDOW
        s_parts = []
        for kh in range(N_KV_HEADS):
            qq = jnp.concatenate(
                [pa(slice(r0, r0 + WINDOW), PA_Q0 + (2 * kh + pr) * LANES, LANES) for pr in range(2)],
                axis=0) * jnp.asarray(HEAD_DIM ** -0.5, bf16)
            for par in range(2):
                kk = kx_scr[2 * kh + par, r0:r0 + 2 * WINDOW, :]
                s_parts.append(lax.dot_general(qq, kk, nt_dims, preferred_element_type=f32))
        cw = D_MODEL // nqb
        y_rnn_parts.append(jnp.dot(yr_ref[...], wro_ref[:, j * cw:(j + 1) * cw], preferred_element_type=f32))
        s = jnp.concatenate(s_parts, axis=0) + bias_scr[...]
        if j == 0:
            s = s + jnp.where(first_tile & (col < WINDOW), NEG_INF, 0.0).astype(f32)
        m = jnp.max(s, axis=-1, keepdims=True)
        p = jnp.exp(s - m).astype(bf16)
        for kh in range(N_KV_HEADS):
            o_ext = []
            for par in range(2):
                t0 = (2 * kh + par) * 2 * WINDOW
                vv = jnp.concatenate([vx_scr[2 * kh + par, r0:r0 + 2 * WINDOW, :], ones_blk], axis=1)
                o_ext.append(jnp.dot(p[t0:t0 + 2 * WINDOW, :], vv, preferred_element_type=f32))
            for pr in range(2):
                pair = 2 * kh + pr
                rows = slice(pr * WINDOW, (pr + 1) * WINDOW)
                rden = []
                for par in range(2):
                    t0 = (2 * kh + par) * 2 * WINDOW + pr * WINDOW
                    sink_term = jnp.exp(sinks_ref[2 * pair + par] - m[t0:t0 + WINDOW, :])
                    rden.append(1.0 / (o_ext[par][rows, LANES:] + sink_term))
                o = (o_ext[0][rows, :LANES] + o_ext[1][rows, :LANES]) * jnp.where(lo_q, rden[0], rden[1])
                ga = pa(slice(r0, r0 + WINDOW), PA_GA0 + pair * LANES, LANES).astype(f32)
                attn_scr[r0:r0 + WINDOW, pair * LANES:(pair + 1) * LANES] = (
                    o * (ga * _sigmoid(ga))).astype(bf16)

    y_attn = jnp.dot(attn_scr[...], wao_ref[...], preferred_element_type=f32)
    y_rnn = jnp.concatenate(y_rnn_parts, axis=1)
    mixed = (_sigmoid(pa(slice(None), PA_MA0, D_MODEL).astype(f32)) * y_attn
             + _sigmoid(pa(slice(None), PA_MR0, D_MODEL).astype(f32)) * y_rnn).astype(bf16)
    zs = []
    row_sum = jnp.zeros((tm, 1), f32)
    for c in range(D_MODEL // COL_TILE):
        k0 = c * COL_TILE
        z = alpha * x_ref[:, k0:k0 + COL_TILE] + jnp.dot(mixed, wo_ref[:, k0:k0 + COL_TILE],
                                                         preferred_element_type=f32)
        row_sum = row_sum + jnp.sum(z, axis=-1, keepdims=True)
        zs.append(z)
    zc = jnp.concatenate(zs, axis=1) - row_sum * (1.0 / D_MODEL)
    var = jnp.mean(zc * zc, axis=-1, keepdims=True)
    out_ref[...] = zc * lax.rsqrt(var + LN_EPS) * lng_ref[...] + lnb_ref[...]


def _const_spec(shape):
    nd = len(shape)
    return pl.BlockSpec(shape, lambda b, s: (0,) * nd, pipeline_mode=pl.Buffered(1))


def _mixer(pa, yr, x2d, bucket, relb, sinks, wao, wro, wo, lng, lnb, *, batch, seq, alpha):
    tm = MIX_TM
    nst = seq // tm
    row_map = lambda b, s: (b * nst + s, 0)
    smem = pl.BlockSpec(memory_space=pltpu.SMEM)
    in_specs = [
        smem, smem,
        pl.BlockSpec((N_PLAIN_TILES, tm, COL_TILE), lambda b, s: (0, b * nst + s, 0)),
        pl.BlockSpec((tm, RNN_WIDTH), row_map),
        pl.BlockSpec((tm, D_MODEL), row_map),
        _const_spec(bucket.shape), _const_spec(wao.shape), _const_spec(wro.shape), _const_spec(wo.shape),
        _const_spec(lng.shape), _const_spec(lnb.shape),
    ]
    scratch = [
        pltpu.VMEM((N_Q_HEADS * WINDOW, 2 * WINDOW), jnp.float32),
        pltpu.VMEM((8, WINDOW + tm, LANES), jnp.bfloat16),
        pltpu.VMEM((8, WINDOW + tm, LANES), jnp.bfloat16),
        pltpu.VMEM((tm, ATTN_WIDTH), jnp.bfloat16),
    ]
    return pl.pallas_call(
        functools.partial(_mixer_kernel, alpha=alpha),
        grid=(batch, nst),
        in_specs=in_specs,
        out_specs=pl.BlockSpec((tm, D_MODEL), row_map),
        out_shape=jax.ShapeDtypeStruct((batch * seq, D_MODEL), jnp.float32),
        scratch_shapes=scratch,
        compiler_params=pltpu.CompilerParams(
            dimension_semantics=("arbitrary", "arbitrary"), vmem_limit_bytes=VMEM_LIMIT),
        name="mixer",
    )(relb, sinks, pa, yr, x2d, bucket, wao, wro, wo, lng, lnb)


def kernel(x, w_in, conv_w, conv_b, w_r, b_r, w_i, b_i, lru_lambda, sinks, w_attn_out, w_rnn_out, w_out,
           ln_g, ln_b, rel_bias):
    batch, seq, d = x.shape
    depth = w_in.shape[0]
    alpha = (2.0 * depth) ** 0.25
    bucket = jnp.asarray(_t5_bucket_table())
    x2d = x.reshape(batch * seq, d)
    as_rows = lambda v: v.reshape(depth, 1, v.shape[-1])
    b_r, b_i, lam, conv_b = as_rows(b_r), as_rows(b_i), as_rows(lru_lambda), as_rows(conv_b)
    for l in range(depth):
        yr, wao, wro, wo, w_plain = _rnn_projection(x2d, w_in, l, w_r, w_i, b_r, b_i, lam, conv_w, conv_b,
                                                    w_attn_out, w_rnn_out, w_out, seq=seq)
        pa = _plain_projection(x2d, w_plain)
        x2d = _mixer(pa, yr, x2d, bucket, rel_bias, sinks[l], wao, wro, wo,
                     ln_g[l][None, :], ln_b[l][None, :], batch=batch, seq=seq, alpha=alpha)
    return x2d.reshape(batch, seq, d)
```

```python
import functools
import math

import numpy as np
import jax
import jax.numpy as jnp
from jax import lax
from jax.experimental import pallas as pl
from jax.experimental.pallas import tpu as pltpu

D_MODEL = 2048
N_Q_HEADS = 16
N_KV_HEADS = 4
HEAD_DIM = 64
ATTN_WIDTH = N_Q_HEADS * HEAD_DIM
KV_WIDTH = N_KV_HEADS * HEAD_DIM
WINDOW = 128
N_BUCKETS = 32
MAX_DISTANCE = 128
RNN_WIDTH = D_MODEL
RNN_BLOCKS = 16
RNN_BLOCK_DIM = RNN_WIDTH // RNN_BLOCKS
CONV_WIDTH = 4
LRU_C = 8.0
LN_EPS = 1e-5
NEG_INF = -1e30

Q0 = 0
K0 = Q0 + ATTN_WIDTH
V0 = K0 + KV_WIDTH
GA0 = V0 + KV_WIDTH
XR0 = GA0 + ATTN_WIDTH
GR0 = XR0 + RNN_WIDTH
MA0 = GR0 + RNN_WIDTH
MR0 = MA0 + D_MODEL
IN_WIDTH = MR0 + D_MODEL

LANES = 128
SUBLANES = 8
VMEM_LIMIT = 60 * 1024 * 1024

COL_TILE = 512
PLAIN_TM = 2048
RNN_TM = 1024
MIX_TM = 256
CAST_ROWS = 64

PA_Q0 = 0
PA_K0 = PA_Q0 + ATTN_WIDTH
PA_V0 = PA_K0 + KV_WIDTH
PA_GA0 = PA_V0 + KV_WIDTH
PA_MA0 = PA_GA0 + ATTN_WIDTH
PA_MR0 = PA_MA0 + D_MODEL
PA_WIDTH = PA_MR0 + D_MODEL
N_PLAIN_TILES = PA_WIDTH // COL_TILE
N_PRE_RNN_TILES = XR0 // COL_TILE
N_RNN_COL_TILES = (MA0 - XR0) // COL_TILE


def _t5_bucket_table():
    qi = np.arange(WINDOW)[:, None]
    kj = np.arange(2 * WINDOW)[None, :]
    dist = qi + WINDOW - kj
    max_exact = N_BUCKETS // 2
    n = np.maximum(dist, 0)
    nf = np.maximum(n, max_exact).astype(np.float32)
    large = max_exact + (np.log(nf / np.float32(max_exact)) / np.float32(math.log(MAX_DISTANCE / max_exact))
                         * np.float32(N_BUCKETS - max_exact)).astype(np.int32)
    large = np.minimum(large, N_BUCKETS - 1)
    bucket = np.where(n < max_exact, n, large)
    in_window = (dist >= 0) & (dist < WINDOW)
    return np.where(in_window, bucket, -1).astype(np.int32)


def _sigmoid(x):
    return 0.5 * jnp.tanh(0.5 * x) + 0.5


def _plain_proj_kernel(x_ref, w_ref, pa_ref):
    pa_ref[...] = jnp.dot(x_ref[...].astype(jnp.bfloat16), w_ref[...].astype(jnp.bfloat16),
                          preferred_element_type=jnp.float32).astype(pa_ref.dtype)


def _plain_projection(x2d, w_in, layer):
    m, k = x2d.shape
    tm = PLAIN_TM

    def w_map(i, n):
        return (layer, 0, jnp.where(n < N_PRE_RNN_TILES, n, n + N_RNN_COL_TILES))

    return pl.pallas_call(
        _plain_proj_kernel,
        grid=(m // tm, N_PLAIN_TILES),
        in_specs=[pl.BlockSpec((tm, k), lambda i, n: (i, 0)),
                  pl.BlockSpec((None, k, COL_TILE), w_map)],
        out_specs=pl.BlockSpec((tm, COL_TILE), lambda i, n: (i, n)),
        out_shape=jax.ShapeDtypeStruct((m, PA_WIDTH), jnp.bfloat16),
        compiler_params=pltpu.CompilerParams(
            dimension_semantics=("arbitrary", "arbitrary"), vmem_limit_bytes=VMEM_LIMIT),
        name="plain_projection",
    )(x2d, w_in)


def _linear_scan(a, u, carry):
    tm = a.shape[0]
    groups = tm // SUBLANES
    a3 = a.reshape(groups, SUBLANES, LANES)
    u3 = u.reshape(groups, SUBLANES, LANES)
    row = lax.broadcasted_iota(jnp.int32, (groups, SUBLANES, LANES), 1)
    for d in (1, 2, 4):
        keep = row >= d
        a_sh = pltpu.roll(a3, d, axis=1)
        u_sh = pltpu.roll(u3, d, axis=1)
        u3 = jnp.where(keep, a3 * u_sh + u3, u3)
        a3 = jnp.where(keep, a3 * a_sh, a3)
    hs = []
    for g in range(groups):
        hg = u3[g] + a3[g] * carry
        hs.append(hg)
        carry = hg[SUBLANES - 1:SUBLANES, :]
    return jnp.concatenate(hs, axis=0), carry


def _rnn_proj_kernel(x_ref, wx_ref, wg_ref, wr_ref, wi_ref, br_ref, bi_ref, lam_ref, convw_ref, convb_ref,
                     wao_ref, wro_ref, wo_ref,
                     yr_ref, wao_bf_ref, wro_bf_ref, wo_bf_ref, hist_scr, hcar_scr, *, tiles_per_seq):
    tm = RNN_TM
    f32 = jnp.float32
    bf16 = jnp.bfloat16
    first = (pl.program_id(1) % tiles_per_seq) == 0

    hist_scr[0:SUBLANES, :] = jnp.where(first, 0.0, hist_scr[tm:tm + SUBLANES, :])
    carry_all = jnp.where(first, 0.0, hcar_scr[...])

    xb = x_ref[...].astype(bf16)
    hist_scr[SUBLANES:SUBLANES + tm, :] = jnp.dot(xb, wx_ref[...].astype(bf16), preferred_element_type=f32)
    wao_bf_ref[...] = wao_ref[...].astype(bf16)
    wro_bf_ref[...] = wro_ref[...].astype(bf16)
    wo_bf_ref[...] = wo_ref[...].astype(bf16)
    wg = wg_ref[...].astype(bf16)
    g_top = jnp.dot(xb[:tm // 2], wg, preferred_element_type=f32)

    neg_lam = -lam_ref[...]
    half_scale = (-0.5 * LRU_C) * (jnp.maximum(neg_lam, 0.0) + jnp.log1p(jnp.exp(-jnp.abs(neg_lam))))
    half_br = 0.5 * br_ref[...]
    half_bi = 0.5 * bi_ref[...]
    row8 = lax.broadcasted_iota(jnp.int32, (SUBLANES, RNN_BLOCK_DIM), 0)
    for b in range(COL_TILE // RNN_BLOCK_DIM):
        c0 = b * RNN_BLOCK_DIM
        c1 = c0 + RNN_BLOCK_DIM
        x_cur = hist_scr[SUBLANES:SUBLANES + tm, c0:c1]
        x_old = hist_scr[0:SUBLANES, c0:c1]
        acc, acc_old = None, None
        for kk in range(CONV_WIDTH - 1):
            wk = convw_ref[kk:kk + 1, c0:c1]
            acc = wk * x_cur if acc is None else wk * x_cur + acc
            acc_old = wk * x_old if acc_old is None else wk * x_old + acc_old
            acc_old = pltpu.roll(acc_old, 1, axis=0)
            shifted = pltpu.roll(acc, 1, axis=0)
            top = jnp.where(row8 == 0, acc_old, shifted[0:SUBLANES, :])
            acc = jnp.concatenate([top, shifted[SUBLANES:, :]], axis=0)
        xc = convw_ref[CONV_WIDTH - 1:CONV_WIDTH, c0:c1] * x_cur + acc + convb_ref[:, c0:c1]
        w_gates = (0.5 * jnp.concatenate([wr_ref[b], wi_ref[b]], axis=1)).astype(bf16)
        ri = jnp.dot(xc.astype(bf16), w_gates, preferred_element_type=f32)
        if b == 0:
            g = jnp.concatenate([g_top, jnp.dot(xb[tm // 2:], wg, preferred_element_type=f32)], axis=0)
        tanh_r = jnp.tanh(ri[:, :RNN_BLOCK_DIM] + half_br[:, c0:c1])
        tanh_i = jnp.tanh(ri[:, RNN_BLOCK_DIM:] + half_bi[:, c0:c1])
        log_a = tanh_r * half_scale[:, c0:c1] + half_scale[:, c0:c1]
        a = jnp.exp(log_a)
        one_m_a2 = jnp.tanh(log_a) * (-1.0 - a * a)
        mult = jnp.where(one_m_a2 > 0.0, one_m_a2 * lax.rsqrt(one_m_a2), 0.0)
        u = (mult * xc) * (0.5 * tanh_i + 0.5)
        h, carry = _linear_scan(a, u, carry_all[:, c0:c1])
        hcar_scr[:, c0:c1] = carry
        gb = g[:, c0:c1]
        yr_ref[:, c0:c1] = (h * (gb * _sigmoid(gb))).astype(yr_ref.dtype)


def _rnn_projection(x2d, w_in, layer, w_r, w_i, b_r, b_i, lam, convw, convb, wao, wro, wo, *, seq):
    m, k = x2d.shape
    tm = RNN_TM
    nblk = COL_TILE // RNN_BLOCK_DIM
    grid = (RNN_WIDTH // COL_TILE, m // tm)
    gate_w = pl.BlockSpec((None, nblk, RNN_BLOCK_DIM, RNN_BLOCK_DIM), lambda c, i: (layer, c, 0, 0))
    row = pl.BlockSpec((None, 1, COL_TILE), lambda c, i: (layer, 0, c))

    weights = (wao, wro, wo)
    cast_in, cast_out = [], []
    for w in weights:
        nb = w.shape[1] // CAST_ROWS
        assert nb <= grid[0] * grid[1]
        blk = lambda c, i, nb=nb: jnp.minimum(c * grid[1] + i, nb - 1)
        cast_in.append(pl.BlockSpec((None, CAST_ROWS, w.shape[2]), lambda c, i, blk=blk: (layer, blk(c, i), 0)))
        cast_out.append(pl.BlockSpec((CAST_ROWS, w.shape[2]), lambda c, i, blk=blk: (blk(c, i), 0)))

    return pl.pallas_call(
        functools.partial(_rnn_proj_kernel, tiles_per_seq=seq // tm),
        grid=grid,
        in_specs=[pl.BlockSpec((tm, k), lambda c, i: (i, 0)),
                  pl.BlockSpec((None, k, COL_TILE), lambda c, i: (layer, 0, XR0 // COL_TILE + c)),
                  pl.BlockSpec((None, k, COL_TILE), lambda c, i: (layer, 0, GR0 // COL_TILE + c)),
                  gate_w, gate_w, row, row, row,
                  pl.BlockSpec((None, CONV_WIDTH, COL_TILE), lambda c, i: (layer, 0, c)),
                  row] + cast_in,
        out_specs=[pl.BlockSpec((tm, COL_TILE), lambda c, i: (i, c))] + cast_out,
        out_shape=[jax.ShapeDtypeStruct((m, RNN_WIDTH), jnp.bfloat16)]
                  + [jax.ShapeDtypeStruct(w.shape[1:], jnp.bfloat16) for w in weights],
        scratch_shapes=[pltpu.VMEM((SUBLANES + tm, COL_TILE), jnp.float32),
                        pltpu.VMEM((1, COL_TILE), jnp.float32)],
        compiler_params=pltpu.CompilerParams(
            dimension_semantics=("arbitrary", "arbitrary"), vmem_limit_bytes=VMEM_LIMIT),
        name="rnn_projection",
    )(x2d, w_in, w_in, w_r, w_i, b_r, b_i, lam, convw, convb, wao, wro, wo)


def _mixer_kernel(relb_ref, sinks_ref,
                  pa_ref, yr_ref, x_ref, bucket_ref, wao_ref, wro_ref, wo_ref, lng_ref, lnb_ref,
                  out_ref,
                  bias_scr, kx_scr, vx_scr, attn_scr,
                  *, alpha):
    tm = MIX_TM
    nqb = tm // WINDOW
    first_tile = pl.program_id(1) == 0
    f32 = jnp.float32
    bf16 = jnp.bfloat16

    @pl.when((pl.program_id(0) == 0) & first_tile)
    def _():
        bucket = bucket_ref[...]
        for t in range(N_Q_HEADS):
            h = 4 * (t // 4) + 2 * (t % 2) + (t // 2) % 2
            acc = jnp.full((WINDOW, 2 * WINDOW), NEG_INF, f32)
            for b in range(N_BUCKETS):
                acc = jnp.where(bucket == b, relb_ref[b, h], acc)
            bias_scr[t * WINDOW:(t + 1) * WINDOW, :] = acc

    for scr in (kx_scr, vx_scr):
        prev = scr[:, tm:tm + WINDOW, :]
        scr[:, 0:WINDOW, :] = jnp.where(first_tile, jnp.zeros_like(prev), prev)

    u32 = jnp.uint32
    lo_u = lax.broadcasted_iota(jnp.int32, (tm // 2, LANES), 1) < HEAD_DIM
    zero_u = jnp.zeros((tm // 2, LANES), u32)
    for p in range(N_KV_HEADS // 2):
        for src0, scr in ((PA_K0, kx_scr), (PA_V0, vx_scr)):
            t = pltpu.bitcast(pa_ref[:, src0 + p * LANES: src0 + (p + 1) * LANES], u32)
            tr = pltpu.roll(t, HEAD_DIM, axis=1)
            variants = (jnp.where(lo_u, t, zero_u),
                        jnp.where(lo_u, zero_u, tr),
                        jnp.where(lo_u, tr, zero_u),
                        jnp.where(lo_u, zero_u, t))
            for v, val in enumerate(variants):
                scr[4 * p + v, WINDOW:WINDOW + tm, :] = pltpu.bitcast(val, bf16)

    lo_q = lax.broadcasted_iota(jnp.int32, (WINDOW, LANES), 1) < HEAD_DIM
    col = lax.broadcasted_iota(jnp.int32, (1, 2 * WINDOW), 1)
    ones_blk = jnp.ones((2 * WINDOW, LANES), bf16)
    nt_dims = (((1,), (1,)), ((), ()))
    y_rnn_parts = []
    for j in range(nqb):
        r0 = j * WINDOW
        s_parts = []
        for kh in range(N_KV_HEADS):
            qq = jnp.concatenate(
                [pa_ref[r0:r0 + WINDOW, PA_Q0 + (2 * kh + pr) * LANES: PA_Q0 + (2 * kh + pr + 1) * LANES]
                 for pr in range(2)], axis=0) * jnp.asarray(HEAD_DIM ** -0.5, bf16)
            for par in range(2):
                kk = kx_scr[2 * kh + par, r0:r0 + 2 * WINDOW, :]
                s_parts.append(lax.dot_general(qq, kk, nt_dims, preferred_element_type=f32))
        cw = D_MODEL // nqb
        y_rnn_parts.append(jnp.dot(yr_ref[...], wro_ref[:, j * cw:(j + 1) * cw], preferred_element_type=f32))
        s = jnp.concatenate(s_parts, axis=0) + bias_scr[...]
        if j == 0:
            s = s + jnp.where(first_tile & (col < WINDOW), NEG_INF, 0.0).astype(f32)
        m = jnp.max(s, axis=-1, keepdims=True)
        p = jnp.exp(s - m).astype(bf16)
        for kh in range(N_KV_HEADS):
            o_ext = []
            for par in range(2):
                t0 = (2 * kh + par) * 2 * WINDOW
                vv = jnp.concatenate([vx_scr[2 * kh + par, r0:r0 + 2 * WINDOW, :], ones_blk], axis=1)
                o_ext.append(jnp.dot(p[t0:t0 + 2 * WINDOW, :], vv, preferred_element_type=f32))
            for pr in range(2):
                pair = 2 * kh + pr
                rows = slice(pr * WINDOW, (pr + 1) * WINDOW)
                rden = []
                for par in range(2):
                    t0 = (2 * kh + par) * 2 * WINDOW + pr * WINDOW
                    sink_term = jnp.exp(sinks_ref[2 * pair + par] - m[t0:t0 + WINDOW, :])
                    rden.append(1.0 / (o_ext[par][rows, LANES:] + sink_term))
                o = (o_ext[0][rows, :LANES] + o_ext[1][rows, :LANES]) * jnp.where(lo_q, rden[0], rden[1])
                ga = pa_ref[r0:r0 + WINDOW, PA_GA0 + pair * LANES: PA_GA0 + (pair + 1) * LANES].astype(f32)
                attn_scr[r0:r0 + WINDOW, pair * LANES:(pair + 1) * LANES] = (
                    o * (ga * _sigmoid(ga))).astype(bf16)

    y_attn = jnp.dot(attn_scr[...], wao_ref[...], preferred_element_type=f32)
    y_rnn = jnp.concatenate(y_rnn_parts, axis=1)
    mixed = (_sigmoid(pa_ref[:, PA_MA0:PA_MA0 + D_MODEL].astype(f32)) * y_attn
             + _sigmoid(pa_ref[:, PA_MR0:PA_MR0 + D_MODEL].astype(f32)) * y_rnn).astype(bf16)
    zs = []
    row_sum = jnp.zeros((tm, 1), f32)
    for c in range(D_MODEL // COL_TILE):
        k0 = c * COL_TILE
        z = alpha * x_ref[:, k0:k0 + COL_TILE] + jnp.dot(mixed, wo_ref[:, k0:k0 + COL_TILE],
                                                         preferred_element_type=f32)
        row_sum = row_sum + jnp.sum(z, axis=-1, keepdims=True)
        zs.append(z)
    zc = jnp.concatenate(zs, axis=1) - row_sum * (1.0 / D_MODEL)
    var = jnp.mean(zc * zc, axis=-1, keepdims=True)
    out_ref[...] = zc * lax.rsqrt(var + LN_EPS) * lng_ref[...] + lnb_ref[...]


def _const_spec(shape):
    nd = len(shape)
    return pl.BlockSpec(shape, lambda b, s: (0,) * nd, pipeline_mode=pl.Buffered(1))


def _mixer(pa, yr, x2d, bucket, relb, sinks, wao, wro, wo, lng, lnb, *, batch, seq, alpha):
    tm = MIX_TM
    nst = seq // tm
    row_map = lambda b, s: (b * nst + s, 0)
    smem = pl.BlockSpec(memory_space=pltpu.SMEM)
    in_specs = [
        smem, smem,
        pl.BlockSpec((tm, PA_WIDTH), row_map),
        pl.BlockSpec((tm, RNN_WIDTH), row_map),
        pl.BlockSpec((tm, D_MODEL), row_map),
        _const_spec(bucket.shape), _const_spec(wao.shape), _const_spec(wro.shape), _const_spec(wo.shape),
        _const_spec(lng.shape), _const_spec(lnb.shape),
    ]
    scratch = [
        pltpu.VMEM((N_Q_HEADS * WINDOW, 2 * WINDOW), jnp.float32),
        pltpu.VMEM((8, WINDOW + tm, LANES), jnp.bfloat16),
        pltpu.VMEM((8, WINDOW + tm, LANES), jnp.bfloat16),
        pltpu.VMEM((tm, ATTN_WIDTH), jnp.bfloat16),
    ]
    return pl.pallas_call(
        functools.partial(_mixer_kernel, alpha=alpha),
        grid=(batch, nst),
        in_specs=in_specs,
        out_specs=pl.BlockSpec((tm, D_MODEL), row_map),
        out_shape=jax.ShapeDtypeStruct((batch * seq, D_MODEL), jnp.float32),
        scratch_shapes=scratch,
        compiler_params=pltpu.CompilerParams(
            dimension_semantics=("arbitrary", "arbitrary"), vmem_limit_bytes=VMEM_LIMIT),
        name="mixer",
    )(relb, sinks, pa, yr, x2d, bucket, wao, wro, wo, lng, lnb)


def kernel(x, w_in, conv_w, conv_b, w_r, b_r, w_i, b_i, lru_lambda, sinks, w_attn_out, w_rnn_out, w_out,
           ln_g, ln_b, rel_bias):
    batch, seq, d = x.shape
    depth = w_in.shape[0]
    alpha = (2.0 * depth) ** 0.25
    bucket = jnp.asarray(_t5_bucket_table())
    x2d = x.reshape(batch * seq, d)
    as_rows = lambda v: v.reshape(depth, 1, v.shape[-1])
    b_r, b_i, lam, conv_b = as_rows(b_r), as_rows(b_i), as_rows(lru_lambda), as_rows(conv_b)
    for l in range(depth):
        pa = _plain_projection(x2d, w_in, l)
        yr, wao, wro, wo = _rnn_projection(x2d, w_in, l, w_r, w_i, b_r, b_i, lam, conv_w, conv_b,
                                           w_attn_out, w_rnn_out, w_out, seq=seq)
        x2d = _mixer(pa, yr, x2d, bucket, rel_bias, sinks[l], wao, wro, wo,
                     ln_g[l][None, :], ln_b[l][None, :], batch=batch, seq=seq, alpha=alpha)
    return x2d.reshape(batch, seq, d)
```

```python
import functools
import math

import numpy as np
import jax
import jax.numpy as jnp
from jax import lax
from jax.experimental import pallas as pl
from jax.experimental.pallas import tpu as pltpu

D_MODEL = 2048
N_Q_HEADS = 16
N_KV_HEADS = 4
HEAD_DIM = 64
ATTN_WIDTH = N_Q_HEADS * HEAD_DIM
KV_WIDTH = N_KV_HEADS * HEAD_DIM
WINDOW = 128
N_BUCKETS = 32
MAX_DISTANCE = 128
RNN_WIDTH = D_MODEL
RNN_BLOCKS = 16
RNN_BLOCK_DIM = RNN_WIDTH // RNN_BLOCKS
CONV_WIDTH = 4
LRU_C = 8.0
LN_EPS = 1e-5
NEG_INF = -1e30

Q0 = 0
K0 = Q0 + ATTN_WIDTH
V0 = K0 + KV_WIDTH
GA0 = V0 + KV_WIDTH
XR0 = GA0 + ATTN_WIDTH
GR0 = XR0 + RNN_WIDTH
MA0 = GR0 + RNN_WIDTH
MR0 = MA0 + D_MODEL
IN_WIDTH = MR0 + D_MODEL

LANES = 128
SUBLANES = 8
VMEM_LIMIT = 60 * 1024 * 1024

COL_TILE = 512
PLAIN_TM = 1024
RNN_TM = 1024
MIX_TM = 256
CAST_ROWS = 64

PA_Q0 = 0
PA_K0 = PA_Q0 + ATTN_WIDTH
PA_V0 = PA_K0 + KV_WIDTH
PA_GA0 = PA_V0 + KV_WIDTH
PA_MA0 = PA_GA0 + ATTN_WIDTH
PA_MR0 = PA_MA0 + D_MODEL
PA_WIDTH = PA_MR0 + D_MODEL
N_PLAIN_TILES = PA_WIDTH // COL_TILE
N_PRE_RNN_TILES = XR0 // COL_TILE
N_RNN_COL_TILES = (MA0 - XR0) // COL_TILE


def _t5_bucket_table():
    qi = np.arange(WINDOW)[:, None]
    kj = np.arange(2 * WINDOW)[None, :]
    dist = qi + WINDOW - kj
    max_exact = N_BUCKETS // 2
    n = np.maximum(dist, 0)
    nf = np.maximum(n, max_exact).astype(np.float32)
    large = max_exact + (np.log(nf / np.float32(max_exact)) / np.float32(math.log(MAX_DISTANCE / max_exact))
                         * np.float32(N_BUCKETS - max_exact)).astype(np.int32)
    large = np.minimum(large, N_BUCKETS - 1)
    bucket = np.where(n < max_exact, n, large)
    in_window = (dist >= 0) & (dist < WINDOW)
    return np.where(in_window, bucket, -1).astype(np.int32)


def _sigmoid(x):
    return 0.5 * jnp.tanh(0.5 * x) + 0.5


def _plain_proj_kernel(x_ref, w_ref, pa_ref):
    pa_ref[...] = jnp.dot(x_ref[...].astype(jnp.bfloat16), w_ref[...].astype(jnp.bfloat16),
                          preferred_element_type=jnp.float32).astype(pa_ref.dtype)


def _plain_projection(x2d, w_in, layer):
    m, k = x2d.shape
    tm = PLAIN_TM

    def w_map(i, n):
        return (layer, 0, jnp.where(n < N_PRE_RNN_TILES, n, n + N_RNN_COL_TILES))

    return pl.pallas_call(
        _plain_proj_kernel,
        grid=(m // tm, N_PLAIN_TILES),
        in_specs=[pl.BlockSpec((tm, k), lambda i, n: (i, 0)),
                  pl.BlockSpec((None, k, COL_TILE), w_map)],
        out_specs=pl.BlockSpec((tm, COL_TILE), lambda i, n: (i, n)),
        out_shape=jax.ShapeDtypeStruct((m, PA_WIDTH), jnp.bfloat16),
        compiler_params=pltpu.CompilerParams(
            dimension_semantics=("arbitrary", "arbitrary"), vmem_limit_bytes=VMEM_LIMIT),
        name="plain_projection",
    )(x2d, w_in)


def _linear_scan(a, u, carry):
    tm = a.shape[0]
    groups = tm // SUBLANES
    a3 = a.reshape(groups, SUBLANES, LANES)
    u3 = u.reshape(groups, SUBLANES, LANES)
    row = lax.broadcasted_iota(jnp.int32, (groups, SUBLANES, LANES), 1)
    for d in (1, 2, 4):
        keep = row >= d
        a_sh = pltpu.roll(a3, d, axis=1)
        u_sh = pltpu.roll(u3, d, axis=1)
        u3 = jnp.where(keep, a3 * u_sh + u3, u3)
        a3 = jnp.where(keep, a3 * a_sh, a3)
    hs = []
    for g in range(groups):
        hg = u3[g] + a3[g] * carry
        hs.append(hg)
        carry = hg[SUBLANES - 1:SUBLANES, :]
    return jnp.concatenate(hs, axis=0), carry


def _rnn_proj_kernel(x_ref, wx_ref, wg_ref, wr_ref, wi_ref, br_ref, bi_ref, lam_ref, convw_ref, convb_ref,
                     wao_ref, wro_ref, wo_ref,
                     yr_ref, wao_bf_ref, wro_bf_ref, wo_bf_ref, hist_scr, hcar_scr, *, tiles_per_seq):
    tm = RNN_TM
    f32 = jnp.float32
    bf16 = jnp.bfloat16
    first = (pl.program_id(1) % tiles_per_seq) == 0

    hist_scr[0:SUBLANES, :] = jnp.where(first, 0.0, hist_scr[tm:tm + SUBLANES, :])
    carry_all = jnp.where(first, 0.0, hcar_scr[...])

    xb = x_ref[...].astype(bf16)
    hist_scr[SUBLANES:SUBLANES + tm, :] = jnp.dot(xb, wx_ref[...].astype(bf16), preferred_element_type=f32)
    wao_bf_ref[...] = wao_ref[...].astype(bf16)
    wro_bf_ref[...] = wro_ref[...].astype(bf16)
    wo_bf_ref[...] = wo_ref[...].astype(bf16)
    wg = wg_ref[...].astype(bf16)
    g_top = jnp.dot(xb[:tm // 2], wg, preferred_element_type=f32)

    neg_lam = -lam_ref[...]
    half_scale = (-0.5 * LRU_C) * (jnp.maximum(neg_lam, 0.0) + jnp.log1p(jnp.exp(-jnp.abs(neg_lam))))
    half_br = 0.5 * br_ref[...]
    half_bi = 0.5 * bi_ref[...]
    row8 = lax.broadcasted_iota(jnp.int32, (SUBLANES, RNN_BLOCK_DIM), 0)
    for b in range(COL_TILE // RNN_BLOCK_DIM):
        c0 = b * RNN_BLOCK_DIM
        c1 = c0 + RNN_BLOCK_DIM
        x_cur = hist_scr[SUBLANES:SUBLANES + tm, c0:c1]
        x_old = hist_scr[0:SUBLANES, c0:c1]
        acc, acc_old = None, None
        for kk in range(CONV_WIDTH - 1):
            wk = convw_ref[kk:kk + 1, c0:c1]
            acc = wk * x_cur if acc is None else wk * x_cur + acc
            acc_old = wk * x_old if acc_old is None else wk * x_old + acc_old
            acc_old = pltpu.roll(acc_old, 1, axis=0)
            shifted = pltpu.roll(acc, 1, axis=0)
            top = jnp.where(row8 == 0, acc_old, shifted[0:SUBLANES, :])
            acc = jnp.concatenate([top, shifted[SUBLANES:, :]], axis=0)
        xc = convw_ref[CONV_WIDTH - 1:CONV_WIDTH, c0:c1] * x_cur + acc + convb_ref[:, c0:c1]
        w_gates = (0.5 * jnp.concatenate([wr_ref[b], wi_ref[b]], axis=1)).astype(bf16)
        ri = jnp.dot(xc.astype(bf16), w_gates, preferred_element_type=f32)
        if b == 0:
            g = jnp.concatenate([g_top, jnp.dot(xb[tm // 2:], wg, preferred_element_type=f32)], axis=0)
        tanh_r = jnp.tanh(ri[:, :RNN_BLOCK_DIM] + half_br[:, c0:c1])
        tanh_i = jnp.tanh(ri[:, RNN_BLOCK_DIM:] + half_bi[:, c0:c1])
        log_a = tanh_r * half_scale[:, c0:c1] + half_scale[:, c0:c1]
        a = jnp.exp(log_a)
        one_m_a2 = jnp.tanh(log_a) * (-1.0 - a * a)
        mult = jnp.where(one_m_a2 > 0.0, one_m_a2 * lax.rsqrt(one_m_a2), 0.0)
        u = (mult * xc) * (0.5 * tanh_i + 0.5)
        h, carry = _linear_scan(a, u, carry_all[:, c0:c1])
        hcar_scr[:, c0:c1] = carry
        gb = g[:, c0:c1]
        yr_ref[:, c0:c1] = (h * (gb * _sigmoid(gb))).astype(yr_ref.dtype)


def _rnn_projection(x2d, w_in, layer, w_r, w_i, b_r, b_i, lam, convw, convb, wao, wro, wo, *, seq):
    m, k = x2d.shape
    tm = RNN_TM
    nblk = COL_TILE // RNN_BLOCK_DIM
    grid = (RNN_WIDTH // COL_TILE, m // tm)
    gate_w = pl.BlockSpec((None, nblk, RNN_BLOCK_DIM, RNN_BLOCK_DIM), lambda c, i: (layer, c, 0, 0))
    row = pl.BlockSpec((None, 1, COL_TILE), lambda c, i: (layer, 0, c))

    weights = (wao, wro, wo)
    cast_in, cast_out = [], []
    for w in weights:
        nb = w.shape[1] // CAST_ROWS
        assert nb <= grid[0] * grid[1]
        blk = lambda c, i, nb=nb: jnp.minimum(c * grid[1] + i, nb - 1)
        cast_in.append(pl.BlockSpec((None, CAST_ROWS, w.shape[2]), lambda c, i, blk=blk: (layer, blk(c, i), 0)))
        cast_out.append(pl.BlockSpec((CAST_ROWS, w.shape[2]), lambda c, i, blk=blk: (blk(c, i), 0)))

    return pl.pallas_call(
        functools.partial(_rnn_proj_kernel, tiles_per_seq=seq // tm),
        grid=grid,
        in_specs=[pl.BlockSpec((tm, k), lambda c, i: (i, 0)),
                  pl.BlockSpec((None, k, COL_TILE), lambda c, i: (layer, 0, XR0 // COL_TILE + c)),
                  pl.BlockSpec((None, k, COL_TILE), lambda c, i: (layer, 0, GR0 // COL_TILE + c)),
                  gate_w, gate_w, row, row, row,
                  pl.BlockSpec((None, CONV_WIDTH, COL_TILE), lambda c, i: (layer, 0, c)),
                  row] + cast_in,
        out_specs=[pl.BlockSpec((tm, COL_TILE), lambda c, i: (i, c))] + cast_out,
        out_shape=[jax.ShapeDtypeStruct((m, RNN_WIDTH), jnp.bfloat16)]
                  + [jax.ShapeDtypeStruct(w.shape[1:], jnp.bfloat16) for w in weights],
        scratch_shapes=[pltpu.VMEM((SUBLANES + tm, COL_TILE), jnp.float32),
                        pltpu.VMEM((1, COL_TILE), jnp.float32)],
        compiler_params=pltpu.CompilerParams(
            dimension_semantics=("arbitrary", "arbitrary"), vmem_limit_bytes=VMEM_LIMIT),
        name="rnn_projection",
    )(x2d, w_in, w_in, w_r, w_i, b_r, b_i, lam, convw, convb, wao, wro, wo)


def _mixer_kernel(relb_ref, sinks_ref,
                  pa_ref, yr_ref, x_ref, bucket_ref, wao_ref, wro_ref, wo_ref, lng_ref, lnb_ref,
                  out_ref,
                  bias_scr, kx_scr, vx_scr, attn_scr,
                  *, alpha):
    tm = MIX_TM
    nqb = tm // WINDOW
    first_tile = pl.program_id(1) == 0
    f32 = jnp.float32
    bf16 = jnp.bfloat16

    @pl.when((pl.program_id(0) == 0) & first_tile)
    def _():
        bucket = bucket_ref[...]
        for t in range(N_Q_HEADS):
            h = 4 * (t // 4) + 2 * (t % 2) + (t // 2) % 2
            acc = jnp.full((WINDOW, 2 * WINDOW), NEG_INF, f32)
            for b in range(N_BUCKETS):
                acc = jnp.where(bucket == b, relb_ref[b, h], acc)
            bias_scr[t * WINDOW:(t + 1) * WINDOW, :] = acc

    for scr in (kx_scr, vx_scr):
        prev = scr[:, tm:tm + WINDOW, :]
        scr[:, 0:WINDOW, :] = jnp.where(first_tile, jnp.zeros_like(prev), prev)

    u32 = jnp.uint32
    lo_u = lax.broadcasted_iota(jnp.int32, (tm // 2, LANES), 1) < HEAD_DIM
    zero_u = jnp.zeros((tm // 2, LANES), u32)
    for p in range(N_KV_HEADS // 2):
        for src0, scr in ((PA_K0, kx_scr), (PA_V0, vx_scr)):
            t = pltpu.bitcast(pa_ref[:, src0 + p * LANES: src0 + (p + 1) * LANES], u32)
            tr = pltpu.roll(t, HEAD_DIM, axis=1)
            variants = (jnp.where(lo_u, t, zero_u),
                        jnp.where(lo_u, zero_u, tr),
                        jnp.where(lo_u, tr, zero_u),
                        jnp.where(lo_u, zero_u, t))
            for v, val in enumerate(variants):
                scr[4 * p + v, WINDOW:WINDOW + tm, :] = pltpu.bitcast(val, bf16)

    lo_q = lax.broadcasted_iota(jnp.int32, (WINDOW, LANES), 1) < HEAD_DIM
    col = lax.broadcasted_iota(jnp.int32, (1, 2 * WINDOW), 1)
    ones_blk = jnp.ones((2 * WINDOW, LANES), bf16)
    nt_dims = (((1,), (1,)), ((), ()))
    y_rnn_parts = []
    for j in range(nqb):
        r0 = j * WINDOW
        s_parts = []
        for kh in range(N_KV_HEADS):
            qq = jnp.concatenate(
                [pa_ref[r0:r0 + WINDOW, PA_Q0 + (2 * kh + pr) * LANES: PA_Q0 + (2 * kh + pr + 1) * LANES]
                 for pr in range(2)], axis=0) * jnp.asarray(HEAD_DIM ** -0.5, bf16)
            for par in range(2):
                kk = kx_scr[2 * kh + par, r0:r0 + 2 * WINDOW, :]
                s_parts.append(lax.dot_general(qq, kk, nt_dims, preferred_element_type=f32))
        cw = D_MODEL // nqb
        y_rnn_parts.append(jnp.dot(yr_ref[...], wro_ref[:, j * cw:(j + 1) * cw], preferred_element_type=f32))
        s = jnp.concatenate(s_parts, axis=0) + bias_scr[...]
        if j == 0:
            s = s + jnp.where(first_tile & (col < WINDOW), NEG_INF, 0.0).astype(f32)
        m = jnp.max(s, axis=-1, keepdims=True)
        p = jnp.exp(s - m).astype(bf16)
        for kh in range(N_KV_HEADS):
            o_ext = []
            for par in range(2):
                t0 = (2 * kh + par) * 2 * WINDOW
                vv = jnp.concatenate([vx_scr[2 * kh + par, r0:r0 + 2 * WINDOW, :], ones_blk], axis=1)
                o_ext.append(jnp.dot(p[t0:t0 + 2 * WINDOW, :], vv, preferred_element_type=f32))
            for pr in range(2):
                pair = 2 * kh + pr
                rows = slice(pr * WINDOW, (pr + 1) * WINDOW)
                rden = []
                for par in range(2):
                    t0 = (2 * kh + par) * 2 * WINDOW + pr * WINDOW
                    sink_term = jnp.exp(sinks_ref[2 * pair + par] - m[t0:t0 + WINDOW, :])
                    rden.append(1.0 / (o_ext[par][rows, LANES:] + sink_term))
                o = (o_ext[0][rows, :LANES] + o_ext[1][rows, :LANES]) * jnp.where(lo_q, rden[0], rden[1])
                ga = pa_ref[r0:r0 + WINDOW, PA_GA0 + pair * LANES: PA_GA0 + (pair + 1) * LANES].astype(f32)
                attn_scr[r0:r0 + WINDOW, pair * LANES:(pair + 1) * LANES] = (
                    o * (ga * _sigmoid(ga))).astype(bf16)

    y_attn = jnp.dot(attn_scr[...], wao_ref[...], preferred_element_type=f32)
    y_rnn = jnp.concatenate(y_rnn_parts, axis=1)
    mixed = (_sigmoid(pa_ref[:, PA_MA0:PA_MA0 + D_MODEL].astype(f32)) * y_attn
             + _sigmoid(pa_ref[:, PA_MR0:PA_MR0 + D_MODEL].astype(f32)) * y_rnn).astype(bf16)
    zs = []
    row_sum = jnp.zeros((tm, 1), f32)
    for c in range(D_MODEL // COL_TILE):
        k0 = c * COL_TILE
        z = alpha * x_ref[:, k0:k0 + COL_TILE] + jnp.dot(mixed, wo_ref[:, k0:k0 + COL_TILE],
                                                         preferred_element_type=f32)
        row_sum = row_sum + jnp.sum(z, axis=-1, keepdims=True)
        zs.append(z)
    zc = jnp.concatenate(zs, axis=1) - row_sum * (1.0 / D_MODEL)
    var = jnp.mean(zc * zc, axis=-1, keepdims=True)
    out_ref[...] = zc * lax.rsqrt(var + LN_EPS) * lng_ref[...] + lnb_ref[...]


def _const_spec(shape):
    nd = len(shape)
    return pl.BlockSpec(shape, lambda b, s: (0,) * nd, pipeline_mode=pl.Buffered(1))


def _mixer(pa, yr, x2d, bucket, relb, sinks, wao, wro, wo, lng, lnb, *, batch, seq, alpha):
    tm = MIX_TM
    nst = seq // tm
    row_map = lambda b, s: (b * nst + s, 0)
    smem = pl.BlockSpec(memory_space=pltpu.SMEM)
    in_specs = [
        smem, smem,
        pl.BlockSpec((tm, PA_WIDTH), row_map),
        pl.BlockSpec((tm, RNN_WIDTH), row_map),
        pl.BlockSpec((tm, D_MODEL), row_map),
        _const_spec(bucket.shape), _const_spec(wao.shape), _const_spec(wro.shape), _const_spec(wo.shape),
        _const_spec(lng.shape), _const_spec(lnb.shape),
    ]
    scratch = [
        pltpu.VMEM((N_Q_HEADS * WINDOW, 2 * WINDOW), jnp.float32),
        pltpu.VMEM((8, WINDOW + tm, LANES), jnp.bfloat16),
        pltpu.VMEM((8, WINDOW + tm, LANES), jnp.bfloat16),
        pltpu.VMEM((tm, ATTN_WIDTH), jnp.bfloat16),
    ]
    return pl.pallas_call(
        functools.partial(_mixer_kernel, alpha=alpha),
        grid=(batch, nst),
        in_specs=in_specs,
        out_specs=pl.BlockSpec((tm, D_MODEL), row_map),
        out_shape=jax.ShapeDtypeStruct((batch * seq, D_MODEL), jnp.float32),
        scratch_shapes=scratch,
        compiler_params=pltpu.CompilerParams(
            dimension_semantics=("arbitrary", "arbitrary"), vmem_limit_bytes=VMEM_LIMIT),
        name="mixer",
    )(relb, sinks, pa, yr, x2d, bucket, wao, wro, wo, lng, lnb)


def kernel(x, w_in, conv_w, conv_b, w_r, b_r, w_i, b_i, lru_lambda, sinks, w_attn_out, w_rnn_out, w_out,
           ln_g, ln_b, rel_bias):
    batch, seq, d = x.shape
    depth = w_in.shape[0]
    alpha = (2.0 * depth) ** 0.25
    bucket = jnp.asarray(_t5_bucket_table())
    x2d = x.reshape(batch * seq, d)
    as_rows = lambda v: v.reshape(depth, 1, v.shape[-1])
    b_r, b_i, lam, conv_b = as_rows(b_r), as_rows(b_i), as_rows(lru_lambda), as_rows(conv_b)
    for l in range(depth):
        pa = _plain_projection(x2d, w_in, l)
        yr, wao, wro, wo = _rnn_projection(x2d, w_in, l, w_r, w_i, b_r, b_i, lam, conv_w, conv_b,
                                           w_attn_out, w_rnn_out, w_out, seq=seq)
        x2d = _mixer(pa, yr, x2d, bucket, rel_bias, sinks[l], wao, wro, wo,
                     ln_g[l][None, :], ln_b[l][None, :], batch=batch, seq=seq, alpha=alpha)
    return x2d.reshape(batch, seq, d)
```

```python
import functools
import math

import numpy as np
import jax
import jax.numpy as jnp
from jax import lax
from jax.experimental import pallas as pl
from jax.experimental.pallas import tpu as pltpu

D_MODEL = 2048
N_Q_HEADS = 16
N_KV_HEADS = 4
HEAD_DIM = 64
ATTN_WIDTH = N_Q_HEADS * HEAD_DIM
KV_WIDTH = N_KV_HEADS * HEAD_DIM
WINDOW = 128
N_BUCKETS = 32
MAX_DISTANCE = 128
RNN_WIDTH = D_MODEL
RNN_BLOCKS = 16
RNN_BLOCK_DIM = RNN_WIDTH // RNN_BLOCKS
CONV_WIDTH = 4
LRU_C = 8.0
LN_EPS = 1e-5
NEG_INF = -1e30

Q0 = 0
K0 = Q0 + ATTN_WIDTH
V0 = K0 + KV_WIDTH
GA0 = V0 + KV_WIDTH
XR0 = GA0 + ATTN_WIDTH
GR0 = XR0 + RNN_WIDTH
MA0 = GR0 + RNN_WIDTH
MR0 = MA0 + D_MODEL
IN_WIDTH = MR0 + D_MODEL

LANES = 128
SUBLANES = 8
VMEM_LIMIT = 60 * 1024 * 1024

COL_TILE = 512
PLAIN_TM = 1024
RNN_TM = 1024
MIX_TM = 256
CAST_ROWS = 64

PA_Q0 = 0
PA_K0 = PA_Q0 + ATTN_WIDTH
PA_V0 = PA_K0 + KV_WIDTH
PA_GA0 = PA_V0 + KV_WIDTH
PA_MA0 = PA_GA0 + ATTN_WIDTH
PA_MR0 = PA_MA0 + D_MODEL
PA_WIDTH = PA_MR0 + D_MODEL
PLAIN_TN = PA_WIDTH // 2
MXU_DIM = 256


def _t5_bucket_table():
    qi = np.arange(WINDOW)[:, None]
    kj = np.arange(2 * WINDOW)[None, :]
    dist = qi + WINDOW - kj
    max_exact = N_BUCKETS // 2
    n = np.maximum(dist, 0)
    nf = np.maximum(n, max_exact).astype(np.float32)
    large = max_exact + (np.log(nf / np.float32(max_exact)) / np.float32(math.log(MAX_DISTANCE / max_exact))
                         * np.float32(N_BUCKETS - max_exact)).astype(np.int32)
    large = np.minimum(large, N_BUCKETS - 1)
    bucket = np.where(n < max_exact, n, large)
    in_window = (dist >= 0) & (dist < WINDOW)
    return np.where(in_window, bucket, -1).astype(np.int32)


def _sigmoid(x):
    return 0.5 * jnp.tanh(0.5 * x) + 0.5


def _plain_proj_kernel(x_ref, w_ref, pa_ref):
    xb = x_ref[...].astype(jnp.bfloat16)
    split = (PLAIN_TN // MXU_DIM + 1) // 2 * MXU_DIM
    for c0, c1 in ((0, split), (split, PLAIN_TN)):
        pa_ref[:, c0:c1] = jnp.dot(xb, w_ref[:, c0:c1], preferred_element_type=jnp.float32).astype(pa_ref.dtype)


def _plain_projection(x2d, w_plain):
    m, k = x2d.shape
    tm = PLAIN_TM
    n_tiles = w_plain.shape[0]
    return pl.pallas_call(
        _plain_proj_kernel,
        grid=(n_tiles, m // tm),
        in_specs=[pl.BlockSpec((tm, k), lambda n, i: (i, 0)),
                  pl.BlockSpec((None, k, PLAIN_TN), lambda n, i: (n, 0, 0), pipeline_mode=pl.Buffered(1))],
        out_specs=pl.BlockSpec((tm, PLAIN_TN), lambda n, i: (i, n)),
        out_shape=jax.ShapeDtypeStruct((m, PA_WIDTH), jnp.bfloat16),
        compiler_params=pltpu.CompilerParams(
            dimension_semantics=("arbitrary", "arbitrary"), vmem_limit_bytes=VMEM_LIMIT),
        name="plain_projection",
    )(x2d, w_plain)


def _linear_scan(a, u, carry):
    tm = a.shape[0]
    groups = tm // SUBLANES
    a3 = a.reshape(groups, SUBLANES, LANES)
    u3 = u.reshape(groups, SUBLANES, LANES)
    row = lax.broadcasted_iota(jnp.int32, (groups, SUBLANES, LANES), 1)
    for d in (1, 2, 4):
        keep = row >= d
        a_sh = pltpu.roll(a3, d, axis=1)
        u_sh = pltpu.roll(u3, d, axis=1)
        u3 = jnp.where(keep, a3 * u_sh + u3, u3)
        a3 = jnp.where(keep, a3 * a_sh, a3)
    hs = []
    for g in range(groups):
        hg = u3[g] + a3[g] * carry
        hs.append(hg)
        carry = hg[SUBLANES - 1:SUBLANES, :]
    return jnp.concatenate(hs, axis=0), carry


def _rnn_proj_kernel(x_ref, wx_ref, wg_ref, wr_ref, wi_ref, br_ref, bi_ref, lam_ref, convw_ref, convb_ref,
                     wao_ref, wro_ref, wo_ref, wpa_ref, wpm_ref,
                     yr_ref, wao_bf_ref, wro_bf_ref, wo_bf_ref, wp_bf_ref, hist_scr, hcar_scr, *, tiles_per_seq):
    tm = RNN_TM
    f32 = jnp.float32
    bf16 = jnp.bfloat16
    first = (pl.program_id(1) % tiles_per_seq) == 0

    hist_scr[0:SUBLANES, :] = jnp.where(first, 0.0, hist_scr[tm:tm + SUBLANES, :])
    carry_all = jnp.where(first, 0.0, hcar_scr[...])

    xb = x_ref[...].astype(bf16)
    hist_scr[SUBLANES:SUBLANES + tm, :] = jnp.dot(xb, wx_ref[...].astype(bf16), preferred_element_type=f32)
    wao_bf_ref[...] = wao_ref[...].astype(bf16)
    wro_bf_ref[...] = wro_ref[...].astype(bf16)
    wo_bf_ref[...] = wo_ref[...].astype(bf16)
    w_plain = jnp.concatenate([wpa_ref[0], wpm_ref[0]], axis=1).astype(bf16)
    for t in range(PA_WIDTH // PLAIN_TN):
        wp_bf_ref[t] = w_plain[:, t * PLAIN_TN:(t + 1) * PLAIN_TN]
    wg = wg_ref[...].astype(bf16)
    g_top = jnp.dot(xb[:tm // 2], wg, preferred_element_type=f32)

    neg_lam = -lam_ref[...]
    half_scale = (-0.5 * LRU_C) * (jnp.maximum(neg_lam, 0.0) + jnp.log1p(jnp.exp(-jnp.abs(neg_lam))))
    half_br = 0.5 * br_ref[...]
    half_bi = 0.5 * bi_ref[...]
    row8 = lax.broadcasted_iota(jnp.int32, (SUBLANES, RNN_BLOCK_DIM), 0)
    for b in range(COL_TILE // RNN_BLOCK_DIM):
        c0 = b * RNN_BLOCK_DIM
        c1 = c0 + RNN_BLOCK_DIM
        x_cur = hist_scr[SUBLANES:SUBLANES + tm, c0:c1]
        x_old = hist_scr[0:SUBLANES, c0:c1]
        acc, acc_old = None, None
        for kk in range(CONV_WIDTH - 1):
            wk = convw_ref[kk:kk + 1, c0:c1]
            acc = wk * x_cur if acc is None else wk * x_cur + acc
            acc_old = wk * x_old if acc_old is None else wk * x_old + acc_old
            acc_old = pltpu.roll(acc_old, 1, axis=0)
            shifted = pltpu.roll(acc, 1, axis=0)
            top = jnp.where(row8 == 0, acc_old, shifted[0:SUBLANES, :])
            acc = jnp.concatenate([top, shifted[SUBLANES:, :]], axis=0)
        xc = convw_ref[CONV_WIDTH - 1:CONV_WIDTH, c0:c1] * x_cur + acc + convb_ref[:, c0:c1]
        w_gates = (0.5 * jnp.concatenate([wr_ref[b], wi_ref[b]], axis=1)).astype(bf16)
        ri = jnp.dot(xc.astype(bf16), w_gates, preferred_element_type=f32)
        if b == 0:
            g = jnp.concatenate([g_top, jnp.dot(xb[tm // 2:], wg, preferred_element_type=f32)], axis=0)
        tanh_r = jnp.tanh(ri[:, :RNN_BLOCK_DIM] + half_br[:, c0:c1])
        tanh_i = jnp.tanh(ri[:, RNN_BLOCK_DIM:] + half_bi[:, c0:c1])
        log_a = tanh_r * half_scale[:, c0:c1] + half_scale[:, c0:c1]
        a = jnp.exp(log_a)
        one_m_a2 = jnp.tanh(log_a) * (-1.0 - a * a)
        mult = jnp.where(one_m_a2 > 0.0, one_m_a2 * lax.rsqrt(one_m_a2), 0.0)
        u = (mult * xc) * (0.5 * tanh_i + 0.5)
        h, carry = _linear_scan(a, u, carry_all[:, c0:c1])
        hcar_scr[:, c0:c1] = carry
        gb = g[:, c0:c1]
        yr_ref[:, c0:c1] = (h * (gb * _sigmoid(gb))).astype(yr_ref.dtype)


def _rnn_projection(x2d, w_in, layer, w_r, w_i, b_r, b_i, lam, convw, convb, wao, wro, wo, *, seq):
    m, k = x2d.shape
    tm = RNN_TM
    nblk = COL_TILE // RNN_BLOCK_DIM
    grid = (RNN_WIDTH // COL_TILE, m // tm)
    gate_w = pl.BlockSpec((None, nblk, RNN_BLOCK_DIM, RNN_BLOCK_DIM), lambda c, i: (layer, c, 0, 0))
    row = pl.BlockSpec((None, 1, COL_TILE), lambda c, i: (layer, 0, c))

    weights = (wao, wro, wo)
    cast_in, cast_out = [], []
    for w in weights:
        nb = w.shape[1] // CAST_ROWS
        assert nb <= grid[0] * grid[1]
        blk = lambda c, i, nb=nb: jnp.minimum(c * grid[1] + i, nb - 1)
        cast_in.append(pl.BlockSpec((None, CAST_ROWS, w.shape[2]), lambda c, i, blk=blk: (layer, blk(c, i), 0)))
        cast_out.append(pl.BlockSpec((CAST_ROWS, w.shape[2]), lambda c, i, blk=blk: (blk(c, i), 0)))
    w_rows = k // (grid[0] * grid[1])
    for col0, width in ((Q0, XR0 - Q0), (MA0, IN_WIDTH - MA0)):
        cast_in.append(pl.BlockSpec(
            (pl.Element(1), pl.Element(w_rows), pl.Element(width)),
            lambda c, i, col0=col0: (layer, pl.multiple_of((c * grid[1] + i) * w_rows, SUBLANES), col0)))
    n_plain_tiles = PA_WIDTH // PLAIN_TN
    cast_out.append(pl.BlockSpec((n_plain_tiles, w_rows, PLAIN_TN), lambda c, i: (0, c * grid[1] + i, 0)))

    return pl.pallas_call(
        functools.partial(_rnn_proj_kernel, tiles_per_seq=seq // tm),
        grid=grid,
        in_specs=[pl.BlockSpec((tm, k), lambda c, i: (i, 0)),
                  pl.BlockSpec((None, k, COL_TILE), lambda c, i: (layer, 0, XR0 // COL_TILE + c)),
                  pl.BlockSpec((None, k, COL_TILE), lambda c, i: (layer, 0, GR0 // COL_TILE + c)),
                  gate_w, gate_w, row, row, row,
                  pl.BlockSpec((None, CONV_WIDTH, COL_TILE), lambda c, i: (layer, 0, c)),
                  row] + cast_in,
        out_specs=[pl.BlockSpec((tm, COL_TILE), lambda c, i: (i, c))] + cast_out,
        out_shape=[jax.ShapeDtypeStruct((m, RNN_WIDTH), jnp.bfloat16)]
                  + [jax.ShapeDtypeStruct(w.shape[1:], jnp.bfloat16) for w in weights]
                  + [jax.ShapeDtypeStruct((n_plain_tiles, k, PLAIN_TN), jnp.bfloat16)],
        scratch_shapes=[pltpu.VMEM((SUBLANES + tm, COL_TILE), jnp.float32),
                        pltpu.VMEM((1, COL_TILE), jnp.float32)],
        compiler_params=pltpu.CompilerParams(
            dimension_semantics=("arbitrary", "arbitrary"), vmem_limit_bytes=VMEM_LIMIT),
        name="rnn_projection",
    )(x2d, w_in, w_in, w_r, w_i, b_r, b_i, lam, convw, convb, wao, wro, wo, w_in, w_in)


def _mixer_kernel(relb_ref, sinks_ref,
                  pa_ref, yr_ref, x_ref, bucket_ref, wao_ref, wro_ref, wo_ref, lng_ref, lnb_ref,
                  out_ref,
                  bias_scr, kx_scr, vx_scr, attn_scr,
                  *, alpha):
    tm = MIX_TM
    nqb = tm // WINDOW
    first_tile = pl.program_id(1) == 0
    f32 = jnp.float32
    bf16 = jnp.bfloat16

    @pl.when((pl.program_id(0) == 0) & first_tile)
    def _():
        bucket = bucket_ref[...]
        for t in range(N_Q_HEADS):
            h = 4 * (t // 4) + 2 * (t % 2) + (t // 2) % 2
            acc = jnp.full((WINDOW, 2 * WINDOW), NEG_INF, f32)
            for b in range(N_BUCKETS):
                acc = jnp.where(bucket == b, relb_ref[b, h], acc)
            bias_scr[t * WINDOW:(t + 1) * WINDOW, :] = acc

    for scr in (kx_scr, vx_scr):
        prev = scr[:, tm:tm + WINDOW, :]
        scr[:, 0:WINDOW, :] = jnp.where(first_tile, jnp.zeros_like(prev), prev)

    u32 = jnp.uint32
    lo_u = lax.broadcasted_iota(jnp.int32, (tm // 2, LANES), 1) < HEAD_DIM
    zero_u = jnp.zeros((tm // 2, LANES), u32)
    for p in range(N_KV_HEADS // 2):
        for src0, scr in ((PA_K0, kx_scr), (PA_V0, vx_scr)):
            t = pltpu.bitcast(pa_ref[:, src0 + p * LANES: src0 + (p + 1) * LANES], u32)
            tr = pltpu.roll(t, HEAD_DIM, axis=1)
            variants = (jnp.where(lo_u, t, zero_u),
                        jnp.where(lo_u, zero_u, tr),
                        jnp.where(lo_u, tr, zero_u),
                        jnp.where(lo_u, zero_u, t))
            for v, val in enumerate(variants):
                scr[4 * p + v, WINDOW:WINDOW + tm, :] = pltpu.bitcast(val, bf16)

    lo_q = lax.broadcasted_iota(jnp.int32, (WINDOW, LANES), 1) < HEAD_DIM
    col = lax.broadcasted_iota(jnp.int32, (1, 2 * WINDOW), 1)
    ones_blk = jnp.ones((2 * WINDOW, LANES), bf16)
    nt_dims = (((1,), (1,)), ((), ()))
    y_rnn_parts = []
    for j in range(nqb):
        r0 = j * WINDOW
        s_parts = []
        for kh in range(N_KV_HEADS):
            qq = jnp.concatenate(
                [pa_ref[r0:r0 + WINDOW, PA_Q0 + (2 * kh + pr) * LANES: PA_Q0 + (2 * kh + pr + 1) * LANES]
                 for pr in range(2)], axis=0) * jnp.asarray(HEAD_DIM ** -0.5, bf16)
            for par in range(2):
                kk = kx_scr[2 * kh + par, r0:r0 + 2 * WINDOW, :]
                s_parts.append(lax.dot_general(qq, kk, nt_dims, preferred_element_type=f32))
        cw = D_MODEL // nqb
        y_rnn_parts.append(jnp.dot(yr_ref[...], wro_ref[:, j * cw:(j + 1) * cw], preferred_element_type=f32))
        s = jnp.concatenate(s_parts, axis=0) + bias_scr[...]
        if j == 0:
            s = s + jnp.where(first_tile & (col < WINDOW), NEG_INF, 0.0).astype(f32)
        m = jnp.max(s, axis=-1, keepdims=True)
        p = jnp.exp(s - m).astype(bf16)
        for kh in range(N_KV_HEADS):
            o_ext = []
            for par in range(2):
                t0 = (2 * kh + par) * 2 * WINDOW
                vv = jnp.concatenate([vx_scr[2 * kh + par, r0:r0 + 2 * WINDOW, :], ones_blk], axis=1)
                o_ext.append(jnp.dot(p[t0:t0 + 2 * WINDOW, :], vv, preferred_element_type=f32))
            for pr in range(2):
                pair = 2 * kh + pr
                rows = slice(pr * WINDOW, (pr + 1) * WINDOW)
                rden = []
                for par in range(2):
                    t0 = (2 * kh + par) * 2 * WINDOW + pr * WINDOW
                    sink_term = jnp.exp(sinks_ref[2 * pair + par] - m[t0:t0 + WINDOW, :])
                    rden.append(1.0 / (o_ext[par][rows, LANES:] + sink_term))
                o = (o_ext[0][rows, :LANES] + o_ext[1][rows, :LANES]) * jnp.where(lo_q, rden[0], rden[1])
                ga = pa_ref[r0:r0 + WINDOW, PA_GA0 + pair * LANES: PA_GA0 + (pair + 1) * LANES].astype(f32)
                attn_scr[r0:r0 + WINDOW, pair * LANES:(pair + 1) * LANES] = (
                    o * (ga * _sigmoid(ga))).astype(bf16)

    y_attn = jnp.dot(attn_scr[...], wao_ref[...], preferred_element_type=f32)
    y_rnn = jnp.concatenate(y_rnn_parts, axis=1)
    mixed = (_sigmoid(pa_ref[:, PA_MA0:PA_MA0 + D_MODEL].astype(f32)) * y_attn
             + _sigmoid(pa_ref[:, PA_MR0:PA_MR0 + D_MODEL].astype(f32)) * y_rnn).astype(bf16)
    zs = []
    row_sum = jnp.zeros((tm, 1), f32)
    for c in range(D_MODEL // COL_TILE):
        k0 = c * COL_TILE
        z = alpha * x_ref[:, k0:k0 + COL_TILE] + jnp.dot(mixed, wo_ref[:, k0:k0 + COL_TILE],
                                                         preferred_element_type=f32)
        row_sum = row_sum + jnp.sum(z, axis=-1, keepdims=True)
        zs.append(z)
    zc = jnp.concatenate(zs, axis=1) - row_sum * (1.0 / D_MODEL)
    var = jnp.mean(zc * zc, axis=-1, keepdims=True)
    out_ref[...] = zc * lax.rsqrt(var + LN_EPS) * lng_ref[...] + lnb_ref[...]


def _const_spec(shape):
    nd = len(shape)
    return pl.BlockSpec(shape, lambda b, s: (0,) * nd, pipeline_mode=pl.Buffered(1))


def _mixer(pa, yr, x2d, bucket, relb, sinks, wao, wro, wo, lng, lnb, *, batch, seq, alpha):
    tm = MIX_TM
    nst = seq // tm
    row_map = lambda b, s: (b * nst + s, 0)
    smem = pl.BlockSpec(memory_space=pltpu.SMEM)
    in_specs = [
        smem, smem,
        pl.BlockSpec((tm, PA_WIDTH), row_map),
        pl.BlockSpec((tm, RNN_WIDTH), row_map),
        pl.BlockSpec((tm, D_MODEL), row_map),
        _const_spec(bucket.shape), _const_spec(wao.shape), _const_spec(wro.shape), _const_spec(wo.shape),
        _const_spec(lng.shape), _const_spec(lnb.shape),
    ]
    scratch = [
        pltpu.VMEM((N_Q_HEADS * WINDOW, 2 * WINDOW), jnp.float32),
        pltpu.VMEM((8, WINDOW + tm, LANES), jnp.bfloat16),
        pltpu.VMEM((8, WINDOW + tm, LANES), jnp.bfloat16),
        pltpu.VMEM((tm, ATTN_WIDTH), jnp.bfloat16),
    ]
    return pl.pallas_call(
        functools.partial(_mixer_kernel, alpha=alpha),
        grid=(batch, nst),
        in_specs=in_specs,
        out_specs=pl.BlockSpec((tm, D_MODEL), row_map),
        out_shape=jax.ShapeDtypeStruct((batch * seq, D_MODEL), jnp.float32),
        scratch_shapes=scratch,
        compiler_params=pltpu.CompilerParams(
            dimension_semantics=("arbitrary", "arbitrary"), vmem_limit_bytes=VMEM_LIMIT),
        name="mixer",
    )(relb, sinks, pa, yr, x2d, bucket, wao, wro, wo, lng, lnb)


def kernel(x, w_in, conv_w, conv_b, w_r, b_r, w_i, b_i, lru_lambda, sinks, w_attn_out, w_rnn_out, w_out,
           ln_g, ln_b, rel_bias):
    batch, seq, d = x.shape
    depth = w_in.shape[0]
    alpha = (2.0 * depth) ** 0.25
    bucket = jnp.asarray(_t5_bucket_table())
    x2d = x.reshape(batch * seq, d)
    as_rows = lambda v: v.reshape(depth, 1, v.shape[-1])
    b_r, b_i, lam, conv_b = as_rows(b_r), as_rows(b_i), as_rows(lru_lambda), as_rows(conv_b)
    for l in range(depth):
        yr, wao, wro, wo, w_plain = _rnn_projection(x2d, w_in, l, w_r, w_i, b_r, b_i, lam, conv_w, conv_b,
                                                    w_attn_out, w_rnn_out, w_out, seq=seq)
        pa = _plain_projection(x2d, w_plain)
        x2d = _mixer(pa, yr, x2d, bucket, rel_bias, sinks[l], wao, wro, wo,
                     ln_g[l][None, :], ln_b[l][None, :], batch=batch, seq=seq, alpha=alpha)
    return x2d.reshape(batch, seq, d)
```

```python
import functools
import math

import numpy as np
import jax
import jax.numpy as jnp
from jax import lax
from jax.experimental import pallas as pl
from jax.experimental.pallas import tpu as pltpu

D_MODEL = 2048
N_Q_HEADS = 16
N_KV_HEADS = 4
HEAD_DIM = 64
ATTN_WIDTH = N_Q_HEADS * HEAD_DIM
KV_WIDTH = N_KV_HEADS * HEAD_DIM
WINDOW = 128
N_BUCKETS = 32
MAX_DISTANCE = 128
RNN_WIDTH = D_MODEL
RNN_BLOCKS = 16
RNN_BLOCK_DIM = RNN_WIDTH // RNN_BLOCKS
CONV_WIDTH = 4
LRU_C = 8.0
LN_EPS = 1e-5
NEG_INF = -1e30

Q0 = 0
K0 = Q0 + ATTN_WIDTH
V0 = K0 + KV_WIDTH
GA0 = V0 + KV_WIDTH
XR0 = GA0 + ATTN_WIDTH
GR0 = XR0 + RNN_WIDTH
MA0 = GR0 + RNN_WIDTH
MR0 = MA0 + D_MODEL
IN_WIDTH = MR0 + D_MODEL

LANES = 128
SUBLANES = 8
VMEM_LIMIT = 60 * 1024 * 1024

COL_TILE = 512
PLAIN_TM = 1024
RNN_TM = 1024
MIX_TM = 256
CAST_ROWS = 64

PA_Q0 = 0
PA_K0 = PA_Q0 + ATTN_WIDTH
PA_V0 = PA_K0 + KV_WIDTH
PA_GA0 = PA_V0 + KV_WIDTH
PA_MA0 = PA_GA0 + ATTN_WIDTH
PA_MR0 = PA_MA0 + D_MODEL
PA_WIDTH = PA_MR0 + D_MODEL
PLAIN_TN = PA_WIDTH // 2
MXU_DIM = 256
N_KV_VARIANTS = 2 * N_KV_HEADS


def _t5_bucket_table():
    qi = np.arange(WINDOW)[:, None]
    kj = np.arange(2 * WINDOW)[None, :]
    dist = qi + WINDOW - kj
    max_exact = N_BUCKETS // 2
    n = np.maximum(dist, 0)
    nf = np.maximum(n, max_exact).astype(np.float32)
    large = max_exact + (np.log(nf / np.float32(max_exact)) / np.float32(math.log(MAX_DISTANCE / max_exact))
                         * np.float32(N_BUCKETS - max_exact)).astype(np.int32)
    large = np.minimum(large, N_BUCKETS - 1)
    bucket = np.where(n < max_exact, n, large)
    in_window = (dist >= 0) & (dist < WINDOW)
    return np.where(in_window, bucket, -1).astype(np.int32)


def _sigmoid(x):
    return 0.5 * jnp.tanh(0.5 * x) + 0.5


def _plain_proj_kernel(x_ref, w_ref, pa_ref):
    xb = x_ref[...].astype(jnp.bfloat16)
    split = (PLAIN_TN // MXU_DIM + 1) // 2 * MXU_DIM
    for c0, c1 in ((0, split), (split, PLAIN_TN)):
        pa_ref[:, c0:c1] = jnp.dot(xb, w_ref[:, c0:c1], preferred_element_type=jnp.float32).astype(pa_ref.dtype)


def _plain_projection(x2d, w_plain):
    m, k = x2d.shape
    tm = PLAIN_TM
    n_tiles = w_plain.shape[0]
    return pl.pallas_call(
        _plain_proj_kernel,
        grid=(n_tiles, m // tm),
        in_specs=[pl.BlockSpec((tm, k), lambda n, i: (i, 0)),
                  pl.BlockSpec((None, k, PLAIN_TN), lambda n, i: (n, 0, 0), pipeline_mode=pl.Buffered(1))],
        out_specs=pl.BlockSpec((tm, PLAIN_TN), lambda n, i: (i, n)),
        out_shape=jax.ShapeDtypeStruct((m, PA_WIDTH), jnp.bfloat16),
        compiler_params=pltpu.CompilerParams(
            dimension_semantics=("arbitrary", "arbitrary"), vmem_limit_bytes=VMEM_LIMIT),
        name="plain_projection",
    )(x2d, w_plain)


def _linear_scan(a, u, carry):
    tm = a.shape[0]
    groups = tm // SUBLANES
    a3 = a.reshape(groups, SUBLANES, LANES)
    u3 = u.reshape(groups, SUBLANES, LANES)
    row = lax.broadcasted_iota(jnp.int32, (groups, SUBLANES, LANES), 1)
    for d in (1, 2, 4):
        keep = row >= d
        a_sh = pltpu.roll(a3, d, axis=1)
        u_sh = pltpu.roll(u3, d, axis=1)
        u3 = jnp.where(keep, a3 * u_sh + u3, u3)
        a3 = jnp.where(keep, a3 * a_sh, a3)
    hs = []
    for g in range(groups):
        hg = u3[g] + a3[g] * carry
        hs.append(hg)
        carry = hg[SUBLANES - 1:SUBLANES, :]
    return jnp.concatenate(hs, axis=0), carry


def _rnn_proj_kernel(x_ref, wx_ref, wg_ref, wr_ref, wi_ref, br_ref, bi_ref, lam_ref, convw_ref, convb_ref,
                     wao_ref, wro_ref, wo_ref, wpa_ref, wpm_ref,
                     yr_ref, wao_bf_ref, wro_bf_ref, wo_bf_ref, wp_bf_ref, hist_scr, hcar_scr, *, tiles_per_seq):
    tm = RNN_TM
    f32 = jnp.float32
    bf16 = jnp.bfloat16
    first = (pl.program_id(1) % tiles_per_seq) == 0

    hist_scr[0:SUBLANES, :] = jnp.where(first, 0.0, hist_scr[tm:tm + SUBLANES, :])
    carry_all = jnp.where(first, 0.0, hcar_scr[...])

    xb = x_ref[...].astype(bf16)
    hist_scr[SUBLANES:SUBLANES + tm, :] = jnp.dot(xb, wx_ref[...].astype(bf16), preferred_element_type=f32)
    wao_bf_ref[...] = wao_ref[...].astype(bf16)
    wro_bf_ref[...] = wro_ref[...].astype(bf16)
    wo_bf_ref[...] = wo_ref[...].astype(bf16)
    w_plain = jnp.concatenate([wpa_ref[0], wpm_ref[0]], axis=1).astype(bf16)
    for t in range(PA_WIDTH // PLAIN_TN):
        wp_bf_ref[t] = w_plain[:, t * PLAIN_TN:(t + 1) * PLAIN_TN]
    wg = wg_ref[...].astype(bf16)
    g_top = jnp.dot(xb[:tm // 2], wg, preferred_element_type=f32)

    neg_lam = -lam_ref[...]
    half_scale = (-0.5 * LRU_C) * (jnp.maximum(neg_lam, 0.0) + jnp.log1p(jnp.exp(-jnp.abs(neg_lam))))
    half_br = 0.5 * br_ref[...]
    half_bi = 0.5 * bi_ref[...]
    row8 = lax.broadcasted_iota(jnp.int32, (SUBLANES, RNN_BLOCK_DIM), 0)
    for b in range(COL_TILE // RNN_BLOCK_DIM):
        c0 = b * RNN_BLOCK_DIM
        c1 = c0 + RNN_BLOCK_DIM
        x_cur = hist_scr[SUBLANES:SUBLANES + tm, c0:c1]
        x_old = hist_scr[0:SUBLANES, c0:c1]
        acc, acc_old = None, None
        for kk in range(CONV_WIDTH - 1):
            wk = convw_ref[kk:kk + 1, c0:c1]
            acc = wk * x_cur if acc is None else wk * x_cur + acc
            acc_old = wk * x_old if acc_old is None else wk * x_old + acc_old
            acc_old = pltpu.roll(acc_old, 1, axis=0)
            shifted = pltpu.roll(acc, 1, axis=0)
            top = jnp.where(row8 == 0, acc_old, shifted[0:SUBLANES, :])
            acc = jnp.concatenate([top, shifted[SUBLANES:, :]], axis=0)
        xc = convw_ref[CONV_WIDTH - 1:CONV_WIDTH, c0:c1] * x_cur + acc + convb_ref[:, c0:c1]
        w_gates = (0.5 * jnp.concatenate([wr_ref[b], wi_ref[b]], axis=1)).astype(bf16)
        ri = jnp.dot(xc.astype(bf16), w_gates, preferred_element_type=f32)
        if b == 0:
            g = jnp.concatenate([g_top, jnp.dot(xb[tm // 2:], wg, preferred_element_type=f32)], axis=0)
        tanh_r = jnp.tanh(ri[:, :RNN_BLOCK_DIM] + half_br[:, c0:c1])
        tanh_i = jnp.tanh(ri[:, RNN_BLOCK_DIM:] + half_bi[:, c0:c1])
        log_a = tanh_r * half_scale[:, c0:c1] + half_scale[:, c0:c1]
        a = jnp.exp(log_a)
        one_m_a2 = jnp.tanh(log_a) * (-1.0 - a * a)
        mult = jnp.where(one_m_a2 > 0.0, one_m_a2 * lax.rsqrt(one_m_a2), 0.0)
        u = (mult * xc) * (0.5 * tanh_i + 0.5)
        h, carry = _linear_scan(a, u, carry_all[:, c0:c1])
        hcar_scr[:, c0:c1] = carry
        gb = g[:, c0:c1]
        yr_ref[:, c0:c1] = (h * (gb * _sigmoid(gb))).astype(yr_ref.dtype)


def _rnn_projection(x2d, w_in, layer, w_r, w_i, b_r, b_i, lam, convw, convb, wao, wro, wo, *, seq):
    m, k = x2d.shape
    tm = RNN_TM
    nblk = COL_TILE // RNN_BLOCK_DIM
    grid = (RNN_WIDTH // COL_TILE, m // tm)
    gate_w = pl.BlockSpec((None, nblk, RNN_BLOCK_DIM, RNN_BLOCK_DIM), lambda c, i: (layer, c, 0, 0))
    row = pl.BlockSpec((None, 1, COL_TILE), lambda c, i: (layer, 0, c))

    weights = (wao, wro, wo)
    cast_in, cast_out = [], []
    for w in weights:
        nb = w.shape[1] // CAST_ROWS
        assert nb <= grid[0] * grid[1]
        blk = lambda c, i, nb=nb: jnp.minimum(c * grid[1] + i, nb - 1)
        cast_in.append(pl.BlockSpec((None, CAST_ROWS, w.shape[2]), lambda c, i, blk=blk: (layer, blk(c, i), 0)))
        cast_out.append(pl.BlockSpec((CAST_ROWS, w.shape[2]), lambda c, i, blk=blk: (blk(c, i), 0)))
    w_rows = k // (grid[0] * grid[1])
    for col0, width in ((Q0, XR0 - Q0), (MA0, IN_WIDTH - MA0)):
        cast_in.append(pl.BlockSpec(
            (pl.Element(1), pl.Element(w_rows), pl.Element(width)),
            lambda c, i, col0=col0: (layer, pl.multiple_of((c * grid[1] + i) * w_rows, SUBLANES), col0)))
    n_plain_tiles = PA_WIDTH // PLAIN_TN
    cast_out.append(pl.BlockSpec((n_plain_tiles, w_rows, PLAIN_TN), lambda c, i: (0, c * grid[1] + i, 0)))

    return pl.pallas_call(
        functools.partial(_rnn_proj_kernel, tiles_per_seq=seq // tm),
        grid=grid,
        in_specs=[pl.BlockSpec((tm, k), lambda c, i: (i, 0)),
                  pl.BlockSpec((None, k, COL_TILE), lambda c, i: (layer, 0, XR0 // COL_TILE + c)),
                  pl.BlockSpec((None, k, COL_TILE), lambda c, i: (layer, 0, GR0 // COL_TILE + c)),
                  gate_w, gate_w, row, row, row,
                  pl.BlockSpec((None, CONV_WIDTH, COL_TILE), lambda c, i: (layer, 0, c)),
                  row] + cast_in,
        out_specs=[pl.BlockSpec((tm, COL_TILE), lambda c, i: (i, c))] + cast_out,
        out_shape=[jax.ShapeDtypeStruct((m, RNN_WIDTH), jnp.bfloat16)]
                  + [jax.ShapeDtypeStruct(w.shape[1:], jnp.bfloat16) for w in weights]
                  + [jax.ShapeDtypeStruct((n_plain_tiles, k, PLAIN_TN), jnp.bfloat16)],
        scratch_shapes=[pltpu.VMEM((SUBLANES + tm, COL_TILE), jnp.float32),
                        pltpu.VMEM((1, COL_TILE), jnp.float32)],
        compiler_params=pltpu.CompilerParams(
            dimension_semantics=("arbitrary", "arbitrary"), vmem_limit_bytes=VMEM_LIMIT),
        name="rnn_projection",
    )(x2d, w_in, w_in, w_r, w_i, b_r, b_i, lam, convw, convb, wao, wro, wo, w_in, w_in)


def _mixer_kernel(relb_ref, sinks_ref,
                  pa_ref, yr_ref, x_ref, bucket_ref, wao_ref, wro_ref, wo_ref, lng_ref, lnb_ref,
                  out_ref,
                  bias_scr, kx_scr, vx_scr, attn_scr,
                  *, alpha):
    tm = MIX_TM
    nqb = tm // WINDOW
    first_tile = pl.program_id(1) == 0
    f32 = jnp.float32
    bf16 = jnp.bfloat16

    @pl.when((pl.program_id(0) == 0) & first_tile)
    def _():
        bucket = bucket_ref[...]
        for t in range(N_Q_HEADS):
            h = 4 * (t // 4) + 2 * (t % 2) + (t // 2) % 2
            acc = jnp.full((WINDOW, 2 * WINDOW), NEG_INF, f32)
            for b in range(N_BUCKETS):
                acc = jnp.where(bucket == b, relb_ref[b, h], acc)
            bias_scr[t * WINDOW:(t + 1) * WINDOW, :] = acc

    for scr in (kx_scr, vx_scr):
        prev = scr[:, tm:tm + WINDOW, :]
        scr[:, 0:WINDOW, :] = jnp.where(first_tile, jnp.zeros_like(prev), prev)

    u32 = jnp.uint32
    lo_u = lax.broadcasted_iota(jnp.int32, (tm // 2, LANES), 1) < HEAD_DIM
    zero_u = jnp.zeros((tm // 2, LANES), u32)
    for p in range(N_KV_HEADS // 2):
        for src0, scr in ((PA_K0, kx_scr), (PA_V0, vx_scr)):
            t = pltpu.bitcast(pa_ref[:, src0 + p * LANES: src0 + (p + 1) * LANES], u32)
            tr = pltpu.roll(t, HEAD_DIM, axis=1)
            variants = (jnp.where(lo_u, t, zero_u),
                        jnp.where(lo_u, zero_u, tr),
                        jnp.where(lo_u, tr, zero_u),
                        jnp.where(lo_u, zero_u, t))
            for v, val in enumerate(variants):
                scr[len(variants) * p + v, WINDOW:WINDOW + tm, :] = pltpu.bitcast(val, bf16)

    lo_q = lax.broadcasted_iota(jnp.int32, (WINDOW, LANES), 1) < HEAD_DIM
    col = lax.broadcasted_iota(jnp.int32, (1, 2 * WINDOW), 1)
    ones_blk = jnp.ones((2 * WINDOW, LANES), bf16)
    nt_dims = (((1,), (1,)), ((), ()))
    y_rnn_parts = []
    for j in range(nqb):
        r0 = j * WINDOW
        s_parts = []
        for kh in range(N_KV_HEADS):
            qq = jnp.concatenate(
                [pa_ref[r0:r0 + WINDOW, PA_Q0 + (2 * kh + pr) * LANES: PA_Q0 + (2 * kh + pr + 1) * LANES]
                 for pr in range(2)], axis=0) * jnp.asarray(HEAD_DIM ** -0.5, bf16)
            for par in range(2):
                kk = kx_scr[2 * kh + par, r0:r0 + 2 * WINDOW, :]
                s_parts.append(lax.dot_general(qq, kk, nt_dims, preferred_element_type=f32))
        cw = D_MODEL // nqb
        y_rnn_parts.append(jnp.dot(yr_ref[...], wro_ref[:, j * cw:(j + 1) * cw], preferred_element_type=f32))
        s = jnp.concatenate(s_parts, axis=0) + bias_scr[...]
        if j == 0:
            s = s + jnp.where(first_tile & (col < WINDOW), NEG_INF, 0.0).astype(f32)
        m = jnp.max(s, axis=-1, keepdims=True)
        p = jnp.exp(s - m).astype(bf16)
        for kh in range(N_KV_HEADS):
            o_ext = []
            for par in range(2):
                t0 = (2 * kh + par) * 2 * WINDOW
                vv = jnp.concatenate([vx_scr[2 * kh + par, r0:r0 + 2 * WINDOW, :], ones_blk], axis=1)
                o_ext.append(jnp.dot(p[t0:t0 + 2 * WINDOW, :], vv, preferred_element_type=f32))
            for pr in range(2):
                pair = 2 * kh + pr
                rows = slice(pr * WINDOW, (pr + 1) * WINDOW)
                rden = []
                for par in range(2):
                    t0 = (2 * kh + par) * 2 * WINDOW + pr * WINDOW
                    sink_term = jnp.exp(sinks_ref[2 * pair + par] - m[t0:t0 + WINDOW, :])
                    rden.append(1.0 / (o_ext[par][rows, LANES:] + sink_term))
                o = (o_ext[0][rows, :LANES] + o_ext[1][rows, :LANES]) * jnp.where(lo_q, rden[0], rden[1])
                ga = pa_ref[r0:r0 + WINDOW, PA_GA0 + pair * LANES: PA_GA0 + (pair + 1) * LANES].astype(f32)
                attn_scr[r0:r0 + WINDOW, pair * LANES:(pair + 1) * LANES] = (
                    o * (ga * _sigmoid(ga))).astype(bf16)

    y_attn = jnp.dot(attn_scr[...], wao_ref[...], preferred_element_type=f32)
    y_rnn = jnp.concatenate(y_rnn_parts, axis=1)
    mixed = (_sigmoid(pa_ref[:, PA_MA0:PA_MA0 + D_MODEL].astype(f32)) * y_attn
             + _sigmoid(pa_ref[:, PA_MR0:PA_MR0 + D_MODEL].astype(f32)) * y_rnn).astype(bf16)
    zs = []
    row_sum = jnp.zeros((tm, 1), f32)
    for c in range(D_MODEL // COL_TILE):
        k0 = c * COL_TILE
        z = alpha * x_ref[:, k0:k0 + COL_TILE] + jnp.dot(mixed, wo_ref[:, k0:k0 + COL_TILE],
                                                         preferred_element_type=f32)
        row_sum = row_sum + jnp.sum(z, axis=-1, keepdims=True)
        zs.append(z)
    zc = jnp.concatenate(zs, axis=1) - row_sum * (1.0 / D_MODEL)
    var = jnp.mean(zc * zc, axis=-1, keepdims=True)
    out_ref[...] = zc * lax.rsqrt(var + LN_EPS) * lng_ref[...] + lnb_ref[...]


def _const_spec(shape):
    nd = len(shape)
    return pl.BlockSpec(shape, lambda b, s: (0,) * nd, pipeline_mode=pl.Buffered(1))


def _mixer(pa, yr, x2d, bucket, relb, sinks, wao, wro, wo, lng, lnb, *, batch, seq, alpha):
    tm = MIX_TM
    nst = seq // tm
    row_map = lambda b, s: (b * nst + s, 0)
    smem = pl.BlockSpec(memory_space=pltpu.SMEM)
    in_specs = [
        smem, smem,
        pl.BlockSpec((tm, PA_WIDTH), row_map),
        pl.BlockSpec((tm, RNN_WIDTH), row_map),
        pl.BlockSpec((tm, D_MODEL), row_map),
        _const_spec(bucket.shape), _const_spec(wao.shape), _const_spec(wro.shape), _const_spec(wo.shape),
        _const_spec(lng.shape), _const_spec(lnb.shape),
    ]
    scratch = [
        pltpu.VMEM((N_Q_HEADS * WINDOW, 2 * WINDOW), jnp.float32),
        pltpu.VMEM((N_KV_VARIANTS, WINDOW + tm, LANES), jnp.bfloat16),
        pltpu.VMEM((N_KV_VARIANTS, WINDOW + tm, LANES), jnp.bfloat16),
        pltpu.VMEM((tm, ATTN_WIDTH), jnp.bfloat16),
    ]
    return pl.pallas_call(
        functools.partial(_mixer_kernel, alpha=alpha),
        grid=(batch, nst),
        in_specs=in_specs,
        out_specs=pl.BlockSpec((tm, D_MODEL), row_map),
        out_shape=jax.ShapeDtypeStruct((batch * seq, D_MODEL), jnp.float32),
        scratch_shapes=scratch,
        compiler_params=pltpu.CompilerParams(
            dimension_semantics=("arbitrary", "arbitrary"), vmem_limit_bytes=VMEM_LIMIT),
        name="mixer",
    )(relb, sinks, pa, yr, x2d, bucket, wao, wro, wo, lng, lnb)


def kernel(x, w_in, conv_w, conv_b, w_r, b_r, w_i, b_i, lru_lambda, sinks, w_attn_out, w_rnn_out, w_out,
           ln_g, ln_b, rel_bias):
    batch, seq, d = x.shape
    depth = w_in.shape[0]
    alpha = (2.0 * depth) ** 0.25
    bucket = jnp.asarray(_t5_bucket_table())
    x2d = x.reshape(batch * seq, d)
    as_rows = lambda v: v.reshape(depth, 1, v.shape[-1])
    b_r, b_i, lam, conv_b = as_rows(b_r), as_rows(b_i), as_rows(lru_lambda), as_rows(conv_b)
    for l in range(depth):
        yr, wao, wro, wo, w_plain = _rnn_projection(x2d, w_in, l, w_r, w_i, b_r, b_i, lam, conv_w, conv_b,
                                                    w_attn_out, w_rnn_out, w_out, seq=seq)
        pa = _plain_projection(x2d, w_plain)
        x2d = _mixer(pa, yr, x2d, bucket, rel_bias, sinks[l], wao, wro, wo,
                     ln_g[l][None, :], ln_b[l][None, :], batch=batch, seq=seq, alpha=alpha)
    return x2d.reshape(batch, seq, d)
```

```python
import functools
import math

import numpy as np
import jax
import jax.numpy as jnp
from jax import lax
from jax.experimental import pallas as pl
from jax.experimental.pallas import tpu as pltpu

D_MODEL = 2048
N_Q_HEADS = 16
N_KV_HEADS = 4
HEAD_DIM = 64
ATTN_WIDTH = N_Q_HEADS * HEAD_DIM
KV_WIDTH = N_KV_HEADS * HEAD_DIM
WINDOW = 128
N_BUCKETS = 32
MAX_DISTANCE = 128
RNN_WIDTH = D_MODEL
RNN_BLOCKS = 16
RNN_BLOCK_DIM = RNN_WIDTH // RNN_BLOCKS
CONV_WIDTH = 4
LRU_C = 8.0
LN_EPS = 1e-5
NEG_INF = -1e30

Q0 = 0
K0 = Q0 + ATTN_WIDTH
V0 = K0 + KV_WIDTH
GA0 = V0 + KV_WIDTH
XR0 = GA0 + ATTN_WIDTH
GR0 = XR0 + RNN_WIDTH
MA0 = GR0 + RNN_WIDTH
MR0 = MA0 + D_MODEL
IN_WIDTH = MR0 + D_MODEL

LANES = 128
SUBLANES = 8
VMEM_LIMIT = 60 * 1024 * 1024

COL_TILE = 512
PLAIN_TM = 1024
RNN_TM = 1024
MIX_TM = 256
CAST_ROWS = 64

PA_Q0 = 0
PA_K0 = PA_Q0 + ATTN_WIDTH
PA_V0 = PA_K0 + KV_WIDTH
PA_GA0 = PA_V0 + KV_WIDTH
PA_MA0 = PA_GA0 + ATTN_WIDTH
PA_WIDTH = PA_MA0 + D_MODEL
PLAIN_TN = PA_WIDTH // 2
MXU_DIM = 256
N_KV_VARIANTS = 2 * N_KV_HEADS


def _t5_bucket_table():
    qi = np.arange(WINDOW)[:, None]
    kj = np.arange(2 * WINDOW)[None, :]
    dist = qi + WINDOW - kj
    max_exact = N_BUCKETS // 2
    n = np.maximum(dist, 0)
    nf = np.maximum(n, max_exact).astype(np.float32)
    large = max_exact + (np.log(nf / np.float32(max_exact)) / np.float32(math.log(MAX_DISTANCE / max_exact))
                         * np.float32(N_BUCKETS - max_exact)).astype(np.int32)
    large = np.minimum(large, N_BUCKETS - 1)
    bucket = np.where(n < max_exact, n, large)
    in_window = (dist >= 0) & (dist < WINDOW)
    return np.where(in_window, bucket, -1).astype(np.int32)


def _sigmoid(x):
    return 0.5 * jnp.tanh(0.5 * x) + 0.5


def _plain_proj_kernel(x_ref, w_ref, pa_ref):
    xb = x_ref[...].astype(jnp.bfloat16)
    split = (PLAIN_TN // MXU_DIM + 1) // 2 * MXU_DIM
    for c0, c1 in ((0, split), (split, PLAIN_TN)):
        pa_ref[:, c0:c1] = jnp.dot(xb, w_ref[:, c0:c1], preferred_element_type=jnp.float32).astype(pa_ref.dtype)


def _plain_projection(x2d, w_plain):
    m, k = x2d.shape
    tm = PLAIN_TM
    n_tiles = w_plain.shape[0]
    return pl.pallas_call(
        _plain_proj_kernel,
        grid=(n_tiles, m // tm),
        in_specs=[pl.BlockSpec((tm, k), lambda n, i: (i, 0)),
                  pl.BlockSpec((None, k, PLAIN_TN), lambda n, i: (n, 0, 0))],
        out_specs=pl.BlockSpec((tm, PLAIN_TN), lambda n, i: (i, n)),
        out_shape=jax.ShapeDtypeStruct((m, PA_WIDTH), jnp.bfloat16),
        compiler_params=pltpu.CompilerParams(
            dimension_semantics=("arbitrary", "arbitrary"), vmem_limit_bytes=VMEM_LIMIT),
        name="plain_projection",
    )(x2d, w_plain)


def _linear_scan(a, u, carry):
    tm = a.shape[0]
    groups = tm // SUBLANES
    a3 = a.reshape(groups, SUBLANES, LANES)
    u3 = u.reshape(groups, SUBLANES, LANES)
    row = lax.broadcasted_iota(jnp.int32, (groups, SUBLANES, LANES), 1)
    for d in (1, 2, 4):
        keep = row >= d
        a_sh = pltpu.roll(a3, d, axis=1)
        u_sh = pltpu.roll(u3, d, axis=1)
        u3 = jnp.where(keep, a3 * u_sh + u3, u3)
        a3 = jnp.where(keep, a3 * a_sh, a3)
    hs = []
    for g in range(groups):
        hg = u3[g] + a3[g] * carry
        hs.append(hg)
        carry = hg[SUBLANES - 1:SUBLANES, :]
    return jnp.concatenate(hs, axis=0), carry


def _rnn_proj_kernel(x_ref, wx_ref, wg_ref, wr_ref, wi_ref, br_ref, bi_ref, lam_ref, convw_ref, convb_ref,
                     wmr_ref, wao_ref, wro_ref, wo_ref, wpa_ref, wpm_ref,
                     yr_ref, pmr_ref, wao_bf_ref, wro_bf_ref, wo_bf_ref, wp_bf_ref, hist_scr, hcar_scr,
                     *, tiles_per_seq):
    tm = RNN_TM
    f32 = jnp.float32
    bf16 = jnp.bfloat16
    first = (pl.program_id(1) % tiles_per_seq) == 0

    hist_scr[0:SUBLANES, :] = jnp.where(first, 0.0, hist_scr[tm:tm + SUBLANES, :])
    carry_all = jnp.where(first, 0.0, hcar_scr[...])

    xb = x_ref[...].astype(bf16)
    hist_scr[SUBLANES:SUBLANES + tm, :] = jnp.dot(xb, wx_ref[...].astype(bf16), preferred_element_type=f32)
    wao_bf_ref[...] = wao_ref[...].astype(bf16)
    wro_bf_ref[...] = wro_ref[...].astype(bf16)
    wo_bf_ref[...] = wo_ref[...].astype(bf16)
    w_plain = jnp.concatenate([wpa_ref[0], wpm_ref[0]], axis=1).astype(bf16)
    for t in range(PA_WIDTH // PLAIN_TN):
        wp_bf_ref[t] = w_plain[:, t * PLAIN_TN:(t + 1) * PLAIN_TN]
    wg = wg_ref[...].astype(bf16)
    wmr = wmr_ref[...].astype(bf16)
    g_top = jnp.dot(xb[:tm // 2], wg, preferred_element_type=f32)

    neg_lam = -lam_ref[...]
    half_scale = (-0.5 * LRU_C) * (jnp.maximum(neg_lam, 0.0) + jnp.log1p(jnp.exp(-jnp.abs(neg_lam))))
    half_br = 0.5 * br_ref[...]
    half_bi = 0.5 * bi_ref[...]
    row8 = lax.broadcasted_iota(jnp.int32, (SUBLANES, RNN_BLOCK_DIM), 0)
    for b in range(COL_TILE // RNN_BLOCK_DIM):
        c0 = b * RNN_BLOCK_DIM
        c1 = c0 + RNN_BLOCK_DIM
        x_cur = hist_scr[SUBLANES:SUBLANES + tm, c0:c1]
        x_old = hist_scr[0:SUBLANES, c0:c1]
        acc, acc_old = None, None
        for kk in range(CONV_WIDTH - 1):
            wk = convw_ref[kk:kk + 1, c0:c1]
            acc = wk * x_cur if acc is None else wk * x_cur + acc
            acc_old = wk * x_old if acc_old is None else wk * x_old + acc_old
            acc_old = pltpu.roll(acc_old, 1, axis=0)
            shifted = pltpu.roll(acc, 1, axis=0)
            top = jnp.where(row8 == 0, acc_old, shifted[0:SUBLANES, :])
            acc = jnp.concatenate([top, shifted[SUBLANES:, :]], axis=0)
        xc = convw_ref[CONV_WIDTH - 1:CONV_WIDTH, c0:c1] * x_cur + acc + convb_ref[:, c0:c1]
        w_gates = (0.5 * jnp.concatenate([wr_ref[b], wi_ref[b]], axis=1)).astype(bf16)
        ri = jnp.dot(xc.astype(bf16), w_gates, preferred_element_type=f32)
        if b == 0:
            g = jnp.concatenate([g_top, jnp.dot(xb[tm // 2:], wg, preferred_element_type=f32)], axis=0)
        elif b <= 2:
            rows = slice((b - 1) * tm // 2, b * tm // 2)
            pmr_ref[rows, :] = jnp.dot(xb[rows], wmr, preferred_element_type=f32).astype(pmr_ref.dtype)
        tanh_r = jnp.tanh(ri[:, :RNN_BLOCK_DIM] + half_br[:, c0:c1])
        tanh_i = jnp.tanh(ri[:, RNN_BLOCK_DIM:] + half_bi[:, c0:c1])
        log_a = tanh_r * half_scale[:, c0:c1] + half_scale[:, c0:c1]
        a = jnp.exp(log_a)
        one_m_a2 = jnp.tanh(log_a) * (-1.0 - a * a)
        mult = jnp.where(one_m_a2 > 0.0, one_m_a2 * lax.rsqrt(one_m_a2), 0.0)
        u = (mult * xc) * (0.5 * tanh_i + 0.5)
        h, carry = _linear_scan(a, u, carry_all[:, c0:c1])
        hcar_scr[:, c0:c1] = carry
        gb = g[:, c0:c1]
        yr_ref[:, c0:c1] = (h * (gb * _sigmoid(gb))).astype(yr_ref.dtype)


def _rnn_projection(x2d, w_in, layer, w_r, w_i, b_r, b_i, lam, convw, convb, wao, wro, wo, *, seq):
    m, k = x2d.shape
    tm = RNN_TM
    nblk = COL_TILE // RNN_BLOCK_DIM
    grid = (RNN_WIDTH // COL_TILE, m // tm)
    gate_w = pl.BlockSpec((None, nblk, RNN_BLOCK_DIM, RNN_BLOCK_DIM), lambda c, i: (layer, c, 0, 0))
    row = pl.BlockSpec((None, 1, COL_TILE), lambda c, i: (layer, 0, c))

    weights = (wao, wro, wo)
    cast_in, cast_out = [], []
    for w in weights:
        nb = w.shape[1] // CAST_ROWS
        assert nb <= grid[0] * grid[1]
        blk = lambda c, i, nb=nb: jnp.minimum(c * grid[1] + i, nb - 1)
        cast_in.append(pl.BlockSpec((None, CAST_ROWS, w.shape[2]), lambda c, i, blk=blk: (layer, blk(c, i), 0)))
        cast_out.append(pl.BlockSpec((CAST_ROWS, w.shape[2]), lambda c, i, blk=blk: (blk(c, i), 0)))
    w_rows = k // (grid[0] * grid[1])
    for col0, width in ((Q0, XR0 - Q0), (MA0, MR0 - MA0)):
        cast_in.append(pl.BlockSpec(
            (pl.Element(1), pl.Element(w_rows), pl.Element(width)),
            lambda c, i, col0=col0: (layer, pl.multiple_of((c * grid[1] + i) * w_rows, SUBLANES), col0)))
    n_plain_tiles = PA_WIDTH // PLAIN_TN
    cast_out.append(pl.BlockSpec((n_plain_tiles, w_rows, PLAIN_TN), lambda c, i: (0, c * grid[1] + i, 0)))

    return pl.pallas_call(
        functools.partial(_rnn_proj_kernel, tiles_per_seq=seq // tm),
        grid=grid,
        in_specs=[pl.BlockSpec((tm, k), lambda c, i: (i, 0)),
                  pl.BlockSpec((None, k, COL_TILE), lambda c, i: (layer, 0, XR0 // COL_TILE + c)),
                  pl.BlockSpec((None, k, COL_TILE), lambda c, i: (layer, 0, GR0 // COL_TILE + c)),
                  gate_w, gate_w, row, row, row,
                  pl.BlockSpec((None, CONV_WIDTH, COL_TILE), lambda c, i: (layer, 0, c)),
                  row,
                  pl.BlockSpec((None, k, COL_TILE), lambda c, i: (layer, 0, MR0 // COL_TILE + c),
                               pipeline_mode=pl.Buffered(1))] + cast_in,
        out_specs=[pl.BlockSpec((tm, COL_TILE), lambda c, i: (i, c))] * 2 + cast_out,
        out_shape=[jax.ShapeDtypeStruct((m, RNN_WIDTH), jnp.bfloat16),
                   jax.ShapeDtypeStruct((m, D_MODEL), jnp.bfloat16)]
                  + [jax.ShapeDtypeStruct(w.shape[1:], jnp.bfloat16) for w in weights]
                  + [jax.ShapeDtypeStruct((n_plain_tiles, k, PLAIN_TN), jnp.bfloat16)],
        scratch_shapes=[pltpu.VMEM((SUBLANES + tm, COL_TILE), jnp.float32),
                        pltpu.VMEM((1, COL_TILE), jnp.float32)],
        compiler_params=pltpu.CompilerParams(
            dimension_semantics=("arbitrary", "arbitrary"), vmem_limit_bytes=VMEM_LIMIT),
        name="rnn_projection",
    )(x2d, w_in, w_in, w_r, w_i, b_r, b_i, lam, convw, convb, w_in, wao, wro, wo, w_in, w_in)


def _mixer_kernel(relb_ref, sinks_ref,
                  pa_ref, pmr_ref, yr_ref, x_ref, bucket_ref, wao_ref, wro_ref, wo_ref, lng_ref, lnb_ref,
                  out_ref,
                  bias_scr, kx_scr, vx_scr, attn_scr,
                  *, alpha):
    tm = MIX_TM
    nqb = tm // WINDOW
    first_tile = pl.program_id(1) == 0
    f32 = jnp.float32
    bf16 = jnp.bfloat16

    @pl.when((pl.program_id(0) == 0) & first_tile)
    def _():
        bucket = bucket_ref[...]
        for t in range(N_Q_HEADS):
            h = 4 * (t // 4) + 2 * (t % 2) + (t // 2) % 2
            acc = jnp.full((WINDOW, 2 * WINDOW), NEG_INF, f32)
            for b in range(N_BUCKETS):
                acc = jnp.where(bucket == b, relb_ref[b, h], acc)
            bias_scr[t * WINDOW:(t + 1) * WINDOW, :] = acc

    for scr in (kx_scr, vx_scr):
        prev = scr[:, tm:tm + WINDOW, :]
        scr[:, 0:WINDOW, :] = jnp.where(first_tile, jnp.zeros_like(prev), prev)

    u32 = jnp.uint32
    lo_u = lax.broadcasted_iota(jnp.int32, (tm // 2, LANES), 1) < HEAD_DIM
    zero_u = jnp.zeros((tm // 2, LANES), u32)
    for p in range(N_KV_HEADS // 2):
        for src0, scr in ((PA_K0, kx_scr), (PA_V0, vx_scr)):
            t = pltpu.bitcast(pa_ref[:, src0 + p * LANES: src0 + (p + 1) * LANES], u32)
            tr = pltpu.roll(t, HEAD_DIM, axis=1)
            variants = (jnp.where(lo_u, t, zero_u),
                        jnp.where(lo_u, zero_u, tr),
                        jnp.where(lo_u, tr, zero_u),
                        jnp.where(lo_u, zero_u, t))
            for v, val in enumerate(variants):
                scr[len(variants) * p + v, WINDOW:WINDOW + tm, :] = pltpu.bitcast(val, bf16)

    lo_q = lax.broadcasted_iota(jnp.int32, (WINDOW, LANES), 1) < HEAD_DIM
    col = lax.broadcasted_iota(jnp.int32, (1, 2 * WINDOW), 1)
    ones_blk = jnp.ones((2 * WINDOW, LANES), bf16)
    nt_dims = (((1,), (1,)), ((), ()))
    y_rnn_parts = []
    for j in range(nqb):
        r0 = j * WINDOW
        s_parts = []
        for kh in range(N_KV_HEADS):
            qq = jnp.concatenate(
                [pa_ref[r0:r0 + WINDOW, PA_Q0 + (2 * kh + pr) * LANES: PA_Q0 + (2 * kh + pr + 1) * LANES]
                 for pr in range(2)], axis=0) * jnp.asarray(HEAD_DIM ** -0.5, bf16)
            for par in range(2):
                kk = kx_scr[2 * kh + par, r0:r0 + 2 * WINDOW, :]
                s_parts.append(lax.dot_general(qq, kk, nt_dims, preferred_element_type=f32))
        cw = D_MODEL // nqb
        y_rnn_parts.append(jnp.dot(yr_ref[...], wro_ref[:, j * cw:(j + 1) * cw], preferred_element_type=f32))
        s = jnp.concatenate(s_parts, axis=0) + bias_scr[...]
        if j == 0:
            s = s + jnp.where(first_tile & (col < WINDOW), NEG_INF, 0.0).astype(f32)
        m = jnp.max(s, axis=-1, keepdims=True)
        p = jnp.exp(s - m).astype(bf16)
        for kh in range(N_KV_HEADS):
            o_ext = []
            for par in range(2):
                t0 = (2 * kh + par) * 2 * WINDOW
                vv = jnp.concatenate([vx_scr[2 * kh + par, r0:r0 + 2 * WINDOW, :], ones_blk], axis=1)
                o_ext.append(jnp.dot(p[t0:t0 + 2 * WINDOW, :], vv, preferred_element_type=f32))
            for pr in range(2):
                pair = 2 * kh + pr
                rows = slice(pr * WINDOW, (pr + 1) * WINDOW)
                rden = []
                for par in range(2):
                    t0 = (2 * kh + par) * 2 * WINDOW + pr * WINDOW
                    sink_term = jnp.exp(sinks_ref[2 * pair + par] - m[t0:t0 + WINDOW, :])
                    rden.append(1.0 / (o_ext[par][rows, LANES:] + sink_term))
                o = (o_ext[0][rows, :LANES] + o_ext[1][rows, :LANES]) * jnp.where(lo_q, rden[0], rden[1])
                ga = pa_ref[r0:r0 + WINDOW, PA_GA0 + pair * LANES: PA_GA0 + (pair + 1) * LANES].astype(f32)
                attn_scr[r0:r0 + WINDOW, pair * LANES:(pair + 1) * LANES] = (
                    o * (ga * _sigmoid(ga))).astype(bf16)

    y_attn = jnp.dot(attn_scr[...], wao_ref[...], preferred_element_type=f32)
    y_rnn = jnp.concatenate(y_rnn_parts, axis=1)
    mixed = (_sigmoid(pa_ref[:, PA_MA0:PA_MA0 + D_MODEL].astype(f32)) * y_attn
             + _sigmoid(pmr_ref[...].astype(f32)) * y_rnn).astype(bf16)
    zs = []
    row_sum = jnp.zeros((tm, 1), f32)
    for c in range(D_MODEL // COL_TILE):
        k0 = c * COL_TILE
        z = alpha * x_ref[:, k0:k0 + COL_TILE] + jnp.dot(mixed, wo_ref[:, k0:k0 + COL_TILE],
                                                         preferred_element_type=f32)
        row_sum = row_sum + jnp.sum(z, axis=-1, keepdims=True)
        zs.append(z)
    zc = jnp.concatenate(zs, axis=1) - row_sum * (1.0 / D_MODEL)
    var = jnp.mean(zc * zc, axis=-1, keepdims=True)
    out_ref[...] = zc * lax.rsqrt(var + LN_EPS) * lng_ref[...] + lnb_ref[...]


def _const_spec(shape):
    nd = len(shape)
    return pl.BlockSpec(shape, lambda b, s: (0,) * nd, pipeline_mode=pl.Buffered(1))


def _mixer(pa, pmr, yr, x2d, bucket, relb, sinks, wao, wro, wo, lng, lnb, *, batch, seq, alpha):
    tm = MIX_TM
    nst = seq // tm
    row_map = lambda b, s: (b * nst + s, 0)
    smem = pl.BlockSpec(memory_space=pltpu.SMEM)
    in_specs = [
        smem, smem,
        pl.BlockSpec((tm, PA_WIDTH), row_map),
        pl.BlockSpec((tm, D_MODEL), row_map),
        pl.BlockSpec((tm, RNN_WIDTH), row_map),
        pl.BlockSpec((tm, D_MODEL), row_map),
        _const_spec(bucket.shape), _const_spec(wao.shape), _const_spec(wro.shape), _const_spec(wo.shape),
        _const_spec(lng.shape), _const_spec(lnb.shape),
    ]
    scratch = [
        pltpu.VMEM((N_Q_HEADS * WINDOW, 2 * WINDOW), jnp.float32),
        pltpu.VMEM((N_KV_VARIANTS, WINDOW + tm, LANES), jnp.bfloat16),
        pltpu.VMEM((N_KV_VARIANTS, WINDOW + tm, LANES), jnp.bfloat16),
        pltpu.VMEM((tm, ATTN_WIDTH), jnp.bfloat16),
    ]
    return pl.pallas_call(
        functools.partial(_mixer_kernel, alpha=alpha),
        grid=(batch, nst),
        in_specs=in_specs,
        out_specs=pl.BlockSpec((tm, D_MODEL), row_map),
        out_shape=jax.ShapeDtypeStruct((batch * seq, D_MODEL), jnp.float32),
        scratch_shapes=scratch,
        compiler_params=pltpu.CompilerParams(
            dimension_semantics=("arbitrary", "arbitrary"), vmem_limit_bytes=VMEM_LIMIT),
        name="mixer",
    )(relb, sinks, pa, pmr, yr, x2d, bucket, wao, wro, wo, lng, lnb)


def kernel(x, w_in, conv_w, conv_b, w_r, b_r, w_i, b_i, lru_lambda, sinks, w_attn_out, w_rnn_out, w_out,
           ln_g, ln_b, rel_bias):
    batch, seq, d = x.shape
    depth = w_in.shape[0]
    alpha = (2.0 * depth) ** 0.25
    bucket = jnp.asarray(_t5_bucket_table())
    x2d = x.reshape(batch * seq, d)
    as_rows = lambda v: v.reshape(depth, 1, v.shape[-1])
    b_r, b_i, lam, conv_b = as_rows(b_r), as_rows(b_i), as_rows(lru_lambda), as_rows(conv_b)
    for l in range(depth):
        yr, pmr, wao, wro, wo, w_plain = _rnn_projection(x2d, w_in, l, w_r, w_i, b_r, b_i, lam, conv_w, conv_b,
                                                         w_attn_out, w_rnn_out, w_out, seq=seq)
        pa = _plain_projection(x2d, w_plain)
        x2d = _mixer(pa, pmr, yr, x2d, bucket, rel_bias, sinks[l], wao, wro, wo,
                     ln_g[l][None, :], ln_b[l][None, :], batch=batch, seq=seq, alpha=alpha)
    return x2d.reshape(batch, seq, d)
```

```python
import functools
import math

import numpy as np
import jax
import jax.numpy as jnp
from jax import lax
from jax.experimental import pallas as pl
from jax.experimental.pallas import tpu as pltpu

D_MODEL = 2048
N_Q_HEADS = 16
N_KV_HEADS = 4
HEAD_DIM = 64
ATTN_WIDTH = N_Q_HEADS * HEAD_DIM
KV_WIDTH = N_KV_HEADS * HEAD_DIM
WINDOW = 128
N_BUCKETS = 32
MAX_DISTANCE = 128
RNN_WIDTH = D_MODEL
RNN_BLOCKS = 16
RNN_BLOCK_DIM = RNN_WIDTH // RNN_BLOCKS
CONV_WIDTH = 4
LRU_C = 8.0
LN_EPS = 1e-5
NEG_INF = -1e30

Q0 = 0
K0 = Q0 + ATTN_WIDTH
V0 = K0 + KV_WIDTH
GA0 = V0 + KV_WIDTH
XR0 = GA0 + ATTN_WIDTH
GR0 = XR0 + RNN_WIDTH
MA0 = GR0 + RNN_WIDTH
MR0 = MA0 + D_MODEL
IN_WIDTH = MR0 + D_MODEL

LANES = 128
SUBLANES = 8
VMEM_LIMIT = 60 * 1024 * 1024

COL_TILE = 512
PLAIN_TM = 1024
RNN_TM = 1024
MIX_TM = 256

PA_Q0 = 0
PA_K0 = PA_Q0 + ATTN_WIDTH
PA_V0 = PA_K0 + KV_WIDTH
PA_GA0 = PA_V0 + KV_WIDTH
PA_MA0 = PA_GA0 + ATTN_WIDTH
PA_WIDTH = PA_MA0 + D_MODEL
PLAIN_TN = PA_WIDTH // 2
MXU_DIM = 256
N_KV_VARIANTS = 2 * N_KV_HEADS


def _t5_bucket_table():
    qi = np.arange(WINDOW)[:, None]
    kj = np.arange(2 * WINDOW)[None, :]
    dist = qi + WINDOW - kj
    max_exact = N_BUCKETS // 2
    n = np.maximum(dist, 0)
    nf = np.maximum(n, max_exact).astype(np.float32)
    large = max_exact + (np.log(nf / np.float32(max_exact)) / np.float32(math.log(MAX_DISTANCE / max_exact))
                         * np.float32(N_BUCKETS - max_exact)).astype(np.int32)
    large = np.minimum(large, N_BUCKETS - 1)
    bucket = np.where(n < max_exact, n, large)
    in_window = (dist >= 0) & (dist < WINDOW)
    return np.where(in_window, bucket, -1).astype(np.int32)


def _sigmoid(x):
    return 0.5 * jnp.tanh(0.5 * x) + 0.5


def _plain_proj_kernel(x_ref, w_ref, wao_ref, wro_ref, wo_ref, pa_ref, wao_bf_ref, wro_bf_ref, wo_bf_ref):
    wao_bf_ref[...] = wao_ref[...].astype(jnp.bfloat16)
    wro_bf_ref[...] = wro_ref[...].astype(jnp.bfloat16)
    wo_bf_ref[...] = wo_ref[...].astype(jnp.bfloat16)
    xb = x_ref[...].astype(jnp.bfloat16)
    split = (PLAIN_TN // MXU_DIM + 1) // 2 * MXU_DIM
    for c0, c1 in ((0, split), (split, PLAIN_TN)):
        pa_ref[:, c0:c1] = jnp.dot(xb, w_ref[:, c0:c1], preferred_element_type=jnp.float32).astype(pa_ref.dtype)


def _plain_projection(x2d, w_plain, wao, wro, wo, layer):
    m, k = x2d.shape
    tm = PLAIN_TM
    n_tiles = w_plain.shape[0]
    n_steps = n_tiles * (m // tm)
    weights = (wao, wro, wo)
    cast_in, cast_out = [], []
    for w in weights:
        rows = w.shape[1] // n_steps
        step = lambda n, i: n * (m // tm) + i
        cast_in.append(pl.BlockSpec((None, rows, w.shape[2]), lambda n, i, step=step: (layer, step(n, i), 0)))
        cast_out.append(pl.BlockSpec((rows, w.shape[2]), lambda n, i, step=step: (step(n, i), 0)))
    return pl.pallas_call(
        _plain_proj_kernel,
        grid=(n_tiles, m // tm),
        in_specs=[pl.BlockSpec((tm, k), lambda n, i: (i, 0)),
                  pl.BlockSpec((None, k, PLAIN_TN), lambda n, i: (n, 0, 0))] + cast_in,
        out_specs=[pl.BlockSpec((tm, PLAIN_TN), lambda n, i: (i, n))] + cast_out,
        out_shape=[jax.ShapeDtypeStruct((m, PA_WIDTH), jnp.bfloat16)]
                  + [jax.ShapeDtypeStruct(w.shape[1:], jnp.bfloat16) for w in weights],
        compiler_params=pltpu.CompilerParams(
            dimension_semantics=("arbitrary", "arbitrary"), vmem_limit_bytes=VMEM_LIMIT),
        name="plain_projection",
    )(x2d, w_plain, wao, wro, wo)


def _linear_scan(a, u, carry):
    tm = a.shape[0]
    groups = tm // SUBLANES
    a3 = a.reshape(groups, SUBLANES, LANES)
    u3 = u.reshape(groups, SUBLANES, LANES)
    row = lax.broadcasted_iota(jnp.int32, (groups, SUBLANES, LANES), 1)
    for d in (1, 2, 4):
        keep = row >= d
        a_sh = pltpu.roll(a3, d, axis=1)
        u_sh = pltpu.roll(u3, d, axis=1)
        u3 = jnp.where(keep, a3 * u_sh + u3, u3)
        a3 = jnp.where(keep, a3 * a_sh, a3)
    hs = []
    for g in range(groups):
        hg = u3[g] + a3[g] * carry
        hs.append(hg)
        carry = hg[SUBLANES - 1:SUBLANES, :]
    return jnp.concatenate(hs, axis=0), carry


def _rnn_proj_kernel(x_ref, wx_ref, wg_ref, wr_ref, wi_ref, br_ref, bi_ref, lam_ref, convw_ref, convb_ref,
                     wmr_ref, wpa_ref, wpm_ref,
                     yr_ref, pmr_ref, wp_bf_ref, hist_scr, hcar_scr,
                     *, tiles_per_seq):
    tm = RNN_TM
    f32 = jnp.float32
    bf16 = jnp.bfloat16
    first = (pl.program_id(1) % tiles_per_seq) == 0

    hist_scr[0:SUBLANES, :] = jnp.where(first, 0.0, hist_scr[tm:tm + SUBLANES, :])
    carry_all = jnp.where(first, 0.0, hcar_scr[...])

    xb = x_ref[...].astype(bf16)
    hist_scr[SUBLANES:SUBLANES + tm, :] = jnp.dot(xb, wx_ref[...].astype(bf16), preferred_element_type=f32)
    w_plain = jnp.concatenate([wpa_ref[0], wpm_ref[0]], axis=1).astype(bf16)
    for t in range(PA_WIDTH // PLAIN_TN):
        wp_bf_ref[t] = w_plain[:, t * PLAIN_TN:(t + 1) * PLAIN_TN]
    wg = wg_ref[...].astype(bf16)
    wmr = wmr_ref[...].astype(bf16)
    g_top = jnp.dot(xb[:tm // 2], wg, preferred_element_type=f32)

    neg_lam = -lam_ref[...]
    half_scale = (-0.5 * LRU_C) * (jnp.maximum(neg_lam, 0.0) + jnp.log1p(jnp.exp(-jnp.abs(neg_lam))))
    half_br = 0.5 * br_ref[...]
    half_bi = 0.5 * bi_ref[...]
    row8 = lax.broadcasted_iota(jnp.int32, (SUBLANES, RNN_BLOCK_DIM), 0)
    for b in range(COL_TILE // RNN_BLOCK_DIM):
        c0 = b * RNN_BLOCK_DIM
        c1 = c0 + RNN_BLOCK_DIM
        x_cur = hist_scr[SUBLANES:SUBLANES + tm, c0:c1]
        x_old = hist_scr[0:SUBLANES, c0:c1]
        acc, acc_old = None, None
        for kk in range(CONV_WIDTH - 1):
            wk = convw_ref[kk:kk + 1, c0:c1]
            acc = wk * x_cur if acc is None else wk * x_cur + acc
            acc_old = wk * x_old if acc_old is None else wk * x_old + acc_old
            acc_old = pltpu.roll(acc_old, 1, axis=0)
            shifted = pltpu.roll(acc, 1, axis=0)
            top = jnp.where(row8 == 0, acc_old, shifted[0:SUBLANES, :])
            acc = jnp.concatenate([top, shifted[SUBLANES:, :]], axis=0)
        xc = convw_ref[CONV_WIDTH - 1:CONV_WIDTH, c0:c1] * x_cur + acc + convb_ref[:, c0:c1]
        w_gates = (0.5 * jnp.concatenate([wr_ref[b], wi_ref[b]], axis=1)).astype(bf16)
        ri = jnp.dot(xc.astype(bf16), w_gates, preferred_element_type=f32)
        if b == 0:
            g = jnp.concatenate([g_top, jnp.dot(xb[tm // 2:], wg, preferred_element_type=f32)], axis=0)
        elif b <= 2:
            rows = slice((b - 1) * tm // 2, b * tm // 2)
            pmr_ref[rows, :] = jnp.dot(xb[rows], wmr, preferred_element_type=f32).astype(pmr_ref.dtype)
        tanh_r = jnp.tanh(ri[:, :RNN_BLOCK_DIM] + half_br[:, c0:c1])
        tanh_i = jnp.tanh(ri[:, RNN_BLOCK_DIM:] + half_bi[:, c0:c1])
        log_a = tanh_r * half_scale[:, c0:c1] + half_scale[:, c0:c1]
        a = jnp.exp(log_a)
        one_m_a2 = jnp.tanh(log_a) * (-1.0 - a * a)
        mult = jnp.where(one_m_a2 > 0.0, one_m_a2 * lax.rsqrt(one_m_a2), 0.0)
        u = (mult * xc) * (0.5 * tanh_i + 0.5)
        h, carry = _linear_scan(a, u, carry_all[:, c0:c1])
        hcar_scr[:, c0:c1] = carry
        gb = g[:, c0:c1]
        yr_ref[:, c0:c1] = (h * (gb * _sigmoid(gb))).astype(yr_ref.dtype)


def _rnn_projection(x2d, w_in, layer, w_r, w_i, b_r, b_i, lam, convw, convb, *, seq):
    m, k = x2d.shape
    tm = RNN_TM
    nblk = COL_TILE // RNN_BLOCK_DIM
    grid = (RNN_WIDTH // COL_TILE, m // tm)
    gate_w = pl.BlockSpec((None, nblk, RNN_BLOCK_DIM, RNN_BLOCK_DIM), lambda c, i: (layer, c, 0, 0))
    row = pl.BlockSpec((None, 1, COL_TILE), lambda c, i: (layer, 0, c))

    cast_in, cast_out = [], []
    w_rows = k // (grid[0] * grid[1])
    for col0, width in ((Q0, XR0 - Q0), (MA0, MR0 - MA0)):
        cast_in.append(pl.BlockSpec(
            (pl.Element(1), pl.Element(w_rows), pl.Element(width)),
            lambda c, i, col0=col0: (layer, pl.multiple_of((c * grid[1] + i) * w_rows, SUBLANES), col0)))
    n_plain_tiles = PA_WIDTH // PLAIN_TN
    cast_out.append(pl.BlockSpec((n_plain_tiles, w_rows, PLAIN_TN), lambda c, i: (0, c * grid[1] + i, 0)))

    return pl.pallas_call(
        functools.partial(_rnn_proj_kernel, tiles_per_seq=seq // tm),
        grid=grid,
        in_specs=[pl.BlockSpec((tm, k), lambda c, i: (i, 0)),
                  pl.BlockSpec((None, k, COL_TILE), lambda c, i: (layer, 0, XR0 // COL_TILE + c)),
                  pl.BlockSpec((None, k, COL_TILE), lambda c, i: (layer, 0, GR0 // COL_TILE + c)),
                  gate_w, gate_w, row, row, row,
                  pl.BlockSpec((None, CONV_WIDTH, COL_TILE), lambda c, i: (layer, 0, c)),
                  row,
                  pl.BlockSpec((None, k, COL_TILE), lambda c, i: (layer, 0, MR0 // COL_TILE + c),
                               )] + cast_in,
        out_specs=[pl.BlockSpec((tm, COL_TILE), lambda c, i: (i, c))] * 2 + cast_out,
        out_shape=[jax.ShapeDtypeStruct((m, RNN_WIDTH), jnp.bfloat16),
                   jax.ShapeDtypeStruct((m, D_MODEL), jnp.bfloat16)]
                  + [jax.ShapeDtypeStruct((n_plain_tiles, k, PLAIN_TN), jnp.bfloat16)],
        scratch_shapes=[pltpu.VMEM((SUBLANES + tm, COL_TILE), jnp.float32),
                        pltpu.VMEM((1, COL_TILE), jnp.float32)],
        compiler_params=pltpu.CompilerParams(
            dimension_semantics=("arbitrary", "arbitrary"), vmem_limit_bytes=VMEM_LIMIT),
        name="rnn_projection",
    )(x2d, w_in, w_in, w_r, w_i, b_r, b_i, lam, convw, convb, w_in, w_in, w_in)


def _mixer_kernel(relb_ref, sinks_ref,
                  pa_ref, pmr_ref, yr_ref, x_ref, bucket_ref, wao_ref, wro_ref, wo_ref, lng_ref, lnb_ref,
                  out_ref,
                  bias_scr, kx_scr, vx_scr, attn_scr,
                  *, alpha):
    tm = MIX_TM
    nqb = tm // WINDOW
    first_tile = pl.program_id(1) == 0
    f32 = jnp.float32
    bf16 = jnp.bfloat16

    @pl.when((pl.program_id(0) == 0) & first_tile)
    def _():
        bucket = bucket_ref[...]
        for t in range(N_Q_HEADS):
            h = 4 * (t // 4) + 2 * (t % 2) + (t // 2) % 2
            acc = jnp.full((WINDOW, 2 * WINDOW), NEG_INF, f32)
            for b in range(N_BUCKETS):
                acc = jnp.where(bucket == b, relb_ref[b, h], acc)
            bias_scr[t * WINDOW:(t + 1) * WINDOW, :] = acc

    for scr in (kx_scr, vx_scr):
        prev = scr[:, tm:tm + WINDOW, :]
        scr[:, 0:WINDOW, :] = jnp.where(first_tile, jnp.zeros_like(prev), prev)

    u32 = jnp.uint32
    lo_u = lax.broadcasted_iota(jnp.int32, (tm // 2, LANES), 1) < HEAD_DIM
    zero_u = jnp.zeros((tm // 2, LANES), u32)
    for p in range(N_KV_HEADS // 2):
        for src0, scr in ((PA_K0, kx_scr), (PA_V0, vx_scr)):
            t = pltpu.bitcast(pa_ref[:, src0 + p * LANES: src0 + (p + 1) * LANES], u32)
            tr = pltpu.roll(t, HEAD_DIM, axis=1)
            variants = (jnp.where(lo_u, t, zero_u),
                        jnp.where(lo_u, zero_u, tr),
                        jnp.where(lo_u, tr, zero_u),
                        jnp.where(lo_u, zero_u, t))
            for v, val in enumerate(variants):
                scr[len(variants) * p + v, WINDOW:WINDOW + tm, :] = pltpu.bitcast(val, bf16)

    lo_q = lax.broadcasted_iota(jnp.int32, (WINDOW, LANES), 1) < HEAD_DIM
    col = lax.broadcasted_iota(jnp.int32, (1, 2 * WINDOW), 1)
    ones_blk = jnp.ones((2 * WINDOW, LANES), bf16)
    nt_dims = (((1,), (1,)), ((), ()))
    y_rnn_parts = []
    for j in range(nqb):
        r0 = j * WINDOW
        s_parts = []
        for kh in range(N_KV_HEADS):
            qq = jnp.concatenate(
                [pa_ref[r0:r0 + WINDOW, PA_Q0 + (2 * kh + pr) * LANES: PA_Q0 + (2 * kh + pr + 1) * LANES]
                 for pr in range(2)], axis=0) * jnp.asarray(HEAD_DIM ** -0.5, bf16)
            for par in range(2):
                kk = kx_scr[2 * kh + par, r0:r0 + 2 * WINDOW, :]
                s_parts.append(lax.dot_general(qq, kk, nt_dims, preferred_element_type=f32))
        cw = D_MODEL // nqb
        y_rnn_parts.append(jnp.dot(yr_ref[...], wro_ref[:, j * cw:(j + 1) * cw], preferred_element_type=f32))
        s = jnp.concatenate(s_parts, axis=0) + bias_scr[...]
        if j == 0:
            s = s + jnp.where(first_tile & (col < WINDOW), NEG_INF, 0.0).astype(f32)
        m = jnp.max(s, axis=-1, keepdims=True)
        p = jnp.exp(s - m).astype(bf16)
        for kh in range(N_KV_HEADS):
            o_ext = []
            for par in range(2):
                t0 = (2 * kh + par) * 2 * WINDOW
                vv = jnp.concatenate([vx_scr[2 * kh + par, r0:r0 + 2 * WINDOW, :], ones_blk], axis=1)
                o_ext.append(jnp.dot(p[t0:t0 + 2 * WINDOW, :], vv, preferred_element_type=f32))
            for pr in range(2):
                pair = 2 * kh + pr
                rows = slice(pr * WINDOW, (pr + 1) * WINDOW)
                rden = []
                for par in range(2):
                    t0 = (2 * kh + par) * 2 * WINDOW + pr * WINDOW
                    sink_term = jnp.exp(sinks_ref[2 * pair + par] - m[t0:t0 + WINDOW, :])
                    rden.append(1.0 / (o_ext[par][rows, LANES:] + sink_term))
                o = (o_ext[0][rows, :LANES] + o_ext[1][rows, :LANES]) * jnp.where(lo_q, rden[0], rden[1])
                ga = pa_ref[r0:r0 + WINDOW, PA_GA0 + pair * LANES: PA_GA0 + (pair + 1) * LANES].astype(f32)
                attn_scr[r0:r0 + WINDOW, pair * LANES:(pair + 1) * LANES] = (
                    o * (ga * _sigmoid(ga))).astype(bf16)

    y_attn = jnp.dot(attn_scr[...], wao_ref[...], preferred_element_type=f32)
    y_rnn = jnp.concatenate(y_rnn_parts, axis=1)
    mixed = (_sigmoid(pa_ref[:, PA_MA0:PA_MA0 + D_MODEL].astype(f32)) * y_attn
             + _sigmoid(pmr_ref[...].astype(f32)) * y_rnn).astype(bf16)
    zs = []
    row_sum = jnp.zeros((tm, 1), f32)
    for c in range(D_MODEL // COL_TILE):
        k0 = c * COL_TILE
        z = alpha * x_ref[:, k0:k0 + COL_TILE] + jnp.dot(mixed, wo_ref[:, k0:k0 + COL_TILE],
                                                         preferred_element_type=f32)
        row_sum = row_sum + jnp.sum(z, axis=-1, keepdims=True)
        zs.append(z)
    zc = jnp.concatenate(zs, axis=1) - row_sum * (1.0 / D_MODEL)
    var = jnp.mean(zc * zc, axis=-1, keepdims=True)
    out_ref[...] = zc * lax.rsqrt(var + LN_EPS) * lng_ref[...] + lnb_ref[...]


def _const_spec(shape):
    nd = len(shape)
    return pl.BlockSpec(shape, lambda b, s: (0,) * nd, pipeline_mode=pl.Buffered(1))


def _mixer(pa, pmr, yr, x2d, bucket, relb, sinks, wao, wro, wo, lng, lnb, *, batch, seq, alpha):
    tm = MIX_TM
    nst = seq // tm
    row_map = lambda b, s: (b * nst + s, 0)
    smem = pl.BlockSpec(memory_space=pltpu.SMEM)
    in_specs = [
        smem, smem,
        pl.BlockSpec((tm, PA_WIDTH), row_map),
        pl.BlockSpec((tm, D_MODEL), row_map),
        pl.BlockSpec((tm, RNN_WIDTH), row_map),
        pl.BlockSpec((tm, D_MODEL), row_map),
        _const_spec(bucket.shape), _const_spec(wao.shape), _const_spec(wro.shape), _const_spec(wo.shape),
        _const_spec(lng.shape), _const_spec(lnb.shape),
    ]
    scratch = [
        pltpu.VMEM((N_Q_HEADS * WINDOW, 2 * WINDOW), jnp.float32),
        pltpu.VMEM((N_KV_VARIANTS, WINDOW + tm, LANES), jnp.bfloat16),
        pltpu.VMEM((N_KV_VARIANTS, WINDOW + tm, LANES), jnp.bfloat16),
        pltpu.VMEM((tm, ATTN_WIDTH), jnp.bfloat16),
    ]
    return pl.pallas_call(
        functools.partial(_mixer_kernel, alpha=alpha),
        grid=(batch, nst),
        in_specs=in_specs,
        out_specs=pl.BlockSpec((tm, D_MODEL), row_map),
        out_shape=jax.ShapeDtypeStruct((batch * seq, D_MODEL), jnp.float32),
        scratch_shapes=scratch,
        compiler_params=pltpu.CompilerParams(
            dimension_semantics=("arbitrary", "arbitrary"), vmem_limit_bytes=VMEM_LIMIT),
        name="mixer",
    )(relb, sinks, pa, pmr, yr, x2d, bucket, wao, wro, wo, lng, lnb)


def kernel(x, w_in, conv_w, conv_b, w_r, b_r, w_i, b_i, lru_lambda, sinks, w_attn_out, w_rnn_out, w_out,
           ln_g, ln_b, rel_bias):
    batch, seq, d = x.shape
    depth = w_in.shape[0]
    alpha = (2.0 * depth) ** 0.25
    bucket = jnp.asarray(_t5_bucket_table())
    x2d = x.reshape(batch * seq, d)
    as_rows = lambda v: v.reshape(depth, 1, v.shape[-1])
    b_r, b_i, lam, conv_b = as_rows(b_r), as_rows(b_i), as_rows(lru_lambda), as_rows(conv_b)
    for l in range(depth):
        yr, pmr, w_plain = _rnn_projection(x2d, w_in, l, w_r, w_i, b_r, b_i, lam, conv_w, conv_b, seq=seq)
        pa, wao, wro, wo = _plain_projection(x2d, w_plain, w_attn_out, w_rnn_out, w_out, l)
        x2d = _mixer(pa, pmr, yr, x2d, bucket, rel_bias, sinks[l], wao, wro, wo,
                     ln_g[l][None, :], ln_b[l][None, :], batch=batch, seq=seq, alpha=alpha)
    return x2d.reshape(batch, seq, d)
```

```python
import functools
import math

import numpy as np
import jax
import jax.numpy as jnp
from jax import lax
from jax.experimental import pallas as pl
from jax.experimental.pallas import tpu as pltpu

D_MODEL = 2048
N_Q_HEADS = 16
N_KV_HEADS = 4
HEAD_DIM = 64
ATTN_WIDTH = N_Q_HEADS * HEAD_DIM
KV_WIDTH = N_KV_HEADS * HEAD_DIM
WINDOW = 128
N_BUCKETS = 32
MAX_DISTANCE = 128
RNN_WIDTH = D_MODEL
RNN_BLOCKS = 16
RNN_BLOCK_DIM = RNN_WIDTH // RNN_BLOCKS
CONV_WIDTH = 4
LRU_C = 8.0
LN_EPS = 1e-5
NEG_INF = -1e30

Q0 = 0
K0 = Q0 + ATTN_WIDTH
V0 = K0 + KV_WIDTH
GA0 = V0 + KV_WIDTH
XR0 = GA0 + ATTN_WIDTH
GR0 = XR0 + RNN_WIDTH
MA0 = GR0 + RNN_WIDTH
MR0 = MA0 + D_MODEL
IN_WIDTH = MR0 + D_MODEL

LANES = 128
SUBLANES = 8
VMEM_LIMIT = 60 * 1024 * 1024

COL_TILE = 512
PLAIN_TM = 1024
RNN_TM = 1024
MIX_TM = 256

PA_Q0 = 0
PA_K0 = PA_Q0 + ATTN_WIDTH
PA_V0 = PA_K0 + KV_WIDTH
PA_GA0 = PA_V0 + KV_WIDTH
PA_MA0 = PA_GA0 + ATTN_WIDTH
PA_WIDTH = PA_MA0 + D_MODEL
PLAIN_TN = PA_WIDTH // 2
MXU_DIM = 256
N_KV_VARIANTS = 2 * N_KV_HEADS


def _t5_bucket_table():
    qi = np.arange(WINDOW)[:, None]
    kj = np.arange(2 * WINDOW)[None, :]
    dist = qi + WINDOW - kj
    max_exact = N_BUCKETS // 2
    n = np.maximum(dist, 0)
    nf = np.maximum(n, max_exact).astype(np.float32)
    large = max_exact + (np.log(nf / np.float32(max_exact)) / np.float32(math.log(MAX_DISTANCE / max_exact))
                         * np.float32(N_BUCKETS - max_exact)).astype(np.int32)
    large = np.minimum(large, N_BUCKETS - 1)
    bucket = np.where(n < max_exact, n, large)
    in_window = (dist >= 0) & (dist < WINDOW)
    return np.where(in_window, bucket, -1).astype(np.int32)


def _sigmoid(x):
    return 0.5 * jnp.tanh(0.5 * x) + 0.5


def _plain_proj_kernel(relb_ref,
                       x_ref, w_ref, wao_ref, wro_ref, wo_ref, bucket_ref,
                       pa_ref, wao_bf_ref, wro_bf_ref, wo_bf_ref, bias_ref, *, slots_per_step):
    wao_bf_ref[...] = wao_ref[...].astype(jnp.bfloat16)
    wro_bf_ref[...] = wro_ref[...].astype(jnp.bfloat16)
    wo_bf_ref[...] = wo_ref[...].astype(jnp.bfloat16)
    step = pl.program_id(0) * pl.num_programs(1) + pl.program_id(1)
    bucket = bucket_ref[...]
    for i in range(slots_per_step):
        t = step * slots_per_step + i
        h = 4 * (t // 4) + 2 * (t % 2) + (t // 2) % 2
        row = jnp.full(bucket.shape, NEG_INF, jnp.float32)
        for b in range(N_BUCKETS):
            row = jnp.where(bucket == b, relb_ref[b, h], row)
        rows = jnp.concatenate([row] * (WINDOW // SUBLANES), axis=0)
        bias_ref[i * WINDOW:(i + 1) * WINDOW, :] = pltpu.roll(rows, 0, 1, stride=1, stride_axis=0)
    xb = x_ref[...].astype(jnp.bfloat16)
    split = (PLAIN_TN // MXU_DIM + 1) // 2 * MXU_DIM
    for c0, c1 in ((0, split), (split, PLAIN_TN)):
        pa_ref[:, c0:c1] = jnp.dot(xb, w_ref[:, c0:c1], preferred_element_type=jnp.float32).astype(pa_ref.dtype)


def _plain_projection(x2d, w_plain, wao, wro, wo, layer, bucket, relb):
    m, k = x2d.shape
    tm = PLAIN_TM
    n_tiles = w_plain.shape[0]
    n_steps = n_tiles * (m // tm)
    weights = (wao, wro, wo)
    step = lambda n, i: n * (m // tm) + i
    cast_in, cast_out = [], []
    for w in weights:
        rows = w.shape[1] // n_steps
        cast_in.append(pl.BlockSpec((None, rows, w.shape[2]), lambda n, i: (layer, step(n, i), 0)))
        cast_out.append(pl.BlockSpec((rows, w.shape[2]), lambda n, i: (step(n, i), 0)))
    assert N_Q_HEADS % n_steps == 0, (N_Q_HEADS, n_steps)
    slots = N_Q_HEADS // n_steps
    return pl.pallas_call(
        functools.partial(_plain_proj_kernel, slots_per_step=slots),
        grid=(n_tiles, m // tm),
        in_specs=[pl.BlockSpec(memory_space=pltpu.SMEM),
                  pl.BlockSpec((tm, k), lambda n, i: (i, 0)),
                  pl.BlockSpec((None, k, PLAIN_TN), lambda n, i: (n, 0, 0))] + cast_in
                 + [pl.BlockSpec(bucket.shape, lambda n, i: (0, 0))],
        out_specs=[pl.BlockSpec((tm, PLAIN_TN), lambda n, i: (i, n))] + cast_out
                  + [pl.BlockSpec((slots * WINDOW, 2 * WINDOW), lambda n, i: (step(n, i), 0))],
        out_shape=[jax.ShapeDtypeStruct((m, PA_WIDTH), jnp.bfloat16)]
                  + [jax.ShapeDtypeStruct(w.shape[1:], jnp.bfloat16) for w in weights]
                  + [jax.ShapeDtypeStruct((N_Q_HEADS * WINDOW, 2 * WINDOW), jnp.float32)],
        compiler_params=pltpu.CompilerParams(
            dimension_semantics=("arbitrary", "arbitrary"), vmem_limit_bytes=VMEM_LIMIT),
        name="plain_projection",
    )(relb, x2d, w_plain, wao, wro, wo, bucket)


def _linear_scan(a, u, carry):
    tm = a.shape[0]
    groups = tm // SUBLANES
    a3 = a.reshape(groups, SUBLANES, LANES)
    u3 = u.reshape(groups, SUBLANES, LANES)
    row = lax.broadcasted_iota(jnp.int32, (groups, SUBLANES, LANES), 1)
    for d in (1, 2, 4):
        keep = row >= d
        a_sh = pltpu.roll(a3, d, axis=1)
        u_sh = pltpu.roll(u3, d, axis=1)
        u3 = jnp.where(keep, a3 * u_sh + u3, u3)
        a3 = jnp.where(keep, a3 * a_sh, a3)
    hs = []
    for g in range(groups):
        hg = u3[g] + a3[g] * carry
        hs.append(hg)
        carry = hg[SUBLANES - 1:SUBLANES, :]
    return jnp.concatenate(hs, axis=0), carry


def _rnn_proj_kernel(x_ref, wx_ref, wg_ref, wr_ref, wi_ref, br_ref, bi_ref, lam_ref, convw_ref, convb_ref,
                     wmr_ref, wpa_ref, wpm_ref,
                     yr_ref, pmr_ref, wp_bf_ref, hist_scr, hcar_scr,
                     *, tiles_per_seq):
    tm = RNN_TM
    f32 = jnp.float32
    bf16 = jnp.bfloat16
    first = (pl.program_id(1) % tiles_per_seq) == 0

    hist_scr[0:SUBLANES, :] = jnp.where(first, 0.0, hist_scr[tm:tm + SUBLANES, :])
    carry_all = jnp.where(first, 0.0, hcar_scr[...])

    xb = x_ref[...].astype(bf16)
    hist_scr[SUBLANES:SUBLANES + tm, :] = jnp.dot(xb, wx_ref[...].astype(bf16), preferred_element_type=f32)
    w_plain = jnp.concatenate([wpa_ref[0], wpm_ref[0]], axis=1).astype(bf16)
    for t in range(PA_WIDTH // PLAIN_TN):
        wp_bf_ref[t] = w_plain[:, t * PLAIN_TN:(t + 1) * PLAIN_TN]
    wg = wg_ref[...].astype(bf16)
    wmr = wmr_ref[...].astype(bf16)
    g_top = jnp.dot(xb[:tm // 2], wg, preferred_element_type=f32)

    neg_lam = -lam_ref[...]
    half_scale = (-0.5 * LRU_C) * (jnp.maximum(neg_lam, 0.0) + jnp.log1p(jnp.exp(-jnp.abs(neg_lam))))
    half_br = 0.5 * br_ref[...]
    half_bi = 0.5 * bi_ref[...]
    row8 = lax.broadcasted_iota(jnp.int32, (SUBLANES, RNN_BLOCK_DIM), 0)
    for b in range(COL_TILE // RNN_BLOCK_DIM):
        c0 = b * RNN_BLOCK_DIM
        c1 = c0 + RNN_BLOCK_DIM
        x_cur = hist_scr[SUBLANES:SUBLANES + tm, c0:c1]
        x_old = hist_scr[0:SUBLANES, c0:c1]
        acc, acc_old = None, None
        for kk in range(CONV_WIDTH - 1):
            wk = convw_ref[kk:kk + 1, c0:c1]
            acc = wk * x_cur if acc is None else wk * x_cur + acc
            acc_old = wk * x_old if acc_old is None else wk * x_old + acc_old
            acc_old = pltpu.roll(acc_old, 1, axis=0)
            shifted = pltpu.roll(acc, 1, axis=0)
            top = jnp.where(row8 == 0, acc_old, shifted[0:SUBLANES, :])
            acc = jnp.concatenate([top, shifted[SUBLANES:, :]], axis=0)
        xc = convw_ref[CONV_WIDTH - 1:CONV_WIDTH, c0:c1] * x_cur + acc + convb_ref[:, c0:c1]
        w_gates = (0.5 * jnp.concatenate([wr_ref[b], wi_ref[b]], axis=1)).astype(bf16)
        ri = jnp.dot(xc.astype(bf16), w_gates, preferred_element_type=f32)
        if b == 0:
            g = jnp.concatenate([g_top, jnp.dot(xb[tm // 2:], wg, preferred_element_type=f32)], axis=0)
        elif b <= 2:
            rows = slice((b - 1) * tm // 2, b * tm // 2)
            pmr_ref[rows, :] = jnp.dot(xb[rows], wmr, preferred_element_type=f32).astype(pmr_ref.dtype)
        tanh_r = jnp.tanh(ri[:, :RNN_BLOCK_DIM] + half_br[:, c0:c1])
        tanh_i = jnp.tanh(ri[:, RNN_BLOCK_DIM:] + half_bi[:, c0:c1])
        log_a = tanh_r * half_scale[:, c0:c1] + half_scale[:, c0:c1]
        a = jnp.exp(log_a)
        one_m_a2 = jnp.tanh(log_a) * (-1.0 - a * a)
        mult = jnp.where(one_m_a2 > 0.0, one_m_a2 * lax.rsqrt(one_m_a2), 0.0)
        u = (mult * xc) * (0.5 * tanh_i + 0.5)
        h, carry = _linear_scan(a, u, carry_all[:, c0:c1])
        hcar_scr[:, c0:c1] = carry
        gb = g[:, c0:c1]
        yr_ref[:, c0:c1] = (h * (gb * _sigmoid(gb))).astype(yr_ref.dtype)


def _rnn_projection(x2d, w_in, layer, w_r, w_i, b_r, b_i, lam, convw, convb, *, seq):
    m, k = x2d.shape
    tm = RNN_TM
    nblk = COL_TILE // RNN_BLOCK_DIM
    grid = (RNN_WIDTH // COL_TILE, m // tm)
    gate_w = pl.BlockSpec((None, nblk, RNN_BLOCK_DIM, RNN_BLOCK_DIM), lambda c, i: (layer, c, 0, 0))
    row = pl.BlockSpec((None, 1, COL_TILE), lambda c, i: (layer, 0, c))

    cast_in, cast_out = [], []
    w_rows = k // (grid[0] * grid[1])
    for col0, width in ((Q0, XR0 - Q0), (MA0, MR0 - MA0)):
        cast_in.append(pl.BlockSpec(
            (pl.Element(1), pl.Element(w_rows), pl.Element(width)),
            lambda c, i, col0=col0: (layer, pl.multiple_of((c * grid[1] + i) * w_rows, SUBLANES), col0)))
    n_plain_tiles = PA_WIDTH // PLAIN_TN
    cast_out.append(pl.BlockSpec((n_plain_tiles, w_rows, PLAIN_TN), lambda c, i: (0, c * grid[1] + i, 0)))

    return pl.pallas_call(
        functools.partial(_rnn_proj_kernel, tiles_per_seq=seq // tm),
        grid=grid,
        in_specs=[pl.BlockSpec((tm, k), lambda c, i: (i, 0)),
                  pl.BlockSpec((None, k, COL_TILE), lambda c, i: (layer, 0, XR0 // COL_TILE + c)),
                  pl.BlockSpec((None, k, COL_TILE), lambda c, i: (layer, 0, GR0 // COL_TILE + c)),
                  gate_w, gate_w, row, row, row,
                  pl.BlockSpec((None, CONV_WIDTH, COL_TILE), lambda c, i: (layer, 0, c)),
                  row,
                  pl.BlockSpec((None, k, COL_TILE), lambda c, i: (layer, 0, MR0 // COL_TILE + c),
                               )] + cast_in,
        out_specs=[pl.BlockSpec((tm, COL_TILE), lambda c, i: (i, c))] * 2 + cast_out,
        out_shape=[jax.ShapeDtypeStruct((m, RNN_WIDTH), jnp.bfloat16),
                   jax.ShapeDtypeStruct((m, D_MODEL), jnp.bfloat16)]
                  + [jax.ShapeDtypeStruct((n_plain_tiles, k, PLAIN_TN), jnp.bfloat16)],
        scratch_shapes=[pltpu.VMEM((SUBLANES + tm, COL_TILE), jnp.float32),
                        pltpu.VMEM((1, COL_TILE), jnp.float32)],
        compiler_params=pltpu.CompilerParams(
            dimension_semantics=("arbitrary", "arbitrary"), vmem_limit_bytes=VMEM_LIMIT),
        name="rnn_projection",
    )(x2d, w_in, w_in, w_r, w_i, b_r, b_i, lam, convw, convb, w_in, w_in, w_in)


def _mixer_kernel(sinks_ref,
                  pa_ref, pmr_ref, yr_ref, x_ref, bias_ref, wao_ref, wro_ref, wo_ref, lng_ref, lnb_ref,
                  out_ref,
                  kx_scr, vx_scr, attn_scr,
                  *, alpha):
    tm = MIX_TM
    nqb = tm // WINDOW
    first_tile = pl.program_id(1) == 0
    f32 = jnp.float32
    bf16 = jnp.bfloat16

    for scr in (kx_scr, vx_scr):
        prev = scr[:, tm:tm + WINDOW, :]
        scr[:, 0:WINDOW, :] = jnp.where(first_tile, jnp.zeros_like(prev), prev)

    u32 = jnp.uint32
    lo_u = lax.broadcasted_iota(jnp.int32, (tm // 2, LANES), 1) < HEAD_DIM
    zero_u = jnp.zeros((tm // 2, LANES), u32)
    for p in range(N_KV_HEADS // 2):
        for src0, scr in ((PA_K0, kx_scr), (PA_V0, vx_scr)):
            t = pltpu.bitcast(pa_ref[:, src0 + p * LANES: src0 + (p + 1) * LANES], u32)
            tr = pltpu.roll(t, HEAD_DIM, axis=1)
            variants = (jnp.where(lo_u, t, zero_u),
                        jnp.where(lo_u, zero_u, tr),
                        jnp.where(lo_u, tr, zero_u),
                        jnp.where(lo_u, zero_u, t))
            for v, val in enumerate(variants):
                scr[len(variants) * p + v, WINDOW:WINDOW + tm, :] = pltpu.bitcast(val, bf16)

    lo_q = lax.broadcasted_iota(jnp.int32, (WINDOW, LANES), 1) < HEAD_DIM
    col = lax.broadcasted_iota(jnp.int32, (1, 2 * WINDOW), 1)
    ones_blk = jnp.ones((2 * WINDOW, LANES), bf16)
    nt_dims = (((1,), (1,)), ((), ()))
    y_rnn_parts = []
    for j in range(nqb):
        r0 = j * WINDOW
        s_parts = []
        for kh in range(N_KV_HEADS):
            qq = jnp.concatenate(
                [pa_ref[r0:r0 + WINDOW, PA_Q0 + (2 * kh + pr) * LANES: PA_Q0 + (2 * kh + pr + 1) * LANES]
                 for pr in range(2)], axis=0) * jnp.asarray(HEAD_DIM ** -0.5, bf16)
            for par in range(2):
                kk = kx_scr[2 * kh + par, r0:r0 + 2 * WINDOW, :]
                s_parts.append(lax.dot_general(qq, kk, nt_dims, preferred_element_type=f32))
        cw = D_MODEL // nqb
        y_rnn_parts.append(jnp.dot(yr_ref[...], wro_ref[:, j * cw:(j + 1) * cw], preferred_element_type=f32))
        s = jnp.concatenate(s_parts, axis=0) + bias_ref[...]
        if j == 0:
            s = s + jnp.where(first_tile & (col < WINDOW), NEG_INF, 0.0).astype(f32)
        m = jnp.max(s, axis=-1, keepdims=True)
        p = jnp.exp(s - m).astype(bf16)
        for kh in range(N_KV_HEADS):
            o_ext = []
            for par in range(2):
                t0 = (2 * kh + par) * 2 * WINDOW
                vv = jnp.concatenate([vx_scr[2 * kh + par, r0:r0 + 2 * WINDOW, :], ones_blk], axis=1)
                o_ext.append(jnp.dot(p[t0:t0 + 2 * WINDOW, :], vv, preferred_element_type=f32))
            for pr in range(2):
                pair = 2 * kh + pr
                rows = slice(pr * WINDOW, (pr + 1) * WINDOW)
                rden = []
                for par in range(2):
                    t0 = (2 * kh + par) * 2 * WINDOW + pr * WINDOW
                    sink_term = jnp.exp(sinks_ref[2 * pair + par] - m[t0:t0 + WINDOW, :])
                    rden.append(1.0 / (o_ext[par][rows, LANES:] + sink_term))
                o = (o_ext[0][rows, :LANES] + o_ext[1][rows, :LANES]) * jnp.where(lo_q, rden[0], rden[1])
                ga = pa_ref[r0:r0 + WINDOW, PA_GA0 + pair * LANES: PA_GA0 + (pair + 1) * LANES].astype(f32)
                attn_scr[r0:r0 + WINDOW, pair * LANES:(pair + 1) * LANES] = (
                    o * (ga * _sigmoid(ga))).astype(bf16)

    y_attn = jnp.dot(attn_scr[...], wao_ref[...], preferred_element_type=f32)
    y_rnn = jnp.concatenate(y_rnn_parts, axis=1)
    mixed = (_sigmoid(pa_ref[:, PA_MA0:PA_MA0 + D_MODEL].astype(f32)) * y_attn
             + _sigmoid(pmr_ref[...].astype(f32)) * y_rnn).astype(bf16)
    zs = []
    row_sum = jnp.zeros((tm, 1), f32)
    for c in range(D_MODEL // COL_TILE):
        k0 = c * COL_TILE
        z = alpha * x_ref[:, k0:k0 + COL_TILE] + jnp.dot(mixed, wo_ref[:, k0:k0 + COL_TILE],
                                                         preferred_element_type=f32)
        row_sum = row_sum + jnp.sum(z, axis=-1, keepdims=True)
        zs.append(z)
    zc = jnp.concatenate(zs, axis=1) - row_sum * (1.0 / D_MODEL)
    var = jnp.mean(zc * zc, axis=-1, keepdims=True)
    out_ref[...] = zc * lax.rsqrt(var + LN_EPS) * lng_ref[...] + lnb_ref[...]


def _const_spec(shape):
    nd = len(shape)
    return pl.BlockSpec(shape, lambda b, s: (0,) * nd, pipeline_mode=pl.Buffered(1))


def _mixer(pa, pmr, yr, x2d, bias, sinks, wao, wro, wo, lng, lnb, *, batch, seq, alpha):
    tm = MIX_TM
    nst = seq // tm
    row_map = lambda b, s: (b * nst + s, 0)
    in_specs = [
        pl.BlockSpec(memory_space=pltpu.SMEM),
        pl.BlockSpec((tm, PA_WIDTH), row_map),
        pl.BlockSpec((tm, D_MODEL), row_map),
        pl.BlockSpec((tm, RNN_WIDTH), row_map),
        pl.BlockSpec((tm, D_MODEL), row_map),
        _const_spec(bias.shape), _const_spec(wao.shape), _const_spec(wro.shape), _const_spec(wo.shape),
        _const_spec(lng.shape), _const_spec(lnb.shape),
    ]
    scratch = [
        pltpu.VMEM((N_KV_VARIANTS, WINDOW + tm, LANES), jnp.bfloat16),
        pltpu.VMEM((N_KV_VARIANTS, WINDOW + tm, LANES), jnp.bfloat16),
        pltpu.VMEM((tm, ATTN_WIDTH), jnp.bfloat16),
    ]
    return pl.pallas_call(
        functools.partial(_mixer_kernel, alpha=alpha),
        grid=(batch, nst),
        in_specs=in_specs,
        out_specs=pl.BlockSpec((tm, D_MODEL), row_map),
        out_shape=jax.ShapeDtypeStruct((batch * seq, D_MODEL), jnp.float32),
        scratch_shapes=scratch,
        compiler_params=pltpu.CompilerParams(
            dimension_semantics=("arbitrary", "arbitrary"), vmem_limit_bytes=VMEM_LIMIT),
        name="mixer",
    )(sinks, pa, pmr, yr, x2d, bias, wao, wro, wo, lng, lnb)


def kernel(x, w_in, conv_w, conv_b, w_r, b_r, w_i, b_i, lru_lambda, sinks, w_attn_out, w_rnn_out, w_out,
           ln_g, ln_b, rel_bias):
    batch, seq, d = x.shape
    depth = w_in.shape[0]
    alpha = (2.0 * depth) ** 0.25
    table = _t5_bucket_table()
    assert all(np.array_equal(table[q], np.roll(table[0], q)) for q in range(WINDOW)), "bias table is not Toeplitz"
    bucket = jnp.asarray(np.broadcast_to(table[0], (SUBLANES, 2 * WINDOW)))
    x2d = x.reshape(batch * seq, d)
    as_rows = lambda v: v.reshape(depth, 1, v.shape[-1])
    b_r, b_i, lam, conv_b = as_rows(b_r), as_rows(b_i), as_rows(lru_lambda), as_rows(conv_b)
    for l in range(depth):
        yr, pmr, w_plain = _rnn_projection(x2d, w_in, l, w_r, w_i, b_r, b_i, lam, conv_w, conv_b, seq=seq)
        pa, wao, wro, wo, bias = _plain_projection(x2d, w_plain, w_attn_out, w_rnn_out, w_out, l, bucket, rel_bias)
        x2d = _mixer(pa, pmr, yr, x2d, bias, sinks[l], wao, wro, wo,
                     ln_g[l][None, :], ln_b[l][None, :], batch=batch, seq=seq, alpha=alpha)
    return x2d.reshape(batch, seq, d)
```
